```python
import math
import jax, jax.numpy as jnp
from jax import lax
import numpy as np

D_MODEL = 2048
BATCH = 4
SEQ = 2048
DEPTH = 2
DEC_BATCH = 128
DEC_SEQ = 4
PAST_LEN = 16384
PAGE_SIZE = 128

MIX_WIDTH = D_MODEL
RW_WIDTH = MIX_WIDTH // 2
RW_HEAD = 64
RW_HEADS = RW_WIDTH // RW_HEAD
RW_DECAY_LORA = D_MODEL // 32
RW_AAA_LORA = D_MODEL // 32
RW_GATE_LORA = D_MODEL // 16
RW_COLS = 3 * RW_WIDTH + RW_DECAY_LORA + RW_AAA_LORA + RW_GATE_LORA
GD_WIDTH = MIX_WIDTH - RW_WIDTH
GD_HEAD = 128
GD_HEADS = GD_WIDTH // GD_HEAD
GD_CONV = 4
GD_CHUNK = 64
GD_COLS = 4 * GD_WIDTH + 2 * GD_HEADS
IN_COLS = RW_COLS + GD_COLS
N_MEM = 256
XA_HEADS = 4
XA_HEAD = 128
XA_WIDTH = XA_HEADS * XA_HEAD
D_FF = -(-8 * D_MODEL // (3 * 256)) * 256
DEEPNORM_ALPHA = (2 * DEPTH) ** 0.25
DEEPNORM_BETA = (8 * DEPTH) ** -0.25
LN_EPS = 1e-5
RW_GN_EPS = 64e-5
GD_NORM_EPS = 1e-6

kernel_name = 'hybrid_rwkv7_gdn_memxattn_step'


def _layer_norm(x, g, b):
    xf = x.astype(jnp.float32)
    mu = jnp.mean(xf, -1, keepdims=True)
    var = jnp.mean(jnp.square(xf - mu), -1, keepdims=True)
    return ((xf - mu) * lax.rsqrt(var + LN_EPS) * g + b).astype(x.dtype)


def _l2norm(t):
    return t * lax.rsqrt(jnp.sum(t * t, -1, keepdims=True) + 1e-12)


def _rwkv7_mix(p, prev, S0, mu, w0, w2, a0, a2, g2, k_k, k_a, r_k, lnx_w, lnx_b):
    f32 = jnp.float32
    B, T, _ = p.shape
    p = p.astype(f32)
    shifted = jnp.concatenate([prev.astype(f32)[:, None], p[:, :-1]], axis=1)
    m = p + (shifted - p) * mu
    c0, c1, c2, c3 = RW_WIDTH, 2 * RW_WIDTH, 3 * RW_WIDTH, 3 * RW_WIDTH + RW_DECAY_LORA
    c4 = c3 + RW_AAA_LORA
    r, k, v = m[..., :c0], m[..., c0:c1], m[..., c1:c2]
    wd, ad, gd = m[..., c2:c3], m[..., c3:c4], m[..., c4:]
    w_log = -jax.nn.softplus(-(w0 + jnp.tanh(wd) @ w2)) - 0.5
    decay = jnp.exp(-jnp.exp(w_log))
    a = jax.nn.sigmoid(a0 + ad @ a2)
    g = jax.nn.sigmoid(gd) @ g2
    hs = lambda t: t.reshape(B, T, RW_HEADS, RW_HEAD)
    kk = _l2norm(hs(k * k_k))
    k = k * (1.0 + (a - 1.0) * k_a)
    r, k, v, decay, a = hs(r), hs(k), hs(v), hs(decay), hs(a)
    in_a = -kk
    in_b = kk * a

    def step(S, inp):
        r_t, w_t, k_t, v_t, a_t, b_t = inp
        Sa = jnp.einsum('bhvk,bhk->bhv', S, a_t)
        S = S * w_t[:, :, None, :] + Sa[..., None] * b_t[:, :, None, :] + v_t[..., None] * k_t[:, :, None, :]
        return S, jnp.einsum('bhvk,bhk->bhv', S, r_t)

    tm = lambda t: jnp.swapaxes(t, 0, 1)
    S, o = lax.scan(step, S0.astype(f32), (tm(r), tm(decay), tm(k), tm(v), tm(in_a), tm(in_b)))
    o = tm(o)
    o_mu = jnp.mean(o, -1, keepdims=True)
    o_var = jnp.mean(jnp.square(o - o_mu), -1, keepdims=True)
    o = ((o - o_mu) * lax.rsqrt(o_var + RW_GN_EPS)).reshape(B, T, RW_WIDTH) * lnx_w + lnx_b
    bonus = jnp.sum(r * k * r_k, -1, keepdims=True) * v
    o = (o + bonus.reshape(B, T, RW_WIDTH)) * g
    return o, S, p[:, -1]


def _chunk_gated_delta(q, k, v, beta, g, S0):
    B, T, H, Dk = q.shape
    Dv = v.shape[-1]
    C = min(GD_CHUNK, T)
    n = -(-T // C)
    pad = n * C - T

    def blk(t):
        t = jnp.pad(t, [(0, 0), (0, pad)] + [(0, 0)] * (t.ndim - 2))
        t = jnp.moveaxis(t, 2, 1)
        return t.reshape(t.shape[:2] + (n, C) + t.shape[3:])

    q, k, v, beta, g = blk(q), blk(k), blk(v), blk(beta), blk(g)
    G = jnp.cumsum(g, -1)
    idx = jnp.arange(C)
    causal = idx[:, None] >= idx[None, :]
    strict = idx[:, None] > idx[None, :]
    decay = jnp.exp(jnp.where(causal, G[..., :, None] - G[..., None, :], -jnp.inf))
    kb = k * beta[..., None]
    L = jnp.where(strict, jnp.einsum('bhnik,bhnjk->bhnij', kb, k) * decay, 0.0)
    A = L + jnp.eye(C, dtype=L.dtype)
    rhs = jnp.concatenate([v * beta[..., None], kb * jnp.exp(G)[..., None]], -1)
    sol = lax.linalg.triangular_solve(A, rhs, left_side=True, lower=True, unit_diagonal=True)
    u, w = sol[..., :Dv], sol[..., Dv:]
    attn = jnp.einsum('bhnik,bhnjk->bhnij', q, k) * decay
    qg = q * jnp.exp(G)[..., None]
    kg = k * jnp.exp(G[..., -1:] - G)[..., None]
    gl = jnp.exp(G[..., -1])

    def step(S, inp):
        qg_c, kg_c, u_c, w_c, attn_c, gl_c = inp
        v_new = u_c - jnp.einsum('bhik,bhkv->bhiv', w_c, S)
        o = jnp.einsum('bhik,bhkv->bhiv', qg_c, S) + jnp.einsum('bhij,bhjv->bhiv', attn_c, v_new)
        S = S * gl_c[..., None, None] + jnp.einsum('bhjk,bhjv->bhkv', kg_c, v_new)
        return S, o

    mv = lambda t: jnp.moveaxis(t, 2, 0)
    S, o = lax.scan(step, S0, (mv(qg), mv(kg), mv(u), mv(w), mv(attn), mv(gl)))
    o = jnp.moveaxis(o, 0, 2).reshape(B, H, n * C, Dv)[:, :, :T]
    return jnp.moveaxis(o, 1, 2), S


def _gdn_mix(p, conv_prev, S0, conv_w, a_log, dt_bias, norm_w):
    f32 = jnp.float32
    B, T, _ = p.shape
    p = p.astype(f32)
    W3 = 3 * GD_WIDTH
    qkv = p[..., :W3]
    z = p[..., W3:W3 + GD_WIDTH]
    b_raw = p[..., W3 + GD_WIDTH:W3 + GD_WIDTH + GD_HEADS]
    a_raw = p[..., W3 + GD_WIDTH + GD_HEADS:]
    xp = jnp.concatenate([conv_prev.astype(f32), qkv], axis=1)
    conv = xp[:, 0:T] * conv_w[0]
    for j in range(1, GD_CONV):
        conv = conv + xp[:, j:j + T] * conv_w[j]
    qkv = jax.nn.silu(conv)
    hs = lambda t: t.reshape(B, T, GD_HEADS, GD_HEAD)
    q = _l2norm(hs(qkv[..., :GD_WIDTH])) * (GD_HEAD ** -0.5)
    k = _l2norm(hs(qkv[..., GD_WIDTH:2 * GD_WIDTH]))
    v = hs(qkv[..., 2 * GD_WIDTH:])
    beta = jax.nn.sigmoid(b_raw)
    g = -jnp.exp(a_log) * jax.nn.softplus(a_raw + dt_bias)
    o, S = _chunk_gated_delta(q, k, v, beta, g, S0.astype(f32))
    o = o * lax.rsqrt(jnp.mean(o * o, -1, keepdims=True) + GD_NORM_EPS) * norm_w * jax.nn.silu(hs(z))
    return o.reshape(B, T, GD_WIDTH), S, xp[:, -(GD_CONV - 1):]


def _mem_kv(mem, wk, wv):
    B = mem.shape[0]
    k = (mem @ wk).reshape(B, N_MEM, XA_HEADS, XA_HEAD)
    v = (mem @ wv).reshape(B, N_MEM, XA_HEADS, XA_HEAD)
    return k, v


def _cross_attn(x, mk, mv, wq, wo):
    B, T, _ = x.shape
    q = (x @ wq).reshape(B, T, XA_HEADS, XA_HEAD)
    s = jnp.einsum('bthd,bmhd->bhtm', q, mk).astype(jnp.float32) * (XA_HEAD ** -0.5)
    pr = jax.nn.softmax(s, axis=-1).astype(mv.dtype)
    o = jnp.einsum('bhtm,bmhd->bthd', pr, mv).reshape(B, T, XA_WIDTH).astype(x.dtype)
    return o @ wo


def _decoder_layer(x, mk, mv, rw_S, shift_prev, gd_S, conv_prev, lp):
    p = x @ lp['w_in']
    o_rw, rw_S, shift_new = _rwkv7_mix(p[..., :RW_COLS], shift_prev, rw_S, lp['mu_shift'], lp['rw_w0'], lp['rw_w2'],
                                       lp['rw_a0'], lp['rw_a2'], lp['rw_g2'], lp['rw_kk'], lp['rw_ka'], lp['rw_rk'],
                                       lp['rw_lnx_w'], lp['rw_lnx_b'])
    o_gd, gd_S, conv_new = _gdn_mix(p[..., RW_COLS:], conv_prev, gd_S, lp['gd_conv_w'], lp['gd_a_log'],
                                    lp['gd_dt_bias'], lp['gd_norm_w'])
    h = jnp.concatenate([o_rw, o_gd], -1).astype(x.dtype) @ lp['w_out']
    x = _layer_norm(DEEPNORM_ALPHA * x + h, lp['ln1_g'], lp['ln1_b'])
    x = _layer_norm(DEEPNORM_ALPHA * x + _cross_attn(x, mk, mv, lp['xa_wq'], lp['xa_wo']), lp['ln2_g'], lp['ln2_b'])
    f = (jax.nn.silu(x @ lp['ff_wg']) * (x @ lp['ff_wu'])) @ lp['ff_wd']
    x = _layer_norm(DEEPNORM_ALPHA * x + f, lp['ln3_g'], lp['ln3_b'])
    return x, rw_S, shift_new, gd_S, conv_new


def setup_inputs(seed: int = 0) -> dict:
    key = jax.random.key(seed)
    ks = iter(jax.random.split(key, 48))
    f32 = jnp.float32

    def nrm(shape, scale):
        return jax.random.normal(next(ks), shape, f32) * scale

    def unif(shape, lo, hi):
        return jax.random.uniform(next(ks), shape, f32, lo, hi)

    dt = jnp.exp(unif((DEPTH, GD_HEADS), math.log(1e-3), math.log(1e-1)))
    return {
        'x_prompt': nrm((BATCH, SEQ, D_MODEL), 1.0),
        'mem_prompt': nrm((BATCH, N_MEM, D_MODEL), 1.0),
        'x_sample': nrm((DEC_BATCH, DEC_SEQ, D_MODEL), 1.0),
        'state_rwkv': nrm((DEPTH, DEC_BATCH, RW_HEADS, RW_HEAD, RW_HEAD), 0.1),
        'state_shift': nrm((DEPTH, DEC_BATCH, RW_COLS), 1.0),
        'state_gdn': nrm((DEPTH, DEC_BATCH, GD_HEADS, GD_HEAD, GD_HEAD), 0.1),
        'state_conv': nrm((DEPTH, DEC_BATCH, GD_CONV - 1, 3 * GD_WIDTH), 1.0),
        'cache_mem_k': nrm((DEPTH, DEC_BATCH, N_MEM, XA_HEADS, XA_HEAD), 1.0),
        'cache_mem_v': nrm((DEPTH, DEC_BATCH, N_MEM, XA_HEADS, XA_HEAD), 1.0),
        'w_in': nrm((DEPTH, D_MODEL, IN_COLS), D_MODEL ** -0.5),
        'mu_shift': unif((DEPTH, RW_COLS), 0.0, 1.0),
        'rw_w0': unif((DEPTH, RW_WIDTH), -5.0, 1.0),
        'rw_w2': nrm((DEPTH, RW_DECAY_LORA, RW_WIDTH), 0.5 * RW_DECAY_LORA ** -0.5),
        'rw_a0': nrm((DEPTH, RW_WIDTH), 0.1),
        'rw_a2': nrm((DEPTH, RW_AAA_LORA, RW_WIDTH), RW_AAA_LORA ** -0.5),
        'rw_g2': nrm((DEPTH, RW_GATE_LORA, RW_WIDTH), RW_GATE_LORA ** -0.5),
        'rw_kk': 0.85 + nrm((DEPTH, RW_WIDTH), 0.02),
        'rw_ka': 1.0 + nrm((DEPTH, RW_WIDTH), 0.02),
        'rw_rk': nrm((DEPTH, RW_HEADS, RW_HEAD), 0.1),
        'rw_lnx_w': 1.0 + nrm((DEPTH, RW_WIDTH), 0.02),
        'rw_lnx_b': nrm((DEPTH, RW_WIDTH), 0.02),
        'gd_conv_w': nrm((DEPTH, GD_CONV, 3 * GD_WIDTH), GD_CONV ** -0.5),
        'gd_a_log': jnp.log(unif((DEPTH, GD_HEADS), 1.0, 16.0)),
        'gd_dt_bias': dt + jnp.log(-jnp.expm1(-dt)),
        'gd_norm_w': 1.0 + nrm((DEPTH, GD_HEAD), 0.02),
        'w_out': nrm((DEPTH, MIX_WIDTH, D_MODEL), MIX_WIDTH ** -0.5 * DEEPNORM_BETA),
        'ln1_g': 1.0 + nrm((DEPTH, D_MODEL), 0.02),
        'ln1_b': nrm((DEPTH, D_MODEL), 0.02),
        'xa_wq': nrm((DEPTH, D_MODEL, XA_WIDTH), D_MODEL ** -0.5),
        'xa_wk': nrm((DEPTH, D_MODEL, XA_WIDTH), D_MODEL ** -0.5),
        'xa_wv': nrm((DEPTH, D_MODEL, XA_WIDTH), D_MODEL ** -0.5),
        'xa_wo': nrm((DEPTH, XA_WIDTH, D_MODEL), XA_WIDTH ** -0.5 * DEEPNORM_BETA),
        'ln2_g': 1.0 + nrm((DEPTH, D_MODEL), 0.02),
        'ln2_b': nrm((DEPTH, D_MODEL), 0.02),
        'ff_wg': nrm((DEPTH, D_MODEL, D_FF), D_MODEL ** -0.5),
        'ff_wu': nrm((DEPTH, D_MODEL, D_FF), D_MODEL ** -0.5),
        'ff_wd': nrm((DEPTH, D_FF, D_MODEL), D_FF ** -0.5 * DEEPNORM_BETA),
        'ln3_g': 1.0 + nrm((DEPTH, D_MODEL), 0.02),
        'ln3_b': nrm((DEPTH, D_MODEL), 0.02),
    }


def reference(x_prompt, mem_prompt, x_sample, state_rwkv, state_shift, state_gdn, state_conv, cache_mem_k,
              cache_mem_v, w_in, mu_shift, rw_w0, rw_w2, rw_a0, rw_a2, rw_g2, rw_kk, rw_ka, rw_rk, rw_lnx_w,
              rw_lnx_b, gd_conv_w, gd_a_log, gd_dt_bias, gd_norm_w, w_out, ln1_g, ln1_b, xa_wq, xa_wk, xa_wv,
              xa_wo, ln2_g, ln2_b, ff_wg, ff_wu, ff_wd, ln3_g, ln3_b):
    f32 = jnp.float32
    Bp = x_prompt.shape[0]
    dtp = x_prompt.dtype
    yp, ys = x_prompt, x_sample
    p_rw, p_sh, p_gd, p_cv, p_mk, p_mv = [], [], [], [], [], []
    s_rw, s_sh, s_gd, s_cv = [], [], [], []
    for l in range(DEPTH):
        lp = {
            'w_in': w_in[l], 'mu_shift': mu_shift[l], 'rw_w0': rw_w0[l], 'rw_w2': rw_w2[l], 'rw_a0': rw_a0[l],
            'rw_a2': rw_a2[l], 'rw_g2': rw_g2[l], 'rw_kk': rw_kk[l], 'rw_ka': rw_ka[l], 'rw_rk': rw_rk[l],
            'rw_lnx_w': rw_lnx_w[l], 'rw_lnx_b': rw_lnx_b[l], 'gd_conv_w': gd_conv_w[l], 'gd_a_log': gd_a_log[l],
            'gd_dt_bias': gd_dt_bias[l], 'gd_norm_w': gd_norm_w[l], 'w_out': w_out[l], 'ln1_g': ln1_g[l],
            'ln1_b': ln1_b[l], 'xa_wq': xa_wq[l], 'xa_wo': xa_wo[l], 'ln2_g': ln2_g[l], 'ln2_b': ln2_b[l],
            'ff_wg': ff_wg[l], 'ff_wu': ff_wu[l], 'ff_wd': ff_wd[l], 'ln3_g': ln3_g[l], 'ln3_b': ln3_b[l],
        }
        mk, mv = _mem_kv(mem_prompt, xa_wk[l], xa_wv[l])
        yp, a1, a2, a3, a4 = _decoder_layer(
            yp, mk, mv,
            jnp.zeros((Bp, RW_HEADS, RW_HEAD, RW_HEAD), f32), jnp.zeros((Bp, RW_COLS), f32),
            jnp.zeros((Bp, GD_HEADS, GD_HEAD, GD_HEAD), f32), jnp.zeros((Bp, GD_CONV - 1, 3 * GD_WIDTH), f32), lp)
        p_rw.append(a1.astype(dtp)); p_sh.append(a2.astype(dtp)); p_gd.append(a3.astype(dtp))
        p_cv.append(a4.astype(dtp)); p_mk.append(mk); p_mv.append(mv)
        ys, b1, b2, b3, b4 = _decoder_layer(
            ys, cache_mem_k[l], cache_mem_v[l], state_rwkv[l], state_shift[l], state_gdn[l], state_conv[l], lp)
        s_rw.append(b1.astype(state_rwkv.dtype)); s_sh.append(b2.astype(state_shift.dtype))
        s_gd.append(b3.astype(state_gdn.dtype)); s_cv.append(b4.astype(state_conv.dtype))
    return (yp, ys, jnp.stack(p_rw), jnp.stack(p_sh), jnp.stack(p_gd), jnp.stack(p_cv), jnp.stack(p_mk),
            jnp.stack(p_mv), jnp.stack(s_rw), jnp.stack(s_sh), jnp.stack(s_gd), jnp.stack(s_cv))
```

```python
import functools
import math

import jax
import jax.numpy as jnp
from jax import lax
from jax.experimental import pallas as pl
from jax.experimental.pallas import tpu as pltpu

F32 = jnp.float32
BF16 = jnp.bfloat16
HI = lax.Precision.HIGHEST

RW_HEAD = 64
GD_HEAD = 128
GD_CONV = 4
XA_HEAD = 128
LN_EPS = 1e-5
RW_GN_EPS = 64e-5
GD_NORM_EPS = 1e-6

VMEM_LIMIT_BYTES = 56 * 1024 * 1024

_NT = (((1,), (1,)), ((), ()))
_TN = (((0,), (0,)), ((), ()))


def _cparams(*sem):
    return pltpu.CompilerParams(dimension_semantics=sem, vmem_limit_bytes=VMEM_LIMIT_BYTES)


def _mm_kernel(x_ref, w_ref, o_ref):
    o_ref[...] = jnp.dot(x_ref[...], w_ref[...], preferred_element_type=F32)


def matmul(x, w, *, tm, tn):
    M, K = x.shape
    N = w.shape[1]
    assert M % tm == 0 and N % tn == 0
    return pl.pallas_call(
        _mm_kernel,
        grid=(M // tm, N // tn),
        in_specs=[pl.BlockSpec((tm, K), lambda i, j: (i, 0)),
                  pl.BlockSpec((K, tn), lambda i, j: (0, j))],
        out_specs=pl.BlockSpec((tm, tn), lambda i, j: (i, j)),
        out_shape=jax.ShapeDtypeStruct((M, N), F32),
        compiler_params=_cparams("parallel", "arbitrary"),
        name="matmul",
    )(x, w)


def _layer_norm_rows(y, g, b):
    mu = jnp.mean(y, -1, keepdims=True)
    yc = y - mu
    var = jnp.mean(yc * yc, -1, keepdims=True)
    return yc * lax.rsqrt(var + LN_EPS) * g + b


def _mm_ln_kernel(x_ref, w_ref, res_ref, g_ref, b_ref, o_ref, *, alpha):
    h = jnp.dot(x_ref[...], w_ref[...], preferred_element_type=F32)
    o_ref[...] = _layer_norm_rows(alpha * res_ref[...] + h, g_ref[...], b_ref[...])


def matmul_ln(x, w, res, g, b, *, alpha, tm):
    M, K = x.shape
    D = w.shape[1]
    assert M % tm == 0
    return pl.pallas_call(
        functools.partial(_mm_ln_kernel, alpha=alpha),
        grid=(M // tm,),
        in_specs=[pl.BlockSpec((tm, K), lambda i: (i, 0)),
                  pl.BlockSpec((K, D), lambda i: (0, 0)),
                  pl.BlockSpec((tm, D), lambda i: (i, 0)),
                  pl.BlockSpec((1, D), lambda i: (0, 0)),
                  pl.BlockSpec((1, D), lambda i: (0, 0))],
        out_specs=pl.BlockSpec((tm, D), lambda i: (i, 0)),
        out_shape=jax.ShapeDtypeStruct((M, D), F32),
        compiler_params=_cparams("parallel"),
        name="matmul_ln",
    )(x, w, res, g.reshape(1, D), b.reshape(1, D))


def _ffn_ln_kernel(x_ref, wg_ref, wu_ref, wd_ref, res_ref, g_ref, b_ref, o_ref, acc_ref, *, alpha):
    f = pl.program_id(1)

    @pl.when(f == 0)
    def _():
        acc_ref[...] = jnp.zeros_like(acc_ref)

    x = x_ref[...]
    gate = jnp.dot(x, wg_ref[...], preferred_element_type=F32)
    up = jnp.dot(x, wu_ref[...], preferred_element_type=F32)
    hidden = (gate * jax.nn.sigmoid(gate) * up).astype(BF16)
    acc_ref[...] += jnp.dot(hidden, wd_ref[...], preferred_element_type=F32)

    @pl.when(f == pl.num_programs(1) - 1)
    def _():
        o_ref[...] = _layer_norm_rows(alpha * res_ref[...] + acc_ref[...], g_ref[...], b_ref[...])


def ffn_ln(x, wg, wu, wd, res, g, b, *, alpha, tm, tf):
    M, D = x.shape
    Fd = wg.shape[1]
    assert M % tm == 0 and Fd % tf == 0
    return pl.pallas_call(
        functools.partial(_ffn_ln_kernel, alpha=alpha),
        grid=(M // tm, Fd // tf),
        in_specs=[pl.BlockSpec((tm, D), lambda i, f: (i, 0)),
                  pl.BlockSpec((D, tf), lambda i, f: (0, f)),
                  pl.BlockSpec((D, tf), lambda i, f: (0, f)),
                  pl.BlockSpec((tf, D), lambda i, f: (f, 0)),
                  pl.BlockSpec((tm, D), lambda i, f: (i, 0)),
                  pl.BlockSpec((1, D), lambda i, f: (0, 0)),
                  pl.BlockSpec((1, D), lambda i, f: (0, 0))],
        out_specs=pl.BlockSpec((tm, D), lambda i, f: (i, 0)),
        out_shape=jax.ShapeDtypeStruct((M, D), F32),
        scratch_shapes=[pltpu.VMEM((tm, D), F32)],
        compiler_params=_cparams("parallel", "arbitrary"),
        name="ffn_ln",
    )(x, wg, wu, wd, res, g.reshape(1, D), b.reshape(1, D))


def _dot(a, b, dims=None, precision=HI):
    if dims is None:
        return jnp.dot(a, b, preferred_element_type=F32, precision=precision)
    return lax.dot_general(a, b, dims, preferred_element_type=F32, precision=precision)


def _unit_lower_inverse(x, c):
    row = lax.broadcasted_iota(jnp.int32, (c, c), 0)
    col = lax.broadcasted_iota(jnp.int32, (c, c), 1)
    t = jnp.where(row == col, 1.0, 0.0).astype(F32) + x
    p = x
    span = 2
    while span < c:
        p = _dot(p, p)
        t = t + _dot(t, p)
        span *= 2
    return t


def _rwkv_kernel(r_ref, lw_ref, k_ref, v_ref, a_ref, b_ref, s0_ref, o_ref, st_ref, s_scr, *, C, H, N):
    c = pl.program_id(1)

    @pl.when(c == 0)
    def _():
        s_scr[...] = s0_ref[0]

    row = lax.broadcasted_iota(jnp.int32, (C, C), 0)
    col = lax.broadcasted_iota(jnp.int32, (C, C), 1)
    incl = row >= col
    strict = row > col
    lw = lw_ref[0]
    G = _dot(incl.astype(F32), lw)
    g_last = G[C - 1:C, :]
    e_in = jnp.exp(G)
    e_out = jnp.exp(-G)
    e_tail = jnp.exp(g_last - G)
    r_hat = r_ref[0] * e_in
    a_hat = a_ref[0] * jnp.exp(G - lw)
    b_hat = b_ref[0] * e_out
    k_hat = k_ref[0] * e_out
    b_tail = b_ref[0] * e_tail
    k_tail = k_ref[0] * e_tail
    v_all = v_ref[0]
    e_last = jnp.exp(g_last)

    for h in range(H):
        sl = slice(h * N, (h + 1) * N)
        S = s_scr[h]
        ah, bh, kh, rh, vh = a_hat[:, sl], b_hat[:, sl], k_hat[:, sl], r_hat[:, sl], v_all[:, sl]
        a_ab = jnp.where(strict, _dot(ah, bh, _NT), 0.0)
        a_ak = jnp.where(strict, _dot(ah, kh, _NT), 0.0)
        a_rb = jnp.where(incl, _dot(rh, bh, _NT), 0.0)
        a_rk = jnp.where(incl, _dot(rh, kh, _NT), 0.0)
        T = _unit_lower_inverse(a_ab, C)
        U = _dot(T, _dot(ah, S, _NT) + _dot(a_ak, vh))
        o_ref[0, :, sl] = _dot(rh, S, _NT) + _dot(a_rb, U) + _dot(a_rk, vh)
        s_scr[h] = S * e_last[:, sl] + _dot(U, b_tail[:, sl], _TN) + _dot(vh, k_tail[:, sl], _TN)

    @pl.when(c == pl.num_programs(1) - 1)
    def _():
        st_ref[0] = s_scr[...]


def rwkv_recurrence(r, lw, k, v, a, b, s0, *, chunk):
    B, T, W = r.shape
    H, N = s0.shape[1], s0.shape[2]
    assert T % chunk == 0
    seq = pl.BlockSpec((1, chunk, W), lambda i, c: (i, c, 0))
    st = pl.BlockSpec((1, H, N, N), lambda i, c: (i, 0, 0, 0))
    return pl.pallas_call(
        functools.partial(_rwkv_kernel, C=chunk, H=H, N=N),
        grid=(B, T // chunk),
        in_specs=[seq] * 6 + [st],
        out_specs=[seq, st],
        out_shape=[jax.ShapeDtypeStruct((B, T, W), F32), jax.ShapeDtypeStruct((B, H, N, N), F32)],
        scratch_shapes=[pltpu.VMEM((H, N, N), F32)],
        compiler_params=_cparams("parallel", "arbitrary"),
        name="rwkv_recurrence",
    )(r, lw, k, v, a, b, s0)


def _gdn_kernel(q_ref, k_ref, v_ref, beta_ref, gc_ref, gr_ref, s0_ref, o_ref, st_ref, s_scr, *, C, H, N):
    c = pl.program_id(1)

    @pl.when(c == 0)
    def _():
        s_scr[...] = s0_ref[0]

    row = lax.broadcasted_iota(jnp.int32, (C, C), 0)
    col = lax.broadcasted_iota(jnp.int32, (C, C), 1)
    incl = row >= col
    strict = row > col
    tri = incl.astype(F32)
    Gc = _dot(tri, gc_ref[0])
    Gr = _dot(gr_ref[0, 0], tri, _NT)
    beta = beta_ref[0]
    q_all, k_all, v_all = q_ref[0], k_ref[0], v_ref[0]

    for h in range(H):
        sl = slice(h * N, (h + 1) * N)
        S = s_scr[h]
        gc = Gc[:, h:h + 1]
        g_last = Gc[C - 1:C, h:h + 1]
        decay = jnp.where(incl, jnp.exp(jnp.where(incl, gc - Gr[h:h + 1, :], 0.0)), 0.0)
        bh = beta[:, h:h + 1]
        qh, kh, vh = q_all[:, sl], k_all[:, sl], v_all[:, sl]
        kb = kh * bh
        L = jnp.where(strict, _dot(kb, kh, _NT) * decay, 0.0)
        T = _unit_lower_inverse(-L, C)
        e_g = jnp.exp(gc)
        v_new = _dot(T, vh * bh - _dot(kb * e_g, S))
        attn = _dot(qh, kh, _NT) * decay
        o_ref[0, :, sl] = _dot(qh * e_g, S) + _dot(attn, v_new)
        s_scr[h] = S * jnp.exp(g_last) + _dot(kh * jnp.exp(g_last - gc), v_new, _TN)

    @pl.when(c == pl.num_programs(1) - 1)
    def _():
        st_ref[0] = s_scr[...]


def gdn_recurrence(q, k, v, beta, g, s0, *, chunk):
    B, T, W = q.shape
    H, N = s0.shape[1], s0.shape[2]
    assert T % chunk == 0
    n = T // chunk
    g_rows = jnp.swapaxes(g.reshape(B, n, chunk, H), 2, 3)
    seq = pl.BlockSpec((1, chunk, W), lambda i, c: (i, c, 0))
    gate = pl.BlockSpec((1, chunk, H), lambda i, c: (i, c, 0))
    st = pl.BlockSpec((1, H, N, N), lambda i, c: (i, 0, 0, 0))
    return pl.pallas_call(
        functools.partial(_gdn_kernel, C=chunk, H=H, N=N),
        grid=(B, n),
        in_specs=[seq, seq, seq, gate, gate, pl.BlockSpec((1, 1, H, chunk), lambda i, c: (i, c, 0, 0)), st],
        out_specs=[seq, st],
        out_shape=[jax.ShapeDtypeStruct((B, T, W), F32), jax.ShapeDtypeStruct((B, H, N, N), F32)],
        scratch_shapes=[pltpu.VMEM((H, N, N), F32)],
        compiler_params=_cparams("parallel", "arbitrary"),
        name="gdn_recurrence",
    )(q, k, v, beta, g, g_rows, s0)


def _xattn_kernel(q_ref, k_ref, v_ref, o_ref, *, H, Dh):
    q, k, v = q_ref[0], k_ref[0], v_ref[0]
    scale = Dh ** -0.5
    for h in range(H):
        sl = slice(h * Dh, (h + 1) * Dh)
        s = lax.dot_general(q[:, sl].astype(BF16), k[:, sl].astype(BF16), _NT,
                            preferred_element_type=F32) * scale
        e = jnp.exp(s - jnp.max(s, -1, keepdims=True))
        pr = e / jnp.sum(e, -1, keepdims=True)
        o_ref[0, :, sl] = jnp.dot(pr.astype(BF16), v[:, sl].astype(BF16), preferred_element_type=F32)


def cross_attention(q, mk, mv, *, heads, tq):
    B, T, W = q.shape
    Mm = mk.shape[1]
    assert T % tq == 0
    return pl.pallas_call(
        functools.partial(_xattn_kernel, H=heads, Dh=W // heads),
        grid=(B, T // tq),
        in_specs=[pl.BlockSpec((1, tq, W), lambda i, t: (i, t, 0)),
                  pl.BlockSpec((1, Mm, W), lambda i, t: (i, 0, 0)),
                  pl.BlockSpec((1, Mm, W), lambda i, t: (i, 0, 0))],
        out_specs=pl.BlockSpec((1, tq, W), lambda i, t: (i, t, 0)),
        out_shape=jax.ShapeDtypeStruct((B, T, W), F32),
        compiler_params=_cparams("parallel", "arbitrary"),
        name="cross_attention",
    )(q, mk, mv)


def _l2norm(t):
    return t * lax.rsqrt(jnp.sum(t * t, -1, keepdims=True) + 1e-12)


def _pad_time(t, t_pad):
    return jnp.pad(t, [(0, 0), (0, t_pad - t.shape[1])] + [(0, 0)] * (t.ndim - 2))


def rwkv7_mix(p, prev, S0, lp, *, chunk):
    B, T, _ = p.shape
    H, N = S0.shape[1], S0.shape[2]
    Wd = H * N
    shifted = jnp.concatenate([prev[:, None], p[:, :-1]], axis=1)
    m = p + (shifted - p) * lp['mu_shift']
    n_w, n_a = lp['rw_w2'].shape[0], lp['rw_a2'].shape[0]
    c2 = 3 * Wd
    c3 = c2 + n_w
    c4 = c3 + n_a
    r, k, v = m[..., :Wd], m[..., Wd:2 * Wd], m[..., 2 * Wd:c2]
    wd, ad, gd = m[..., c2:c3], m[..., c3:c4], m[..., c4:]
    w_log = -jax.nn.softplus(-(lp['rw_w0'] + jnp.tanh(wd) @ lp['rw_w2'])) - 0.5
    lw = -jnp.exp(w_log)
    a = jax.nn.sigmoid(lp['rw_a0'] + ad @ lp['rw_a2'])
    g = jax.nn.sigmoid(gd) @ lp['rw_g2']
    hs = lambda t: t.reshape(B, T, H, N)
    kk = _l2norm(hs(k * lp['rw_kk'])).reshape(B, T, Wd)
    k = k * (1.0 + (a - 1.0) * lp['rw_ka'])
    Tp = -(-T // chunk) * chunk
    pt = lambda t: _pad_time(t, Tp)
    o, S = rwkv_recurrence(pt(r), pt(lw), pt(k), pt(v), pt(-kk), pt(kk * a), S0, chunk=chunk)
    o = hs(o[:, :T])
    o_mu = jnp.mean(o, -1, keepdims=True)
    o_var = jnp.mean(jnp.square(o - o_mu), -1, keepdims=True)
    o = ((o - o_mu) * lax.rsqrt(o_var + RW_GN_EPS)).reshape(B, T, Wd) * lp['rw_lnx_w'] + lp['rw_lnx_b']
    bonus = jnp.sum(hs(r) * hs(k) * lp['rw_rk'], -1, keepdims=True) * hs(v)
    o = (o + bonus.reshape(B, T, Wd)) * g
    return o, S, p[:, -1]


def gdn_mix(p, conv_prev, S0, lp, *, chunk):
    B, T, _ = p.shape
    H, N = S0.shape[1], S0.shape[2]
    Wd = H * N
    W3 = 3 * Wd
    qkv = p[..., :W3]
    z = p[..., W3:W3 + Wd]
    b_raw = p[..., W3 + Wd:W3 + Wd + H]
    a_raw = p[..., W3 + Wd + H:W3 + Wd + 2 * H]
    xp = jnp.concatenate([conv_prev, qkv], axis=1)
    conv_w = lp['gd_conv_w']
    conv = xp[:, 0:T] * conv_w[0]
    for j in range(1, GD_CONV):
        conv = conv + xp[:, j:j + T] * conv_w[j]
    qkv = jax.nn.silu(conv)
    hs = lambda t: t.reshape(B, T, H, N)
    q = (_l2norm(hs(qkv[..., :Wd])) * (N ** -0.5)).reshape(B, T, Wd)
    k = _l2norm(hs(qkv[..., Wd:2 * Wd])).reshape(B, T, Wd)
    v = qkv[..., 2 * Wd:]
    beta = jax.nn.sigmoid(b_raw)
    g = -jnp.exp(lp['gd_a_log']) * jax.nn.softplus(a_raw + lp['gd_dt_bias'])
    Tp = -(-T // chunk) * chunk
    pt = lambda t: _pad_time(t, Tp)
    o, S = gdn_recurrence(pt(q), pt(k), pt(v), pt(beta), pt(g), S0, chunk=chunk)
    o = hs(o[:, :T])
    o = o * lax.rsqrt(jnp.mean(o * o, -1, keepdims=True) + GD_NORM_EPS) * lp['gd_norm_w'] * jax.nn.silu(hs(z))
    return o.reshape(B, T, Wd), S, xp[:, -(GD_CONV - 1):]


def _round_up(n, m):
    return -(-n // m) * m


def kernel(x_prompt, mem_prompt, x_sample, state_rwkv, state_shift, state_gdn, state_conv, cache_mem_k, cache_mem_v, w_in, mu_shift, rw_w0, rw_w2, rw_a0, rw_a2, rw_g2, rw_kk, rw_ka, rw_rk, rw_lnx_w, rw_lnx_b, gd_conv_w, gd_a_log, gd_dt_bias, gd_norm_w, w_out, ln1_g, ln1_b, xa_wq, xa_wk, xa_wv, xa_wo, ln2_g, ln2_b, ff_wg, ff_wu, ff_wd, ln3_g, ln3_b):
    depth = w_in.shape[0]
    Bp, Tp, D = x_prompt.shape
    Bs, Ts, _ = x_sample.shape
    Mp, Ms = Bp * Tp, Bs * Ts
    rw_heads = state_rwkv.shape[2]
    gd_heads = state_gdn.shape[2]
    rw_cols = state_shift.shape[-1]
    in_cols = w_in.shape[-1]
    n_mem, xa_heads = cache_mem_k.shape[2], cache_mem_k.shape[3]
    xa_width = xa_heads * cache_mem_k.shape[4]
    alpha = (2 * depth) ** 0.25
    tm = 512
    in_cols_pad = _round_up(in_cols, 512)

    x = jnp.concatenate([x_prompt.reshape(Mp, D), x_sample.reshape(Ms, D)], axis=0)
    mem = mem_prompt.reshape(Bp * n_mem, D).astype(BF16)
    zeros = lambda *s: jnp.zeros(s, F32)

    outs = {n: [] for n in ('p_rw', 'p_sh', 'p_gd', 'p_cv', 'p_mk', 'p_mv', 's_rw', 's_sh', 's_gd', 's_cv')}
    for l in range(depth):
        lp = {'mu_shift': mu_shift[l], 'rw_w0': rw_w0[l], 'rw_w2': rw_w2[l], 'rw_a0': rw_a0[l], 'rw_a2': rw_a2[l],
              'rw_g2': rw_g2[l], 'rw_kk': rw_kk[l], 'rw_ka': rw_ka[l], 'rw_rk': rw_rk[l], 'rw_lnx_w': rw_lnx_w[l],
              'rw_lnx_b': rw_lnx_b[l], 'gd_conv_w': gd_conv_w[l], 'gd_a_log': gd_a_log[l],
              'gd_dt_bias': gd_dt_bias[l], 'gd_norm_w': gd_norm_w[l]}
        w_in_l = jnp.pad(w_in[l].astype(BF16), ((0, 0), (0, in_cols_pad - in_cols)))
        w_kv = jnp.concatenate([xa_wk[l], xa_wv[l]], axis=1).astype(BF16)

        kv = matmul(mem, w_kv, tm=512, tn=512)
        mk = kv[:, :xa_width].reshape(Bp, n_mem, xa_width)
        mv = kv[:, xa_width:].reshape(Bp, n_mem, xa_width)

        p = matmul(x.astype(BF16), w_in_l, tm=tm, tn=512)
        p_p = p[:Mp].reshape(Bp, Tp, in_cols_pad)
        p_s = p[Mp:].reshape(Bs, Ts, in_cols_pad)

        o_rw_p, rw_p, sh_p = rwkv7_mix(p_p[..., :rw_cols], zeros(Bp, rw_cols),
                                       zeros(Bp, rw_heads, RW_HEAD, RW_HEAD), lp, chunk=64)
        o_gd_p, gd_p, cv_p = gdn_mix(p_p[..., rw_cols:in_cols], zeros(Bp, GD_CONV - 1, 3 * gd_heads * GD_HEAD),
                                     zeros(Bp, gd_heads, GD_HEAD, GD_HEAD), lp, chunk=64)
        o_rw_s, rw_s, sh_s = rwkv7_mix(p_s[..., :rw_cols], state_shift[l], state_rwkv[l], lp, chunk=8)
        o_gd_s, gd_s, cv_s = gdn_mix(p_s[..., rw_cols:in_cols], state_conv[l], state_gdn[l], lp, chunk=8)

        mix = jnp.concatenate([jnp.concatenate([o_rw_p, o_gd_p], -1).reshape(Mp, D),
                               jnp.concatenate([o_rw_s, o_gd_s], -1).reshape(Ms, D)], axis=0)
        x = matmul_ln(mix.astype(BF16), w_out[l].astype(BF16), x, ln1_g[l], ln1_b[l], alpha=alpha, tm=tm)

        q = matmul(x.astype(BF16), xa_wq[l].astype(BF16), tm=tm, tn=xa_width)
        att_p = cross_attention(q[:Mp].reshape(Bp, Tp, xa_width), mk, mv, heads=xa_heads, tq=512)
        att_s = cross_attention(q[Mp:].reshape(Bs, Ts, xa_width), cache_mem_k[l].reshape(Bs, n_mem, xa_width),
                                cache_mem_v[l].reshape(Bs, n_mem, xa_width), heads=xa_heads, tq=Ts)
        att = jnp.concatenate([att_p.reshape(Mp, xa_width), att_s.reshape(Ms, xa_width)], axis=0)
        x = matmul_ln(att.astype(BF16), xa_wo[l].astype(BF16), x, ln2_g[l], ln2_b[l], alpha=alpha, tm=tm)

        x = ffn_ln(x.astype(BF16), ff_wg[l].astype(BF16), ff_wu[l].astype(BF16), ff_wd[l].astype(BF16), x,
                   ln3_g[l], ln3_b[l], alpha=alpha, tm=tm, tf=512)

        outs['p_rw'].append(rw_p); outs['p_sh'].append(sh_p); outs['p_gd'].append(gd_p); outs['p_cv'].append(cv_p)
        outs['p_mk'].append(mk.reshape(Bp, n_mem, xa_heads, -1)); outs['p_mv'].append(mv.reshape(Bp, n_mem, xa_heads, -1))
        outs['s_rw'].append(rw_s); outs['s_sh'].append(sh_s); outs['s_gd'].append(gd_s); outs['s_cv'].append(cv_s)

    st = {n: jnp.stack(v) for n, v in outs.items()}
    return (x[:Mp].reshape(Bp, Tp, D), x[Mp:].reshape(Bs, Ts, D), st['p_rw'], st['p_sh'], st['p_gd'], st['p_cv'],
            st['p_mk'], st['p_mv'], st['s_rw'], st['s_sh'], st['s_gd'], st['s_cv'])
```

```python
import functools
import math

import jax
import jax.numpy as jnp
from jax import lax
from jax.experimental import pallas as pl
from jax.experimental.pallas import tpu as pltpu

F32 = jnp.float32
BF16 = jnp.bfloat16
HI = lax.Precision.HIGHEST

RW_HEAD = 64
GD_HEAD = 128
GD_CONV = 4
XA_HEAD = 128
LN_EPS = 1e-5
RW_GN_EPS = 64e-5
GD_NORM_EPS = 1e-6

VMEM_LIMIT_BYTES = 56 * 1024 * 1024

_NN = (((1,), (0,)), ((), ()))
_NT = (((1,), (1,)), ((), ()))
_TN = (((0,), (0,)), ((), ()))


def _cparams(*sem):
    return pltpu.CompilerParams(dimension_semantics=sem, vmem_limit_bytes=VMEM_LIMIT_BYTES)


def _mm_kernel(x_ref, w_ref, o_ref):
    o_ref[...] = jnp.dot(x_ref[...], w_ref[...], preferred_element_type=F32)


def matmul(x, w, *, tm, tn):
    M, K = x.shape
    N = w.shape[1]
    assert M % tm == 0 and N % tn == 0
    return pl.pallas_call(
        _mm_kernel,
        grid=(M // tm, N // tn),
        in_specs=[pl.BlockSpec((tm, K), lambda i, j: (i, 0)),
                  pl.BlockSpec((K, tn), lambda i, j: (0, j))],
        out_specs=pl.BlockSpec((tm, tn), lambda i, j: (i, j)),
        out_shape=jax.ShapeDtypeStruct((M, N), F32),
        compiler_params=_cparams("parallel", "arbitrary"),
        name="matmul",
    )(x, w)


def _layer_norm_rows(y, g, b):
    mu = jnp.mean(y, -1, keepdims=True)
    yc = y - mu
    var = jnp.mean(yc * yc, -1, keepdims=True)
    return yc * lax.rsqrt(var + LN_EPS) * g + b


def _mm_ln_kernel(x_ref, w_ref, res_ref, g_ref, b_ref, o_ref, *, alpha):
    h = jnp.dot(x_ref[...], w_ref[...], preferred_element_type=F32)
    o_ref[...] = _layer_norm_rows(alpha * res_ref[...] + h, g_ref[...], b_ref[...])


def matmul_ln(x, w, res, g, b, *, alpha, tm):
    M, K = x.shape
    D = w.shape[1]
    assert M % tm == 0
    return pl.pallas_call(
        functools.partial(_mm_ln_kernel, alpha=alpha),
        grid=(M // tm,),
        in_specs=[pl.BlockSpec((tm, K), lambda i: (i, 0)),
                  pl.BlockSpec((K, D), lambda i: (0, 0)),
                  pl.BlockSpec((tm, D), lambda i: (i, 0)),
                  pl.BlockSpec((1, D), lambda i: (0, 0)),
                  pl.BlockSpec((1, D), lambda i: (0, 0))],
        out_specs=pl.BlockSpec((tm, D), lambda i: (i, 0)),
        out_shape=jax.ShapeDtypeStruct((M, D), F32),
        compiler_params=_cparams("parallel"),
        name="matmul_ln",
    )(x, w, res, g.reshape(1, D), b.reshape(1, D))


def _ffn_ln_kernel(x_ref, wg_ref, wu_ref, wd_ref, res_ref, g_ref, b_ref, o_ref, acc_ref, *, alpha):
    f = pl.program_id(1)

    @pl.when(f == 0)
    def _():
        acc_ref[...] = jnp.zeros_like(acc_ref)

    x = x_ref[...]
    gate = jnp.dot(x, wg_ref[...], preferred_element_type=F32)
    up = jnp.dot(x, wu_ref[...], preferred_element_type=F32)
    hidden = (gate * jax.nn.sigmoid(gate) * up).astype(BF16)
    acc_ref[...] += jnp.dot(hidden, wd_ref[...], preferred_element_type=F32)

    @pl.when(f == pl.num_programs(1) - 1)
    def _():
        o_ref[...] = _layer_norm_rows(alpha * res_ref[...] + acc_ref[...], g_ref[...], b_ref[...])


def ffn_ln(x, wg, wu, wd, res, g, b, *, alpha, tm, tf):
    M, D = x.shape
    Fd = wg.shape[1]
    assert M % tm == 0 and Fd % tf == 0
    return pl.pallas_call(
        functools.partial(_ffn_ln_kernel, alpha=alpha),
        grid=(M // tm, Fd // tf),
        in_specs=[pl.BlockSpec((tm, D), lambda i, f: (i, 0)),
                  pl.BlockSpec((D, tf), lambda i, f: (0, f)),
                  pl.BlockSpec((D, tf), lambda i, f: (0, f)),
                  pl.BlockSpec((tf, D), lambda i, f: (f, 0)),
                  pl.BlockSpec((tm, D), lambda i, f: (i, 0)),
                  pl.BlockSpec((1, D), lambda i, f: (0, 0)),
                  pl.BlockSpec((1, D), lambda i, f: (0, 0))],
        out_specs=pl.BlockSpec((tm, D), lambda i, f: (i, 0)),
        out_shape=jax.ShapeDtypeStruct((M, D), F32),
        scratch_shapes=[pltpu.VMEM((tm, D), F32)],
        compiler_params=_cparams("parallel", "arbitrary"),
        name="ffn_ln",
    )(x, wg, wu, wd, res, g.reshape(1, D), b.reshape(1, D))


def _split_bf16(a):
    hi = a.astype(BF16)
    return hi, (a - hi.astype(F32)).astype(BF16)


def _mm(a, b, dims, mode):
    dot = functools.partial(lax.dot_general, dimension_numbers=dims, preferred_element_type=F32)
    if mode == 'f32':
        return dot(a, b, precision=HI)
    if mode == 'bf16':
        return dot(a.astype(BF16), b.astype(BF16))
    assert mode == 'x3'
    a_hi, a_lo = _split_bf16(a)
    b_hi, b_lo = _split_bf16(b)
    return dot(a_hi, b_hi) + (dot(a_hi, b_lo) + dot(a_lo, b_hi))


RW_MODES = dict(scores='bf16', inverse='bf16', read='bf16', apply='bf16', update='bf16')
GD_MODES = dict(scores='bf16', inverse='bf16', read='bf16', apply='bf16', update='bf16')


def _unit_lower_inverse(xs, c, mode):
    row = lax.broadcasted_iota(jnp.int32, (c, c), 0)
    col = lax.broadcasted_iota(jnp.int32, (c, c), 1)
    eye = jnp.where(row == col, 1.0, 0.0).astype(F32)
    ts = [eye + x for x in xs]
    ps = list(xs)
    span = 2
    while span < c:
        ps = [_mm(p, p, _NN, mode) for p in ps]
        ts = [t + _mm(t, p, _NN, mode) for t, p in zip(ts, ps)]
        span *= 2
    return ts


def _rwkv_kernel(r_ref, lw_ref, k_ref, v_ref, a_ref, b_ref, s0_ref, o_ref, st_ref, s_scr, *, C, H, N, BB, modes):
    c = pl.program_id(1)

    @pl.when(c == 0)
    def _():
        s_scr[...] = s0_ref[...]

    row = lax.broadcasted_iota(jnp.int32, (2 * C, 2 * C), 0)
    col = lax.broadcasted_iota(jnp.int32, (2 * C, 2 * C), 1)
    rowc = jnp.where(row >= C, row - C, row)
    colc = jnp.where(col >= C, col - C, col)
    keep = jnp.where(row >= C, rowc, rowc - 1) >= colc
    tri = (lax.broadcasted_iota(jnp.int32, (C, C), 0) >= lax.broadcasted_iota(jnp.int32, (C, C), 1)).astype(F32)
    mid = C // 2 - 1

    def one_batch(i, carry):
        lw = lw_ref[i]
        G = _mm(tri, lw, _NN, 'f32')
        Gm = G - G[mid:mid + 1, :]
        g_last = G[C - 1:C, :]
        e_out = jnp.exp(-Gm)
        e_tail = jnp.exp(g_last - G)
        k_in, b_in = k_ref[i], b_ref[i]
        AR = jnp.concatenate([a_ref[i] * jnp.exp(Gm - lw), r_ref[i] * jnp.exp(Gm)], axis=0)
        BK = jnp.concatenate([b_in * e_out, k_in * e_out], axis=0)
        BKt = jnp.concatenate([b_in * e_tail, k_in * e_tail], axis=0)
        e_mid = jnp.exp(G[mid:mid + 1, :])
        e_last = jnp.exp(g_last)
        v_all = v_ref[i]
        hd = range(H)
        sl = [slice(h * N, (h + 1) * N) for h in hd]
        S = [s_scr[i, h] for h in hd]
        M = [jnp.where(keep, _mm(AR[:, sl[h]], BK[:, sl[h]], _NT, modes['scores']), 0.0)
             for h in hd]
        SR = [_mm(AR[:, sl[h]], S[h] * e_mid[:, sl[h]], _NT, modes['read']) for h in hd]
        AV = [_mm(M[h][:C, C:], v_all[:, sl[h]], _NN, modes['apply']) for h in hd]
        T = _unit_lower_inverse([M[h][:C, :C] for h in hd], C, modes['inverse'])
        U = [_mm(T[h], SR[h][:C] + AV[h], _NN, modes['apply']) for h in hd]
        UV = [jnp.concatenate([U[h], v_all[:, sl[h]]], axis=0) for h in hd]
        O = [SR[h][C:] + _mm(M[h][C:], UV[h], _NN, modes['apply']) for h in hd]
        S = [S[h] * e_last[:, sl[h]] + _mm(UV[h], BKt[:, sl[h]], _TN, modes['update']) for h in hd]
        for h in hd:
            o_ref[i, :, sl[h]] = O[h]
            s_scr[i, h] = S[h]
        return carry

    if BB == 1:
        one_batch(0, 0)
    else:
        lax.fori_loop(0, BB, one_batch, 0)

    @pl.when(c == pl.num_programs(1) - 1)
    def _():
        st_ref[...] = s_scr[...]


def rwkv_recurrence(r, lw, k, v, a, b, s0, *, chunk, bb=1, modes=None):
    B, T, W = r.shape
    H, N = s0.shape[1], s0.shape[2]
    assert T % chunk == 0 and B % bb == 0
    seq = pl.BlockSpec((bb, chunk, W), lambda i, c: (i, c, 0))
    st = pl.BlockSpec((bb, H, N, N), lambda i, c: (i, 0, 0, 0))
    return pl.pallas_call(
        functools.partial(_rwkv_kernel, C=chunk, H=H, N=N, BB=bb, modes=modes or RW_MODES),
        grid=(B // bb, T // chunk),
        in_specs=[seq] * 6 + [st],
        out_specs=[seq, st],
        out_shape=[jax.ShapeDtypeStruct((B, T, W), F32), jax.ShapeDtypeStruct((B, H, N, N), F32)],
        scratch_shapes=[pltpu.VMEM((bb, H, N, N), F32)],
        compiler_params=_cparams("parallel", "arbitrary"),
        name="rwkv_recurrence",
    )(r, lw, k, v, a, b, s0)


def _gdn_kernel(q_ref, k_ref, v_ref, beta_ref, gc_ref, gr_ref, s0_ref, o_ref, st_ref, s_scr, *, C, H, N, BB, modes):
    c = pl.program_id(1)

    @pl.when(c == 0)
    def _():
        s_scr[...] = s0_ref[...]

    row = lax.broadcasted_iota(jnp.int32, (2 * C, C), 0)
    col = lax.broadcasted_iota(jnp.int32, (2 * C, C), 1)
    keep = jnp.where(row >= C, row - C, row - 1) >= col
    tri = (lax.broadcasted_iota(jnp.int32, (C, C), 0) >= lax.broadcasted_iota(jnp.int32, (C, C), 1)).astype(F32)

    def one_batch(i, carry):
        Gc = _mm(tri, gc_ref[i], _NN, 'f32')
        Gr = _mm(gr_ref[i, 0], tri, _NT, 'f32')
        beta = beta_ref[i]
        q_all, k_all, v_all = q_ref[i], k_ref[i], v_ref[i]
        hd = range(H)
        sl = [slice(h * N, (h + 1) * N) for h in hd]
        S = [s_scr[i, h] for h in hd]
        gc = [Gc[:, h:h + 1] for h in hd]
        g_last = [Gc[C - 1:C, h:h + 1] for h in hd]
        gc2 = [jnp.concatenate([gc[h], gc[h]], axis=0) for h in hd]
        decay = [jnp.where(keep, jnp.exp(jnp.where(keep, gc2[h] - Gr[h:h + 1, :], 0.0)), 0.0) for h in hd]
        bh = [beta[:, h:h + 1] for h in hd]
        KQ = [jnp.concatenate([k_all[:, sl[h]] * bh[h], q_all[:, sl[h]]], axis=0) for h in hd]
        M = [_mm(KQ[h], k_all[:, sl[h]], _NT, modes['scores']) * decay[h] for h in hd]
        SR = [_mm(KQ[h] * jnp.exp(gc2[h]), S[h], _NN, modes['read']) for h in hd]
        T = _unit_lower_inverse([-M[h][:C] for h in hd], C, modes['inverse'])
        v_new = [_mm(T[h], v_all[:, sl[h]] * bh[h] - SR[h][:C], _NN, modes['apply']) for h in hd]
        O = [SR[h][C:] + _mm(M[h][C:], v_new[h], _NN, modes['apply']) for h in hd]
        S = [S[h] * jnp.exp(g_last[h])
             + _mm(k_all[:, sl[h]] * jnp.exp(g_last[h] - gc[h]), v_new[h], _TN, modes['update']) for h in hd]
        for h in hd:
            o_ref[i, :, sl[h]] = O[h]
            s_scr[i, h] = S[h]
        return carry

    if BB == 1:
        one_batch(0, 0)
    else:
        lax.fori_loop(0, BB, one_batch, 0)

    @pl.when(c == pl.num_programs(1) - 1)
    def _():
        st_ref[...] = s_scr[...]


def gdn_recurrence(q, k, v, beta, g, s0, *, chunk, bb=1, modes=None):
    B, T, W = q.shape
    H, N = s0.shape[1], s0.shape[2]
    assert T % chunk == 0 and B % bb == 0
    n = T // chunk
    g_rows = jnp.swapaxes(g.reshape(B, n, chunk, H), 2, 3)
    seq = pl.BlockSpec((bb, chunk, W), lambda i, c: (i, c, 0))
    gate = pl.BlockSpec((bb, chunk, H), lambda i, c: (i, c, 0))
    st = pl.BlockSpec((bb, H, N, N), lambda i, c: (i, 0, 0, 0))
    return pl.pallas_call(
        functools.partial(_gdn_kernel, C=chunk, H=H, N=N, BB=bb, modes=modes or GD_MODES),
        grid=(B // bb, n),
        in_specs=[seq, seq, seq, gate, gate, pl.BlockSpec((bb, 1, H, chunk), lambda i, c: (i, c, 0, 0)), st],
        out_specs=[seq, st],
        out_shape=[jax.ShapeDtypeStruct((B, T, W), F32), jax.ShapeDtypeStruct((B, H, N, N), F32)],
        scratch_shapes=[pltpu.VMEM((bb, H, N, N), F32)],
        compiler_params=_cparams("parallel", "arbitrary"),
        name="gdn_recurrence",
    )(q, k, v, beta, g, g_rows, s0)


def _xattn_kernel(q_ref, k_ref, v_ref, o_ref, *, H, Dh):
    q, k, v = q_ref[0], k_ref[0], v_ref[0]
    scale = Dh ** -0.5
    for h in range(H):
        sl = slice(h * Dh, (h + 1) * Dh)
        s = lax.dot_general(q[:, sl].astype(BF16), k[:, sl].astype(BF16), _NT,
                            preferred_element_type=F32) * scale
        e = jnp.exp(s - jnp.max(s, -1, keepdims=True))
        pr = e / jnp.sum(e, -1, keepdims=True)
        o_ref[0, :, sl] = jnp.dot(pr.astype(BF16), v[:, sl].astype(BF16), preferred_element_type=F32)


def cross_attention(q, mk, mv, *, heads, tq):
    B, T, W = q.shape
    Mm = mk.shape[1]
    assert T % tq == 0
    return pl.pallas_call(
        functools.partial(_xattn_kernel, H=heads, Dh=W // heads),
        grid=(B, T // tq),
        in_specs=[pl.BlockSpec((1, tq, W), lambda i, t: (i, t, 0)),
                  pl.BlockSpec((1, Mm, W), lambda i, t: (i, 0, 0)),
                  pl.BlockSpec((1, Mm, W), lambda i, t: (i, 0, 0))],
        out_specs=pl.BlockSpec((1, tq, W), lambda i, t: (i, t, 0)),
        out_shape=jax.ShapeDtypeStruct((B, T, W), F32),
        compiler_params=_cparams("parallel", "arbitrary"),
        name="cross_attention",
    )(q, mk, mv)


def _l2norm(t):
    return t * lax.rsqrt(jnp.sum(t * t, -1, keepdims=True) + 1e-12)


def _pad_time(t, t_pad):
    return jnp.pad(t, [(0, 0), (0, t_pad - t.shape[1])] + [(0, 0)] * (t.ndim - 2))


def rwkv7_mix(p, prev, S0, lp, *, chunk, bb=1, modes=None):
    B, T, _ = p.shape
    H, N = S0.shape[1], S0.shape[2]
    Wd = H * N
    shifted = jnp.concatenate([prev[:, None], p[:, :-1]], axis=1)
    m = p + (shifted - p) * lp['mu_shift']
    n_w, n_a = lp['rw_w2'].shape[0], lp['rw_a2'].shape[0]
    c2 = 3 * Wd
    c3 = c2 + n_w
    c4 = c3 + n_a
    r, k, v = m[..., :Wd], m[..., Wd:2 * Wd], m[..., 2 * Wd:c2]
    wd, ad, gd = m[..., c2:c3], m[..., c3:c4], m[..., c4:]
    w_log = -jax.nn.softplus(-(lp['rw_w0'] + jnp.tanh(wd) @ lp['rw_w2'])) - 0.5
    lw = -jnp.exp(w_log)
    a = jax.nn.sigmoid(lp['rw_a0'] + ad @ lp['rw_a2'])
    g = jax.nn.sigmoid(gd) @ lp['rw_g2']
    hs = lambda t: t.reshape(B, T, H, N)
    kk = _l2norm(hs(k * lp['rw_kk'])).reshape(B, T, Wd)
    k = k * (1.0 + (a - 1.0) * lp['rw_ka'])
    Tp = -(-T // chunk) * chunk
    pt = lambda t: _pad_time(t, Tp)
    o, S = rwkv_recurrence(pt(r), pt(lw), pt(k), pt(v), pt(-kk), pt(kk * a), S0, chunk=chunk, bb=bb, modes=modes)
    o = hs(o[:, :T])
    o_mu = jnp.mean(o, -1, keepdims=True)
    o_var = jnp.mean(jnp.square(o - o_mu), -1, keepdims=True)
    o = ((o - o_mu) * lax.rsqrt(o_var + RW_GN_EPS)).reshape(B, T, Wd) * lp['rw_lnx_w'] + lp['rw_lnx_b']
    bonus = jnp.sum(hs(r) * hs(k) * lp['rw_rk'], -1, keepdims=True) * hs(v)
    o = (o + bonus.reshape(B, T, Wd)) * g
    return o, S, p[:, -1]


def gdn_mix(p, conv_prev, S0, lp, *, chunk, bb=1, modes=None):
    B, T, _ = p.shape
    H, N = S0.shape[1], S0.shape[2]
    Wd = H * N
    W3 = 3 * Wd
    qkv = p[..., :W3]
    z = p[..., W3:W3 + Wd]
    b_raw = p[..., W3 + Wd:W3 + Wd + H]
    a_raw = p[..., W3 + Wd + H:W3 + Wd + 2 * H]
    xp = jnp.concatenate([conv_prev, qkv], axis=1)
    conv_w = lp['gd_conv_w']
    conv = xp[:, 0:T] * conv_w[0]
    for j in range(1, GD_CONV):
        conv = conv + xp[:, j:j + T] * conv_w[j]
    qkv = jax.nn.silu(conv)
    hs = lambda t: t.reshape(B, T, H, N)
    q = (_l2norm(hs(qkv[..., :Wd])) * (N ** -0.5)).reshape(B, T, Wd)
    k = _l2norm(hs(qkv[..., Wd:2 * Wd])).reshape(B, T, Wd)
    v = qkv[..., 2 * Wd:]
    beta = jax.nn.sigmoid(b_raw)
    g = -jnp.exp(lp['gd_a_log']) * jax.nn.softplus(a_raw + lp['gd_dt_bias'])
    Tp = -(-T // chunk) * chunk
    pt = lambda t: _pad_time(t, Tp)
    o, S = gdn_recurrence(pt(q), pt(k), pt(v), pt(beta), pt(g), S0, chunk=chunk, bb=bb, modes=modes)
    o = hs(o[:, :T])
    o = o * lax.rsqrt(jnp.mean(o * o, -1, keepdims=True) + GD_NORM_EPS) * lp['gd_norm_w'] * jax.nn.silu(hs(z))
    return o.reshape(B, T, Wd), S, xp[:, -(GD_CONV - 1):]


def _round_up(n, m):
    return -(-n // m) * m


def kernel(x_prompt, mem_prompt, x_sample, state_rwkv, state_shift, state_gdn, state_conv, cache_mem_k, cache_mem_v, w_in, mu_shift, rw_w0, rw_w2, rw_a0, rw_a2, rw_g2, rw_kk, rw_ka, rw_rk, rw_lnx_w, rw_lnx_b, gd_conv_w, gd_a_log, gd_dt_bias, gd_norm_w, w_out, ln1_g, ln1_b, xa_wq, xa_wk, xa_wv, xa_wo, ln2_g, ln2_b, ff_wg, ff_wu, ff_wd, ln3_g, ln3_b):
    depth = w_in.shape[0]
    Bp, Tp, D = x_prompt.shape
    Bs, Ts, _ = x_sample.shape
    Mp, Ms = Bp * Tp, Bs * Ts
    rw_heads = state_rwkv.shape[2]
    gd_heads = state_gdn.shape[2]
    rw_cols = state_shift.shape[-1]
    in_cols = w_in.shape[-1]
    n_mem, xa_heads = cache_mem_k.shape[2], cache_mem_k.shape[3]
    xa_width = xa_heads * cache_mem_k.shape[4]
    alpha = (2 * depth) ** 0.25
    tm = 512
    in_cols_pad = _round_up(in_cols, 512)

    x = jnp.concatenate([x_prompt.reshape(Mp, D), x_sample.reshape(Ms, D)], axis=0)
    mem = mem_prompt.reshape(Bp * n_mem, D).astype(BF16)
    zeros = lambda *s: jnp.zeros(s, F32)

    outs = {n: [] for n in ('p_rw', 'p_sh', 'p_gd', 'p_cv', 'p_mk', 'p_mv', 's_rw', 's_sh', 's_gd', 's_cv')}
    for l in range(depth):
        lp = {'mu_shift': mu_shift[l], 'rw_w0': rw_w0[l], 'rw_w2': rw_w2[l], 'rw_a0': rw_a0[l], 'rw_a2': rw_a2[l],
              'rw_g2': rw_g2[l], 'rw_kk': rw_kk[l], 'rw_ka': rw_ka[l], 'rw_rk': rw_rk[l], 'rw_lnx_w': rw_lnx_w[l],
              'rw_lnx_b': rw_lnx_b[l], 'gd_conv_w': gd_conv_w[l], 'gd_a_log': gd_a_log[l],
              'gd_dt_bias': gd_dt_bias[l], 'gd_norm_w': gd_norm_w[l]}
        w_in_l = jnp.pad(w_in[l].astype(BF16), ((0, 0), (0, in_cols_pad - in_cols)))
        w_kv = jnp.concatenate([xa_wk[l], xa_wv[l]], axis=1).astype(BF16)

        kv = matmul(mem, w_kv, tm=512, tn=512)
        mk = kv[:, :xa_width].reshape(Bp, n_mem, xa_width)
        mv = kv[:, xa_width:].reshape(Bp, n_mem, xa_width)

        p = matmul(x.astype(BF16), w_in_l, tm=tm, tn=512)
        p_p = p[:Mp].reshape(Bp, Tp, in_cols_pad)
        p_s = p[Mp:].reshape(Bs, Ts, in_cols_pad)

        o_rw_p, rw_p, sh_p = rwkv7_mix(p_p[..., :rw_cols], zeros(Bp, rw_cols),
                                       zeros(Bp, rw_heads, RW_HEAD, RW_HEAD), lp, chunk=64)
        o_gd_p, gd_p, cv_p = gdn_mix(p_p[..., rw_cols:in_cols], zeros(Bp, GD_CONV - 1, 3 * gd_heads * GD_HEAD),
                                     zeros(Bp, gd_heads, GD_HEAD, GD_HEAD), lp, chunk=64)
        o_rw_s, rw_s, sh_s = rwkv7_mix(p_s[..., :rw_cols], state_shift[l], state_rwkv[l], lp, chunk=8, bb=8)
        o_gd_s, gd_s, cv_s = gdn_mix(p_s[..., rw_cols:in_cols], state_conv[l], state_gdn[l], lp, chunk=8, bb=8)

        mix = jnp.concatenate([jnp.concatenate([o_rw_p, o_gd_p], -1).reshape(Mp, D),
                               jnp.concatenate([o_rw_s, o_gd_s], -1).reshape(Ms, D)], axis=0)
        x = matmul_ln(mix.astype(BF16), w_out[l].astype(BF16), x, ln1_g[l], ln1_b[l], alpha=alpha, tm=tm)

        q = matmul(x.astype(BF16), xa_wq[l].astype(BF16), tm=tm, tn=xa_width)
        att_p = cross_attention(q[:Mp].reshape(Bp, Tp, xa_width), mk, mv, heads=xa_heads, tq=512)
        att_s = cross_attention(q[Mp:].reshape(Bs, Ts, xa_width), cache_mem_k[l].reshape(Bs, n_mem, xa_width),
                                cache_mem_v[l].reshape(Bs, n_mem, xa_width), heads=xa_heads, tq=Ts)
        att = jnp.concatenate([att_p.reshape(Mp, xa_width), att_s.reshape(Ms, xa_width)], axis=0)
        x = matmul_ln(att.astype(BF16), xa_wo[l].astype(BF16), x, ln2_g[l], ln2_b[l], alpha=alpha, tm=tm)

        x = ffn_ln(x.astype(BF16), ff_wg[l].astype(BF16), ff_wu[l].astype(BF16), ff_wd[l].astype(BF16), x,
                   ln3_g[l], ln3_b[l], alpha=alpha, tm=tm, tf=512)

        outs['p_rw'].append(rw_p); outs['p_sh'].append(sh_p); outs['p_gd'].append(gd_p); outs['p_cv'].append(cv_p)
        outs['p_mk'].append(mk.reshape(Bp, n_mem, xa_heads, -1)); outs['p_mv'].append(mv.reshape(Bp, n_mem, xa_heads, -1))
        outs['s_rw'].append(rw_s); outs['s_sh'].append(sh_s); outs['s_gd'].append(gd_s); outs['s_cv'].append(cv_s)

    st = {n: jnp.stack(v) for n, v in outs.items()}
    return (x[:Mp].reshape(Bp, Tp, D), x[Mp:].reshape(Bs, Ts, D), st['p_rw'], st['p_sh'], st['p_gd'], st['p_cv'],
            st['p_mk'], st['p_mv'], st['s_rw'], st['s_sh'], st['s_gd'], st['s_cv'])
```

```python
import functools

import jax
import jax.numpy as jnp
from jax import lax
from jax.experimental import pallas as pl
from jax.experimental.pallas import tpu as pltpu

F32 = jnp.float32
BF16 = jnp.bfloat16
HI = lax.Precision.HIGHEST

LANES = 128
SUBLANES = 8
RW_HEAD = 64
GD_HEAD = 128
GD_CONV = 4
LN_EPS = 1e-5
RW_GN_EPS = 64e-5
GD_NORM_EPS = 1e-6
L2_EPS = 1e-12

PROMPT_CHUNK = 64
SAMPLE_STEPS = SUBLANES
SAMPLE_SEQS = 8

VMEM_LIMIT_BYTES = 56 * 1024 * 1024

_NN = (((1,), (0,)), ((), ()))
_NT = (((1,), (1,)), ((), ()))
_TN = (((0,), (0,)), ((), ()))
_TT = (((0,), (1,)), ((), ()))


def _cparams(*sem):
    return pltpu.CompilerParams(dimension_semantics=sem, vmem_limit_bytes=VMEM_LIMIT_BYTES)


def _round_up(n, m):
    return -(-n // m) * m


def _pick_tile(n, candidates):
    for c in candidates:
        if n % c == 0:
            return c
    raise ValueError(f"no tile in {candidates} divides {n}")


def _mm_kernel(x_ref, w_ref, o_ref):
    o_ref[...] = jnp.dot(x_ref[...], w_ref[...], preferred_element_type=F32).astype(o_ref.dtype)


def matmul(x, w, *, tm, tn, out_dtype=F32):
    M, K = x.shape
    N = w.shape[1]
    assert M % tm == 0 and N % tn == 0
    return pl.pallas_call(
        _mm_kernel,
        grid=(M // tm, N // tn),
        in_specs=[pl.BlockSpec((tm, K), lambda i, j: (i, 0)),
                  pl.BlockSpec((K, tn), lambda i, j: (0, j))],
        out_specs=pl.BlockSpec((tm, tn), lambda i, j: (i, j)),
        out_shape=jax.ShapeDtypeStruct((M, N), out_dtype),
        compiler_params=_cparams("parallel", "arbitrary"),
        name="matmul",
    )(x, w)


def _layer_norm_rows(y, g, b):
    mu = jnp.mean(y, -1, keepdims=True)
    yc = y - mu
    var = jnp.mean(yc * yc, -1, keepdims=True)
    return yc * lax.rsqrt(var + LN_EPS) * g + b


def _mix_out_kernel(a_ref, b_ref, wa_ref, wb_ref, res_ref, g_ref, beta_ref, wq_ref, x_ref, q_ref, *, alpha):
    h = jnp.dot(a_ref[...], wa_ref[...], preferred_element_type=F32)
    h += jnp.dot(b_ref[...], wb_ref[...], preferred_element_type=F32)
    x = _layer_norm_rows(alpha * res_ref[...] + h, g_ref[...], beta_ref[...])
    x_ref[...] = x
    q_ref[...] = jnp.dot(x.astype(BF16), wq_ref[...], preferred_element_type=F32).astype(BF16)


def mix_out_ln_q(o_a, o_b, w_a, w_b, res, g, b, wq, *, alpha, tm):
    M, Ka = o_a.shape
    Kb = o_b.shape[1]
    D = w_a.shape[1]
    Q = wq.shape[1]
    assert M % tm == 0
    rows = lambda n: pl.BlockSpec((tm, n), lambda i: (i, 0))
    full = lambda r, n: pl.BlockSpec((r, n), lambda i: (0, 0))
    return pl.pallas_call(
        functools.partial(_mix_out_kernel, alpha=alpha),
        grid=(M // tm,),
        in_specs=[rows(Ka), rows(Kb), full(Ka, D), full(Kb, D), rows(D), full(1, D), full(1, D), full(D, Q)],
        out_specs=[rows(D), rows(Q)],
        out_shape=[jax.ShapeDtypeStruct((M, D), F32), jax.ShapeDtypeStruct((M, Q), BF16)],
        compiler_params=_cparams("parallel"),
        name="mix_out_ln_q",
    )(o_a, o_b, w_a, w_b, res, g.reshape(1, D), b.reshape(1, D), wq)


def _mm_ln_kernel(x_ref, w_ref, res_ref, g_ref, b_ref, o_ref, ob_ref, *, alpha):
    h = jnp.dot(x_ref[...], w_ref[...], preferred_element_type=F32)
    y = _layer_norm_rows(alpha * res_ref[...] + h, g_ref[...], b_ref[...])
    o_ref[...] = y
    ob_ref[...] = y.astype(BF16)


def matmul_ln(x, w, res, g, b, *, alpha, tm):
    M, K = x.shape
    D = w.shape[1]
    assert M % tm == 0
    return pl.pallas_call(
        functools.partial(_mm_ln_kernel, alpha=alpha),
        grid=(M // tm,),
        in_specs=[pl.BlockSpec((tm, K), lambda i: (i, 0)),
                  pl.BlockSpec((K, D), lambda i: (0, 0)),
                  pl.BlockSpec((tm, D), lambda i: (i, 0)),
                  pl.BlockSpec((1, D), lambda i: (0, 0)),
                  pl.BlockSpec((1, D), lambda i: (0, 0))],
        out_specs=[pl.BlockSpec((tm, D), lambda i: (i, 0))] * 2,
        out_shape=[jax.ShapeDtypeStruct((M, D), F32), jax.ShapeDtypeStruct((M, D), BF16)],
        compiler_params=_cparams("parallel"),
        name="matmul_ln",
    )(x, w, res, g.reshape(1, D), b.reshape(1, D))


def _ffn_ln_kernel(x_ref, wg_ref, wu_ref, wd_ref, res_ref, g_ref, b_ref, o_ref, ob_ref, acc_ref, *, alpha):
    f = pl.program_id(1)

    @pl.when(f == 0)
    def _():
        acc_ref[...] = jnp.zeros_like(acc_ref)

    x = x_ref[...]
    gate = jnp.dot(x, wg_ref[...], preferred_element_type=F32)
    up = jnp.dot(x, wu_ref[...], preferred_element_type=F32)
    hidden = (gate * jax.nn.sigmoid(gate) * up).astype(BF16)
    acc_ref[...] += jnp.dot(hidden, wd_ref[...], preferred_element_type=F32)

    @pl.when(f == pl.num_programs(1) - 1)
    def _():
        y = _layer_norm_rows(alpha * res_ref[...] + acc_ref[...], g_ref[...], b_ref[...])
        o_ref[...] = y
        ob_ref[...] = y.astype(BF16)


def ffn_ln(x, wg, wu, wd, res, g, b, *, alpha, tm, tf):
    M, D = x.shape
    Fd = wg.shape[1]
    assert M % tm == 0 and Fd % tf == 0
    return pl.pallas_call(
        functools.partial(_ffn_ln_kernel, alpha=alpha),
        grid=(M // tm, Fd // tf),
        in_specs=[pl.BlockSpec((tm, D), lambda i, f: (i, 0)),
                  pl.BlockSpec((D, tf), lambda i, f: (0, f)),
                  pl.BlockSpec((D, tf), lambda i, f: (0, f)),
                  pl.BlockSpec((tf, D), lambda i, f: (f, 0)),
                  pl.BlockSpec((tm, D), lambda i, f: (i, 0)),
                  pl.BlockSpec((1, D), lambda i, f: (0, 0)),
                  pl.BlockSpec((1, D), lambda i, f: (0, 0))],
        out_specs=[pl.BlockSpec((tm, D), lambda i, f: (i, 0))] * 2,
        out_shape=[jax.ShapeDtypeStruct((M, D), F32), jax.ShapeDtypeStruct((M, D), BF16)],
        scratch_shapes=[pltpu.VMEM((tm, D), F32)],
        compiler_params=_cparams("parallel", "arbitrary"),
        name="ffn_ln",
    )(x, wg, wu, wd, res, g.reshape(1, D), b.reshape(1, D))


def _mm(a, b, dims, mode):
    dot = functools.partial(lax.dot_general, dimension_numbers=dims, preferred_element_type=F32)
    if mode == 'f32':
        return dot(a, b, precision=HI)
    assert mode == 'bf16'
    return dot(a.astype(BF16), b.astype(BF16))


def _mm_exact_rhs(a, e):
    a1 = a.astype(BF16)
    r1 = a - a1.astype(F32)
    a2 = r1.astype(BF16)
    a3 = (r1 - a2.astype(F32)).astype(BF16)
    dot = functools.partial(jnp.dot, preferred_element_type=F32)
    return dot(a1, e) + (dot(a2, e) + dot(a3, e))


def _unit_lower_inverse(xs, c):
    row = lax.broadcasted_iota(jnp.int32, (c, c), 0)
    col = lax.broadcasted_iota(jnp.int32, (c, c), 1)
    eye = jnp.where(row == col, 1.0, 0.0).astype(F32)
    ts = [eye + x for x in xs]
    ps = list(xs)
    span = 2
    while span < c:
        ps = [_mm(p, p, _NN, 'bf16') for p in ps]
        ts = [t + _mm(t, p, _NN, 'bf16') for t, p in zip(ts, ps)]
        span *= 2
    return ts


def _shift_rows(x, carry_tail, j):
    c = x.shape[0]
    xj = pltpu.roll(x, j, 0)
    cj = pltpu.roll(carry_tail, j, 0)
    row = lax.broadcasted_iota(jnp.int32, (SUBLANES, 1), 0)
    top = jnp.where(row < j, cj, xj[:SUBLANES])
    return top if c == SUBLANES else jnp.concatenate([top, xj[SUBLANES:]], axis=0)


def _tri(c):
    return (lax.broadcasted_iota(jnp.int32, (c, c), 0) >= lax.broadcasted_iota(jnp.int32, (c, c), 1)).astype(F32)


def _valid_rows(c, t_valid):
    return lax.broadcasted_iota(jnp.int32, (c, 1), 0) < t_valid


def _rwkv_kernel(r_ref, k_ref, v_ref, sm_ref, prev_ref, mu_ref, vec_ref, w2_ref, a2_ref, g2_ref, e_ref, et_ref,
                 s0_ref, o_ref, st_ref, s_scr, tail_scr, o_scr, *, C, H, N, BB, t_valid):
    c = pl.program_id(1)
    W = H * N
    sw, sa, sg = w2_ref.shape[0], a2_ref.shape[0], g2_ref.shape[0]

    @pl.when(c == 0)
    def _():
        s_scr[...] = s0_ref[...]
        tail_scr[...] = prev_ref[...]

    row2 = lax.broadcasted_iota(jnp.int32, (2 * C, 2 * C), 0)
    col2 = lax.broadcasted_iota(jnp.int32, (2 * C, 2 * C), 1)
    rowc = jnp.where(row2 >= C, row2 - C, row2)
    colc = jnp.where(col2 >= C, col2 - C, col2)
    keep = jnp.where(row2 >= C, rowc, rowc - 1) >= colc
    tri = _tri(C)
    first = lax.broadcasted_iota(jnp.int32, (C, 1), 0) == 0
    valid = _valid_rows(C, t_valid)
    mid = C // 2 - 1
    vec = vec_ref[...]
    w0, a0, k_k, k_a, r_k, ln_w, ln_b = (vec[j:j + 1] for j in range(7))
    mu = mu_ref[...]
    e_sum, e_exp = e_ref[...], et_ref[...]
    head_sum = lambda t: _mm_exact_rhs(t, e_sum)
    head_bcast = lambda t: _mm_exact_rhs(t, e_exp)

    def one_seq(i, carry):
        rows = pl.ds(i * C, C)
        tail = tail_scr[i]

        def shifted(x, lo):
            hi = lo + x.shape[1]
            prev = jnp.where(first, tail[:, lo:hi], pltpu.roll(x, 1, 0))
            return x + (prev - x) * mu[:, lo:hi]

        xr, xk, xv, xs = r_ref[rows, :], k_ref[rows, :], v_ref[rows, :], sm_ref[rows, :]
        r, k, v, ms = shifted(xr, 0), shifted(xk, W), shifted(xv, 2 * W), shifted(xs, 3 * W)
        for x, lo in ((xr, 0), (xk, W), (xv, 2 * W), (xs, 3 * W)):
            tail_scr[i, :, lo:lo + x.shape[1]] = x[C - 1:C]

        lora_w = jnp.dot(jnp.tanh(ms[:, :sw]).astype(BF16), w2_ref[...], preferred_element_type=F32)
        lora_a = jnp.dot(ms[:, sw:sw + sa].astype(BF16), a2_ref[...], preferred_element_type=F32)
        gate = jnp.dot(jax.nn.sigmoid(ms[:, sw + sa:sw + sa + sg]).astype(BF16), g2_ref[...],
                       preferred_element_type=F32)
        lw = -jnp.exp(-jax.nn.softplus(-(w0 + lora_w)) - 0.5)
        a = jax.nn.sigmoid(a0 + lora_a)
        kk = k * k_k
        kk = kk * head_bcast(lax.rsqrt(head_sum(kk * kk) + L2_EPS))
        k = k * (1.0 + (a - 1.0) * k_a)
        if t_valid < C:
            zero = lambda t: jnp.where(valid, t, 0.0)
            r, k, v, kk, lw = zero(r), zero(k), zero(v), zero(kk), zero(lw)

        G = _mm(tri, lw, _NN, 'f32')
        Gm = G - G[mid:mid + 1, :]
        g_last = G[C - 1:C, :]
        e_out = jnp.exp(-Gm)
        e_tail = jnp.exp(g_last - G)
        b_in = kk * a
        AR = jnp.concatenate([-kk * jnp.exp(Gm - lw), r * jnp.exp(Gm)], axis=0)
        BK = jnp.concatenate([b_in * e_out, k * e_out], axis=0)
        BKt = jnp.concatenate([b_in * e_tail, k * e_tail], axis=0)
        e_mid = jnp.exp(G[mid:mid + 1, :])
        e_last = jnp.exp(g_last)

        hd = range(H)
        sl = [slice(h * N, (h + 1) * N) for h in hd]
        S = [s_scr[i, h] for h in hd]
        M = [jnp.where(keep, _mm(AR[:, sl[h]], BK[:, sl[h]], _NT, 'bf16'), 0.0)
             for h in hd]
        SR = [_mm(AR[:, sl[h]], S[h] * e_mid[:, sl[h]], _NT, 'bf16') for h in hd]
        AV = [_mm(M[h][:C, C:], v[:, sl[h]], _NN, 'bf16') for h in hd]
        T = _unit_lower_inverse([M[h][:C, :C] for h in hd], C)
        U = [_mm(T[h], SR[h][:C] + AV[h], _NN, 'bf16') for h in hd]
        UV = [jnp.concatenate([U[h], v[:, sl[h]]], axis=0) for h in hd]
        O = [SR[h][C:] + _mm(M[h][C:], UV[h], _NN, 'bf16') for h in hd]
        S = [S[h] * e_last[:, sl[h]] + _mm(UV[h], BKt[:, sl[h]], _TN, 'bf16') for h in hd]
        for h in hd:
            o_scr[:, sl[h]] = O[h]
            s_scr[i, h] = S[h]

        o = o_scr[...]
        o = o - head_bcast(head_sum(o) * (1.0 / N))
        o = o * head_bcast(lax.rsqrt(head_sum(o * o) * (1.0 / N) + RW_GN_EPS)) * ln_w + ln_b
        bonus = head_bcast(head_sum(r * k * r_k)) * v
        o_ref[rows, :] = ((o + bonus) * gate).astype(o_ref.dtype)
        return carry

    if BB == 1:
        one_seq(0, 0)
    else:
        lax.fori_loop(0, BB, one_seq, 0)

    @pl.when(c == pl.num_programs(1) - 1)
    def _():
        st_ref[...] = s_scr[...]


def rwkv7_mixer(P, row0, prev, s0, prm, *, chunk, seqs, t_valid):
    B, H, N, _ = s0.shape
    W = H * N
    SW = prm['mu'].shape[1] - 3 * W
    T = prm['T']
    nc = T // chunk
    blk = seqs * chunk
    assert T % chunk == 0 and B % seqs == 0 and row0 % blk == 0 and (7 * W) % SW == 0
    rb0 = row0 // blk
    col = lambda j: pl.BlockSpec((blk, W), lambda i, c: (rb0 + i * nc + c, j))
    full = lambda a: pl.BlockSpec(a.shape, lambda i, c: (0,) * a.ndim)
    st = pl.BlockSpec((seqs, H, N, N), lambda i, c: (i, 0, 0, 0))
    consts = [prm['mu'], prm['vec'], prm['w2'], prm['a2'], prm['g2'], prm['e_sum'], prm['e_exp']]
    return pl.pallas_call(
        functools.partial(_rwkv_kernel, C=chunk, H=H, N=N, BB=seqs, t_valid=t_valid),
        grid=(B // seqs, nc),
        in_specs=[col(0), col(1), col(2),
                  pl.BlockSpec((blk, SW), lambda i, c: (rb0 + i * nc + c, 7 * W // SW)),
                  pl.BlockSpec((seqs, 1, 3 * W + SW), lambda i, c: (i, 0, 0))]
                 + [full(a) for a in consts] + [st],
        out_specs=[pl.BlockSpec((blk, W), lambda i, c: (i * nc + c, 0)), st],
        out_shape=[jax.ShapeDtypeStruct((B * T, W), BF16), jax.ShapeDtypeStruct((B, H, N, N), F32)],
        scratch_shapes=[pltpu.VMEM((seqs, H, N, N), F32), pltpu.VMEM((seqs, 1, 3 * W + SW), F32),
                        pltpu.VMEM((chunk, W), F32)],
        compiler_params=_cparams("parallel", "arbitrary"),
        name="rwkv7_mixer",
    )(P, P, P, P, prev, *consts, s0)


def _gdn_kernel(q_ref, k_ref, v_ref, z_ref, sm_ref, cprev_ref, cw_ref, gate_ref, nw_ref, s0_ref, o_ref, st_ref,
                s_scr, tail_scr, *, C, H, N, BB, t_valid, gate_lane):
    c = pl.program_id(1)
    W = H * N

    @pl.when(c == 0)
    def _():
        s_scr[...] = s0_ref[...]
        tail_scr[...] = cprev_ref[...]

    row2 = lax.broadcasted_iota(jnp.int32, (2 * C, C), 0)
    col2 = lax.broadcasted_iota(jnp.int32, (2 * C, C), 1)
    keep = jnp.where(row2 >= C, row2 - C, row2 - 1) >= col2
    tri = _tri(C)
    valid = _valid_rows(C, t_valid)
    cw = cw_ref[...]
    a_scale = -jnp.exp(gate_ref[0:1, :])
    dt_bias = gate_ref[1:2, :]
    nw = nw_ref[...]

    def one_seq(i, carry):
        rows = pl.ds(i * C, C)
        tail = tail_scr[i]

        def conv_silu(x, lo):
            hi = lo + W
            acc = x * cw[GD_CONV - 1:GD_CONV, lo:hi]
            for j in range(1, GD_CONV):
                acc = acc + _shift_rows(x, tail[:, lo:hi], j) * cw[GD_CONV - 1 - j:GD_CONV - j, lo:hi]
            return acc * jax.nn.sigmoid(acc)

        xq, xk, xv = q_ref[rows, :], k_ref[rows, :], v_ref[rows, :]
        q_all, k_all, v_all = conv_silu(xq, 0), conv_silu(xk, W), conv_silu(xv, 2 * W)
        for x, lo in ((xq, 0), (xk, W), (xv, 2 * W)):
            tail_scr[i, :, lo:lo + W] = x[C - SUBLANES:C]
        z_all = z_ref[rows, :]
        gates = sm_ref[rows, gate_lane:gate_lane + LANES]
        beta = jax.nn.sigmoid(gates[:, :H])
        g = a_scale * jax.nn.softplus(gates[:, H:2 * H] + dt_bias)
        if t_valid < C:
            beta, g = jnp.where(valid, beta, 0.0), jnp.where(valid, g, 0.0)
            q_all, k_all, v_all = (jnp.where(valid, t, 0.0) for t in (q_all, k_all, v_all))
        Gc = _mm(tri, g, _NN, 'f32')
        Gr = _mm(g, tri, _TT, 'f32')

        hd = range(H)
        sl = [slice(h * N, (h + 1) * N) for h in hd]
        l2 = lambda t: t * lax.rsqrt(jnp.sum(t * t, -1, keepdims=True) + L2_EPS)
        q = [l2(q_all[:, sl[h]]) * (N ** -0.5) for h in hd]
        k = [l2(k_all[:, sl[h]]) for h in hd]
        S = [s_scr[i, h] for h in hd]
        gc = [Gc[:, h:h + 1] for h in hd]
        g_last = [Gc[C - 1:C, h:h + 1] for h in hd]
        gc2 = [jnp.concatenate([gc[h], gc[h]], axis=0) for h in hd]
        decay = [jnp.where(keep, jnp.exp(jnp.where(keep, gc2[h] - Gr[h:h + 1, :], 0.0)), 0.0) for h in hd]
        bh = [beta[:, h:h + 1] for h in hd]
        KQ = [jnp.concatenate([k[h] * bh[h], q[h]], axis=0) for h in hd]
        M = [_mm(KQ[h], k[h], _NT, 'bf16') * decay[h] for h in hd]
        SR = [_mm(KQ[h] * jnp.exp(gc2[h]), S[h], _NN, 'bf16') for h in hd]
        T = _unit_lower_inverse([-M[h][:C] for h in hd], C)
        v_new = [_mm(T[h], v_all[:, sl[h]] * bh[h] - SR[h][:C], _NN, 'bf16') for h in hd]
        O = [SR[h][C:] + _mm(M[h][C:], v_new[h], _NN, 'bf16') for h in hd]
        S = [S[h] * jnp.exp(g_last[h]) + _mm(k[h] * jnp.exp(g_last[h] - gc[h]), v_new[h], _TN, 'bf16')
             for h in hd]
        for h in hd:
            zh = z_all[:, sl[h]]
            o = O[h] * lax.rsqrt(jnp.mean(O[h] * O[h], -1, keepdims=True) + GD_NORM_EPS) * nw
            o_ref[rows, sl[h]] = (o * (zh * jax.nn.sigmoid(zh))).astype(o_ref.dtype)
            s_scr[i, h] = S[h]
        return carry

    if BB == 1:
        one_seq(0, 0)
    else:
        lax.fori_loop(0, BB, one_seq, 0)

    @pl.when(c == pl.num_programs(1) - 1)
    def _():
        st_ref[...] = s_scr[...]


def gdn_mixer(P, row0, conv_prev, s0, prm, *, chunk, seqs, t_valid):
    B, H, N, _ = s0.shape
    W = H * N
    SW = prm['SW']
    T = prm['T']
    nc = T // chunk
    blk = seqs * chunk
    assert T % chunk == 0 and B % seqs == 0 and row0 % blk == 0 and chunk >= SUBLANES and 2 * H <= LANES
    rb0 = row0 // blk
    col = lambda j: pl.BlockSpec((blk, W), lambda i, c: (rb0 + i * nc + c, j))
    full = lambda a: pl.BlockSpec(a.shape, lambda i, c: (0,) * a.ndim)
    st = pl.BlockSpec((seqs, H, N, N), lambda i, c: (i, 0, 0, 0))
    consts = [prm['conv_w'], prm['gate'], prm['norm_w']]
    return pl.pallas_call(
        functools.partial(_gdn_kernel, C=chunk, H=H, N=N, BB=seqs, t_valid=t_valid, gate_lane=SW - LANES),
        grid=(B // seqs, nc),
        in_specs=[col(3), col(4), col(5), col(6),
                  pl.BlockSpec((blk, SW), lambda i, c: (rb0 + i * nc + c, 7 * W // SW)),
                  pl.BlockSpec((seqs, SUBLANES, 3 * W), lambda i, c: (i, 0, 0))]
                 + [full(a) for a in consts] + [st],
        out_specs=[pl.BlockSpec((blk, W), lambda i, c: (i * nc + c, 0)), st],
        out_shape=[jax.ShapeDtypeStruct((B * T, W), BF16), jax.ShapeDtypeStruct((B, H, N, N), F32)],
        scratch_shapes=[pltpu.VMEM((seqs, H, N, N), F32), pltpu.VMEM((seqs, SUBLANES, 3 * W), F32)],
        compiler_params=_cparams("parallel", "arbitrary"),
        name="gdn_mixer",
    )(P, P, P, P, P, conv_prev, *consts, s0)


def _xattn_kernel(q_ref, k_ref, v_ref, o_ref, *, H, Dh, BB, tq):
    scale = Dh ** -0.5
    pairs = [(b, h) for b in range(BB) for h in range(H)]
    sl = [slice(h * Dh, (h + 1) * Dh) for h in range(H)]
    k = [k_ref[b].astype(BF16) for b in range(BB)]
    v = [v_ref[b].astype(BF16) for b in range(BB)]
    q = [q_ref[b * tq:(b + 1) * tq, :] for b in range(BB)]
    s = [lax.dot_general(q[b][:, sl[h]], k[b][:, sl[h]], _NT, preferred_element_type=F32) * scale
         for b, h in pairs]
    e = [jnp.exp(t - jnp.max(t, -1, keepdims=True)) for t in s]
    pr = [(t / jnp.sum(t, -1, keepdims=True)).astype(BF16) for t in e]
    o = [jnp.dot(pr[j], v[b][:, sl[h]], preferred_element_type=F32) for j, (b, h) in enumerate(pairs)]
    for j, (b, h) in enumerate(pairs):
        o_ref[b * tq:(b + 1) * tq, sl[h]] = o[j].astype(o_ref.dtype)


def cross_attention(q, row0, mk, mv, *, heads, tq, seqs, q_tiles):
    B, Mm, Wd = mk.shape
    blk = seqs * tq
    assert B % seqs == 0 and row0 % blk == 0 and (seqs == 1 or q_tiles == 1)
    rb0 = row0 // blk
    kv = pl.BlockSpec((seqs, Mm, Wd), lambda i, t: (i, 0, 0))
    return pl.pallas_call(
        functools.partial(_xattn_kernel, H=heads, Dh=Wd // heads, BB=seqs, tq=tq),
        grid=(B // seqs, q_tiles),
        in_specs=[pl.BlockSpec((blk, Wd), lambda i, t: (rb0 + i * q_tiles + t, 0)), kv, kv],
        out_specs=pl.BlockSpec((blk, Wd), lambda i, t: (i * q_tiles + t, 0)),
        out_shape=jax.ShapeDtypeStruct((B * q_tiles * tq, Wd), BF16),
        compiler_params=_cparams("parallel", "arbitrary"),
        name="cross_attention",
    )(q, mk, mv)


def _lane_pad(a, width):
    return jnp.pad(a, [(0, 0)] * (a.ndim - 1) + [(0, width - a.shape[-1])])


def _row_pad(a, rows):
    return jnp.pad(a, [(0, rows - a.shape[0])] + [(0, 0)] * (a.ndim - 1))


def kernel(x_prompt, mem_prompt, x_sample, state_rwkv, state_shift, state_gdn, state_conv, cache_mem_k, cache_mem_v, w_in, mu_shift, rw_w0, rw_w2, rw_a0, rw_a2, rw_g2, rw_kk, rw_ka, rw_rk, rw_lnx_w, rw_lnx_b, gd_conv_w, gd_a_log, gd_dt_bias, gd_norm_w, w_out, ln1_g, ln1_b, xa_wq, xa_wk, xa_wv, xa_wo, ln2_g, ln2_b, ff_wg, ff_wu, ff_wd, ln3_g, ln3_b):
    depth = w_in.shape[0]
    Bp, Tp, D = x_prompt.shape
    Bs, Ts, _ = x_sample.shape
    Hr, Hg = state_rwkv.shape[2], state_gdn.shape[2]
    W = Hr * RW_HEAD
    assert W == Hg * GD_HEAD and Ts <= SAMPLE_STEPS and Ts >= GD_CONV - 1 and Tp % PROMPT_CHUNK == 0
    n_w, n_a, n_g = rw_w2.shape[1], rw_a2.shape[1], rw_g2.shape[1]
    sw, sa, sg = (_round_up(n, LANES) for n in (n_w, n_a, n_g))
    SW = sw + sa + sg + LANES
    n_mem, xa_heads = cache_mem_k.shape[2], cache_mem_k.shape[3]
    xa_width = xa_heads * cache_mem_k.shape[4]
    alpha = (2 * depth) ** 0.25
    Mp, Ms = Bp * Tp, Bs * SAMPLE_STEPS
    M = Mp + Ms
    tm = _pick_tile(M, (512, 384, 256, 128))
    tf = _pick_tile(ff_wg.shape[-1], (512, 256, 128))
    c_w, c_a, c_g = 3 * W, 3 * W + n_w, 3 * W + n_w + n_a
    c_gd = c_g + n_g

    def to_layout(t):
        parts = [t[..., :c_w], t[..., c_gd:c_gd + 4 * W],
                 _lane_pad(t[..., c_w:c_a], sw), _lane_pad(t[..., c_a:c_g], sa), _lane_pad(t[..., c_g:c_gd], sg),
                 _lane_pad(t[..., c_gd + 4 * W:], LANES)]
        return jnp.concatenate(parts, axis=-1)

    def rwkv_layout(t):
        parts = [t[..., :c_w], _lane_pad(t[..., c_w:c_a], sw), _lane_pad(t[..., c_a:c_g], sa),
                 _lane_pad(t[..., c_g:c_gd], sg + LANES)]
        return jnp.concatenate(parts, axis=-1)

    def rwkv_cols(t):
        s0 = 7 * W
        return jnp.concatenate([t[..., :3 * W], t[..., s0:s0 + n_w], t[..., s0 + sw:s0 + sw + n_a],
                                t[..., s0 + sw + sa:s0 + sw + sa + n_g]], axis=-1)

    heads_of = jnp.arange(W) // RW_HEAD
    e_sum = (heads_of[:, None] == jnp.arange(LANES)[None, :]).astype(BF16)
    x_s = jnp.pad(x_sample, ((0, 0), (0, SAMPLE_STEPS - Ts), (0, 0)))
    x = jnp.concatenate([x_prompt.reshape(Mp, D), x_s.reshape(Ms, D)], axis=0)
    xb = x.astype(BF16)
    mem = mem_prompt.reshape(Bp * n_mem, D).astype(BF16)
    zeros = lambda *s: jnp.zeros(s, F32)

    outs = {n: [] for n in ('p_rw', 'p_sh', 'p_gd', 'p_cv', 'p_mk', 'p_mv', 's_rw', 's_sh', 's_gd', 's_cv')}
    for l in range(depth):
        w_l = to_layout(w_in[l]).astype(BF16)
        rw = {'T': None, 'mu': rwkv_layout(mu_shift[l])[None],
              'vec': jnp.stack([rw_w0[l], rw_a0[l], rw_kk[l], rw_ka[l], rw_rk[l].reshape(W), rw_lnx_w[l],
                                rw_lnx_b[l], jnp.zeros((W,), F32)]),
              'w2': _row_pad(rw_w2[l], sw).astype(BF16), 'a2': _row_pad(rw_a2[l], sa).astype(BF16),
              'g2': _row_pad(rw_g2[l], sg).astype(BF16), 'e_sum': e_sum, 'e_exp': e_sum.T}
        gd = {'T': None, 'SW': SW, 'conv_w': gd_conv_w[l], 'gate': jnp.stack([gd_a_log[l], gd_dt_bias[l]]),
              'norm_w': gd_norm_w[l][None]}

        kv = matmul(mem, jnp.concatenate([xa_wk[l], xa_wv[l]], axis=1).astype(BF16),
                    tm=_pick_tile(Bp * n_mem, (512, 256, 128)), tn=_pick_tile(2 * xa_width, (512, 256, 128)))
        mk = kv[:, :xa_width].reshape(Bp, n_mem, xa_width)
        mv = kv[:, xa_width:].reshape(Bp, n_mem, xa_width)

        P = matmul(xb, w_l, tm=tm, tn=_pick_tile(7 * W + SW, (512, 256, 128)))
        P_p = P[:Mp].reshape(Bp, Tp, -1)
        P_s = P[Mp:].reshape(Bs, SAMPLE_STEPS, -1)

        prev_s = rwkv_layout(state_shift[l])[:, None]
        conv_s = jnp.pad(state_conv[l], ((0, 0), (SUBLANES - (GD_CONV - 1), 0), (0, 0)))
        o_rw_p, rw_p = rwkv7_mixer(P, 0, zeros(Bp, 1, 3 * W + SW), zeros(Bp, Hr, RW_HEAD, RW_HEAD),
                                   dict(rw, T=Tp), chunk=PROMPT_CHUNK, seqs=1, t_valid=PROMPT_CHUNK)
        o_gd_p, gd_p = gdn_mixer(P, 0, zeros(Bp, SUBLANES, 3 * W), zeros(Bp, Hg, GD_HEAD, GD_HEAD),
                                 dict(gd, T=Tp), chunk=PROMPT_CHUNK, seqs=1, t_valid=PROMPT_CHUNK)
        o_rw_s, rw_s = rwkv7_mixer(P, Mp, prev_s, state_rwkv[l], dict(rw, T=SAMPLE_STEPS),
                                   chunk=SAMPLE_STEPS, seqs=SAMPLE_SEQS, t_valid=Ts)
        o_gd_s, gd_s = gdn_mixer(P, Mp, conv_s, state_gdn[l], dict(gd, T=SAMPLE_STEPS),
                                 chunk=SAMPLE_STEPS, seqs=SAMPLE_SEQS, t_valid=Ts)

        w_o = w_out[l].astype(BF16)
        x, q = mix_out_ln_q(jnp.concatenate([o_rw_p, o_rw_s], axis=0), jnp.concatenate([o_gd_p, o_gd_s], axis=0),
                            w_o[:W], w_o[W:], x, ln1_g[l], ln1_b[l], xa_wq[l].astype(BF16), alpha=alpha, tm=tm)

        tq = _pick_tile(Tp, (512, 256, 128))
        att_p = cross_attention(q, 0, mk, mv, heads=xa_heads, tq=tq, seqs=1, q_tiles=Tp // tq)
        att_s = cross_attention(q, Mp, cache_mem_k[l].reshape(Bs, n_mem, xa_width),
                                cache_mem_v[l].reshape(Bs, n_mem, xa_width), heads=xa_heads, tq=SAMPLE_STEPS,
                                seqs=SAMPLE_SEQS, q_tiles=1)
        x, xb = matmul_ln(jnp.concatenate([att_p, att_s], axis=0), xa_wo[l].astype(BF16), x, ln2_g[l], ln2_b[l],
                          alpha=alpha, tm=tm)
        x, xb = ffn_ln(xb, ff_wg[l].astype(BF16), ff_wu[l].astype(BF16), ff_wd[l].astype(BF16), x,
                       ln3_g[l], ln3_b[l], alpha=alpha, tm=tm, tf=tf)

        outs['p_rw'].append(rw_p); outs['p_gd'].append(gd_p); outs['s_rw'].append(rw_s); outs['s_gd'].append(gd_s)
        outs['p_sh'].append(rwkv_cols(P_p[:, Tp - 1])); outs['s_sh'].append(rwkv_cols(P_s[:, Ts - 1]))
        outs['p_cv'].append(P_p[:, Tp - (GD_CONV - 1):, 3 * W:6 * W])
        outs['s_cv'].append(P_s[:, Ts - (GD_CONV - 1):Ts, 3 * W:6 * W])
        outs['p_mk'].append(mk.reshape(Bp, n_mem, xa_heads, -1)); outs['p_mv'].append(mv.reshape(Bp, n_mem, xa_heads, -1))

    st = {n: jnp.stack(v) for n, v in outs.items()}
    return (x[:Mp].reshape(Bp, Tp, D), x[Mp:].reshape(Bs, SAMPLE_STEPS, D)[:, :Ts], st['p_rw'], st['p_sh'],
            st['p_gd'], st['p_cv'], st['p_mk'], st['p_mv'], st['s_rw'], st['s_sh'], st['s_gd'], st['s_cv'])
```

```python
import functools

import jax
import jax.numpy as jnp
from jax import lax
from jax.experimental import pallas as pl
from jax.experimental.pallas import tpu as pltpu

F32 = jnp.float32
BF16 = jnp.bfloat16

LANES = 128
SUBLANES = 8
RW_HEAD = 64
GD_HEAD = 128
GD_CONV = 4
LN_EPS = 1e-5
RW_GN_EPS = 64e-5
GD_NORM_EPS = 1e-6
L2_EPS = 1e-12

PROMPT_CHUNK = 64
SAMPLE_STEPS = SUBLANES
SAMPLE_SEQS = 8

VMEM_LIMIT_BYTES = 56 * 1024 * 1024

_NN = (((1,), (0,)), ((), ()))
_NT = (((1,), (1,)), ((), ()))
_TN = (((0,), (0,)), ((), ()))
_TT = (((0,), (1,)), ((), ()))


def _cparams(*sem):
    return pltpu.CompilerParams(dimension_semantics=sem, vmem_limit_bytes=VMEM_LIMIT_BYTES)


def _round_up(n, m):
    return -(-n // m) * m


def _pick_tile(n, candidates):
    for c in candidates:
        if n % c == 0:
            return c
    raise ValueError(f"no tile in {candidates} divides {n}")


def _mm_kernel(x_ref, w_ref, o_ref):
    o_ref[...] = jnp.dot(x_ref[...], w_ref[...], preferred_element_type=F32).astype(o_ref.dtype)


def matmul(x, w, layer, *, tm, tn, out_dtype=F32):
    M, K = x.shape
    N = w.shape[2]
    assert M % tm == 0 and N % tn == 0
    return pl.pallas_call(
        _mm_kernel,
        grid=(M // tm, N // tn),
        in_specs=[pl.BlockSpec((tm, K), lambda i, j: (i, 0)),
                  pl.BlockSpec((None, K, tn), lambda i, j: (layer, 0, j))],
        out_specs=pl.BlockSpec((tm, tn), lambda i, j: (i, j)),
        out_shape=jax.ShapeDtypeStruct((M, N), out_dtype),
        compiler_params=_cparams("parallel", "arbitrary"),
        name="matmul",
    )(x, w)


def _layer_norm_rows(y, g, b):
    mu = jnp.mean(y, -1, keepdims=True)
    yc = y - mu
    var = jnp.mean(yc * yc, -1, keepdims=True)
    return yc * lax.rsqrt(var + LN_EPS) * g + b


def _mix_out_kernel(a_ref, b_ref, wa_ref, wb_ref, res_ref, g_ref, beta_ref, wq_ref, x_ref, q_ref, *, alpha):
    h = jnp.dot(a_ref[...], wa_ref[...], preferred_element_type=F32)
    h += jnp.dot(b_ref[...], wb_ref[...], preferred_element_type=F32)
    x = _layer_norm_rows(alpha * res_ref[...] + h, g_ref[...], beta_ref[...])
    x_ref[...] = x
    q_ref[...] = jnp.dot(x.astype(BF16), wq_ref[...], preferred_element_type=F32).astype(BF16)


def mix_out_ln_q(o_a, o_b, w, wq, layer, res, g, b, *, alpha, tm):
    M, Ka = o_a.shape
    Kb = o_b.shape[1]
    D = w.shape[2]
    Q = wq.shape[2]
    assert M % tm == 0 and Ka == Kb and w.shape[1] == Ka + Kb
    rows = lambda n: pl.BlockSpec((tm, n), lambda i: (i, 0))
    vec = pl.BlockSpec((1, D), lambda i: (0, 0))
    return pl.pallas_call(
        functools.partial(_mix_out_kernel, alpha=alpha),
        grid=(M // tm,),
        in_specs=[rows(Ka), rows(Kb),
                  pl.BlockSpec((None, Ka, D), lambda i: (layer, 0, 0)),
                  pl.BlockSpec((None, Kb, D), lambda i: (layer, 1, 0)),
                  rows(D), vec, vec,
                  pl.BlockSpec((None, D, Q), lambda i: (layer, 0, 0))],
        out_specs=[rows(D), rows(Q)],
        out_shape=[jax.ShapeDtypeStruct((M, D), F32), jax.ShapeDtypeStruct((M, Q), BF16)],
        compiler_params=_cparams("parallel"),
        name="mix_out_ln_q",
    )(o_a, o_b, w, w, res, g.reshape(1, D), b.reshape(1, D), wq)


def _mm_ln_kernel(x_ref, w_ref, res_ref, g_ref, b_ref, o_ref, ob_ref, *, alpha):
    h = jnp.dot(x_ref[...], w_ref[...], preferred_element_type=F32)
    y = _layer_norm_rows(alpha * res_ref[...] + h, g_ref[...], b_ref[...])
    o_ref[...] = y
    ob_ref[...] = y.astype(BF16)


def matmul_ln(x, w, layer, res, g, b, *, alpha, tm):
    M, K = x.shape
    D = w.shape[2]
    assert M % tm == 0
    return pl.pallas_call(
        functools.partial(_mm_ln_kernel, alpha=alpha),
        grid=(M // tm,),
        in_specs=[pl.BlockSpec((tm, K), lambda i: (i, 0)),
                  pl.BlockSpec((None, K, D), lambda i: (layer, 0, 0)),
                  pl.BlockSpec((tm, D), lambda i: (i, 0)),
                  pl.BlockSpec((1, D), lambda i: (0, 0)),
                  pl.BlockSpec((1, D), lambda i: (0, 0))],
        out_specs=[pl.BlockSpec((tm, D), lambda i: (i, 0))] * 2,
        out_shape=[jax.ShapeDtypeStruct((M, D), F32), jax.ShapeDtypeStruct((M, D), BF16)],
        compiler_params=_cparams("parallel"),
        name="matmul_ln",
    )(x, w, res, g.reshape(1, D), b.reshape(1, D))


def _ffn_ln_kernel(x_ref, wg_ref, wu_ref, wd_ref, res_ref, g_ref, b_ref, o_ref, ob_ref, acc_ref, *, alpha):
    f = pl.program_id(1)

    @pl.when(f == 0)
    def _():
        acc_ref[...] = jnp.zeros_like(acc_ref)

    x = x_ref[...]
    gate = jnp.dot(x, wg_ref[...], preferred_element_type=F32)
    up = jnp.dot(x, wu_ref[...], preferred_element_type=F32)
    hidden = (gate * jax.nn.sigmoid(gate) * up).astype(BF16)
    acc_ref[...] += jnp.dot(hidden, wd_ref[...], preferred_element_type=F32)

    @pl.when(f == pl.num_programs(1) - 1)
    def _():
        y = _layer_norm_rows(alpha * res_ref[...] + acc_ref[...], g_ref[...], b_ref[...])
        o_ref[...] = y
        ob_ref[...] = y.astype(BF16)


def ffn_ln(x, wg, wu, wd, layer, res, g, b, *, alpha, tm, tf):
    M, D = x.shape
    Fd = wg.shape[2]
    assert M % tm == 0 and Fd % tf == 0
    return pl.pallas_call(
        functools.partial(_ffn_ln_kernel, alpha=alpha),
        grid=(M // tm, Fd // tf),
        in_specs=[pl.BlockSpec((tm, D), lambda i, f: (i, 0)),
                  pl.BlockSpec((None, D, tf), lambda i, f: (layer, 0, f)),
                  pl.BlockSpec((None, D, tf), lambda i, f: (layer, 0, f)),
                  pl.BlockSpec((None, tf, D), lambda i, f: (layer, f, 0)),
                  pl.BlockSpec((tm, D), lambda i, f: (i, 0)),
                  pl.BlockSpec((1, D), lambda i, f: (0, 0)),
                  pl.BlockSpec((1, D), lambda i, f: (0, 0))],
        out_specs=[pl.BlockSpec((tm, D), lambda i, f: (i, 0))] * 2,
        out_shape=[jax.ShapeDtypeStruct((M, D), F32), jax.ShapeDtypeStruct((M, D), BF16)],
        scratch_shapes=[pltpu.VMEM((tm, D), F32)],
        compiler_params=_cparams("parallel", "arbitrary"),
        name="ffn_ln",
    )(x, wg, wu, wd, res, g.reshape(1, D), b.reshape(1, D))


def _mm(a, b, dims):
    return lax.dot_general(a.astype(BF16), b.astype(BF16), dims, preferred_element_type=F32)


def _three_bf16_terms(a):
    a1 = a.astype(BF16)
    r1 = a - a1.astype(F32)
    a2 = r1.astype(BF16)
    return a1, a2, (r1 - a2.astype(F32)).astype(BF16)


def _unit_lower_inverse(xs, c):
    row = lax.broadcasted_iota(jnp.int32, (c, c), 0)
    col = lax.broadcasted_iota(jnp.int32, (c, c), 1)
    eye = jnp.where(row == col, 1.0, 0.0).astype(F32)
    ts = [eye + x for x in xs]
    ps = list(xs)
    span = 2
    while span < c:
        ps = [_mm(p, p, _NN) for p in ps]
        ts = [t + _mm(t, p, _NN) for t, p in zip(ts, ps)]
        span *= 2
    return ts


def _step_in_seq(R, C):
    return lax.broadcasted_iota(jnp.int32, (R, 1), 0) & (C - 1)


def _seq_tri(R, C):
    row = lax.broadcasted_iota(jnp.int32, (R, R), 0)
    col = lax.broadcasted_iota(jnp.int32, (R, R), 1)
    return jnp.where(row >= col, jnp.where((row & -C) == (col & -C), 1.0, 0.0), 0.0).astype(BF16)


def _per_seq_rows(rows, C):
    rows = [jnp.broadcast_to(t, (C, t.shape[-1])) for t in rows]
    return rows[0] if len(rows) == 1 else jnp.concatenate(rows, axis=0)


def _delayed_rows(x, tails, j, BB, C):
    xj = pltpu.roll(x, j, 0)
    row = lax.broadcasted_iota(jnp.int32, (SUBLANES, 1), 0)
    pieces = []
    for i in range(BB):
        pieces.append(jnp.where(row < j, pltpu.roll(tails[i], j, 0), xj[i * C:i * C + SUBLANES]))
        if C > SUBLANES:
            pieces.append(xj[i * C + SUBLANES:(i + 1) * C])
    return pieces[0] if len(pieces) == 1 else jnp.concatenate(pieces, axis=0)


def _half_sums(x, N):
    low = lax.broadcasted_iota(jnp.int32, (1, LANES), 1) < N
    out = []
    for j in range(x.shape[1] // LANES):
        xv = x[:, j * LANES:(j + 1) * LANES]
        lo = jnp.sum(jnp.where(low, xv, 0.0), -1, keepdims=True)
        out += [lo, jnp.sum(jnp.where(low, 0.0, xv), -1, keepdims=True)]
    return out


def _half_bcast(cols, N):
    low = lax.broadcasted_iota(jnp.int32, (1, LANES), 1) < N
    return jnp.concatenate([jnp.where(low, cols[2 * j], cols[2 * j + 1]) for j in range(len(cols) // 2)], axis=1)


def _rwkv_kernel(r_ref, k_ref, v_ref, sm_ref, prev_ref, mu_ref, vec_ref, w2_ref, a2_ref, g2_ref,
                 s0_ref, o_ref, st_ref, s_scr, tail_scr, o_scr, *, C, H, N, BB, t_valid):
    c = pl.program_id(1)
    W = H * N
    R = BB * C
    sw, sa, sg = w2_ref.shape[0], a2_ref.shape[0], g2_ref.shape[0]

    @pl.when(c == 0)
    def _():
        s_scr[...] = s0_ref[...]
        tail_scr[...] = prev_ref[...]

    step = _step_in_seq(R, C)
    first = step == 0
    vec = vec_ref[...]
    w0, a0, k_k, k_a, r_k, ln_w, ln_b = (vec[j:j + 1] for j in range(7))
    mu = mu_ref[...]
    tail = _per_seq_rows([tail_scr[i] for i in range(BB)], C)

    def shifted(x, lo):
        hi = lo + x.shape[1]
        prev = jnp.where(first, tail[:, lo:hi], pltpu.roll(x, 1, 0))
        return x + (prev - x) * mu[:, lo:hi]

    xr, xk, xv, xs = r_ref[...], k_ref[...], v_ref[...], sm_ref[...]
    r, k, v, ms = shifted(xr, 0), shifted(xk, W), shifted(xv, 2 * W), shifted(xs, 3 * W)
    for x, lo in ((xr, 0), (xk, W), (xv, 2 * W), (xs, 3 * W)):
        for i in range(BB):
            tail_scr[i, :, lo:lo + x.shape[1]] = x[(i + 1) * C - 1:(i + 1) * C]

    lora_w = jnp.dot(jnp.tanh(ms[:, :sw]).astype(BF16), w2_ref[...], preferred_element_type=F32)
    lora_a = jnp.dot(ms[:, sw:sw + sa].astype(BF16), a2_ref[...], preferred_element_type=F32)
    gate = jnp.dot(jax.nn.sigmoid(ms[:, sw + sa:sw + sa + sg]).astype(BF16), g2_ref[...],
                   preferred_element_type=F32)
    lw = -jnp.exp(-jax.nn.softplus(-(w0 + lora_w)) - 0.5)
    a = jax.nn.sigmoid(a0 + lora_a)
    kk = k * k_k
    kk = kk * _half_bcast([lax.rsqrt(s + L2_EPS) for s in _half_sums(kk * kk, N)], N)
    k = k * (1.0 + (a - 1.0) * k_a)
    if t_valid < C:
        valid = step < t_valid
        r, k, v, kk, lw = (jnp.where(valid, t, 0.0) for t in (r, k, v, kk, lw))

    tri = _seq_tri(R, C)
    G = sum(jnp.dot(tri, t, preferred_element_type=F32) for t in _three_bf16_terms(lw))
    mid = C // 2 - 1
    g_mid = [G[i * C + mid:i * C + mid + 1] for i in range(BB)]
    g_last = [G[(i + 1) * C - 1:(i + 1) * C] for i in range(BB)]
    Gm = G - _per_seq_rows(g_mid, C)
    e_out = jnp.exp(-Gm)
    e_tail = jnp.exp(_per_seq_rows(g_last, C) - G)
    b_in = kk * a
    A_, R_ = -kk * jnp.exp(Gm - lw), r * jnp.exp(Gm)
    B_, K_ = b_in * e_out, k * e_out
    Bt, Kt = b_in * e_tail, k * e_tail
    e_mid, e_last = [jnp.exp(t) for t in g_mid], [jnp.exp(t) for t in g_last]

    row2 = lax.broadcasted_iota(jnp.int32, (2 * C, 2 * C), 0)
    col2 = lax.broadcasted_iota(jnp.int32, (2 * C, 2 * C), 1)
    rowc = jnp.where(row2 >= C, row2 - C, row2)
    colc = jnp.where(col2 >= C, col2 - C, col2)
    keep = jnp.where(row2 >= C, rowc, rowc - 1) >= colc

    ch = [(i, h) for i in range(BB) for h in range(H)]
    rs = lambda i: slice(i * C, (i + 1) * C)
    sl = lambda h: slice(h * N, (h + 1) * N)
    pair = lambda x, y, i, h: jnp.concatenate([x[rs(i), sl(h)], y[rs(i), sl(h)]], axis=0)
    S = [s_scr[i, h] for i, h in ch]
    AR = [pair(A_, R_, i, h) for i, h in ch]
    M = [jnp.where(keep, _mm(AR[j], pair(B_, K_, i, h), _NT), 0.0)
         for j, (i, h) in enumerate(ch)]
    SR = [_mm(AR[j], S[j] * e_mid[i][:, sl(h)], _NT) for j, (i, h) in enumerate(ch)]
    AV = [_mm(M[j][:C, C:], v[rs(i), sl(h)], _NN) for j, (i, h) in enumerate(ch)]
    T = _unit_lower_inverse([m[:C, :C] for m in M], C)
    U = [_mm(T[j], SR[j][:C] + AV[j], _NN) for j in range(len(ch))]
    UV = [jnp.concatenate([U[j], v[rs(i), sl(h)]], axis=0) for j, (i, h) in enumerate(ch)]
    O = [SR[j][C:] + _mm(M[j][C:], UV[j], _NN) for j in range(len(ch))]
    S = [S[j] * e_last[i][:, sl(h)] + _mm(UV[j], pair(Bt, Kt, i, h), _TN) for j, (i, h) in enumerate(ch)]
    for j, (i, h) in enumerate(ch):
        o_scr[rs(i), sl(h)] = O[j]
        s_scr[i, h] = S[j]

    o = o_scr[...]
    o = o - _half_bcast([s * (1.0 / N) for s in _half_sums(o, N)], N)
    o = o * _half_bcast([lax.rsqrt(s * (1.0 / N) + RW_GN_EPS) for s in _half_sums(o * o, N)], N) * ln_w + ln_b
    bonus = _half_bcast(_half_sums(r * k * r_k, N), N) * v
    o_ref[...] = ((o + bonus) * gate).astype(o_ref.dtype)

    @pl.when(c == pl.num_programs(1) - 1)
    def _():
        st_ref[...] = s_scr[...]


def rwkv7_mixer(P, row0, prev, s0, prm, *, chunk, seqs, t_valid):
    B, H, N, _ = s0.shape
    W = H * N
    SW = prm['mu'].shape[1] - 3 * W
    T = prm['T']
    nc = T // chunk
    blk = seqs * chunk
    assert T % chunk == 0 and B % seqs == 0 and row0 % blk == 0 and (7 * W) % SW == 0
    assert 2 * N == LANES and chunk & (chunk - 1) == 0
    rb0 = row0 // blk
    col = lambda j: pl.BlockSpec((blk, W), lambda i, c: (rb0 + i * nc + c, j))
    full = lambda a: pl.BlockSpec(a.shape, lambda i, c: (0,) * a.ndim)
    st = pl.BlockSpec((seqs, H, N, N), lambda i, c: (i, 0, 0, 0))
    consts = [prm['mu'], prm['vec'], prm['w2'], prm['a2'], prm['g2']]
    return pl.pallas_call(
        functools.partial(_rwkv_kernel, C=chunk, H=H, N=N, BB=seqs, t_valid=t_valid),
        grid=(B // seqs, nc),
        in_specs=[col(0), col(1), col(2),
                  pl.BlockSpec((blk, SW), lambda i, c: (rb0 + i * nc + c, 7 * W // SW)),
                  pl.BlockSpec((seqs, 1, 3 * W + SW), lambda i, c: (i, 0, 0))]
                 + [full(a) for a in consts] + [st],
        out_specs=[pl.BlockSpec((blk, W), lambda i, c: (i * nc + c, 0)), st],
        out_shape=[jax.ShapeDtypeStruct((B * T, W), BF16), jax.ShapeDtypeStruct((B, H, N, N), F32)],
        scratch_shapes=[pltpu.VMEM((seqs, H, N, N), F32), pltpu.VMEM((seqs, 1, 3 * W + SW), F32),
                        pltpu.VMEM((blk, W), F32)],
        compiler_params=_cparams("parallel", "arbitrary"),
        name="rwkv7_mixer",
    )(P, P, P, P, prev, *consts, s0)


def _gdn_kernel(q_ref, k_ref, v_ref, z_ref, sm_ref, cprev_ref, cw_ref, gate_ref, nw_ref, s0_ref, o_ref, st_ref,
                s_scr, tail_scr, *, C, H, N, BB, t_valid, gate_lane):
    c = pl.program_id(1)
    W = H * N
    R = BB * C

    @pl.when(c == 0)
    def _():
        s_scr[...] = s0_ref[...]
        tail_scr[...] = cprev_ref[...]

    step = _step_in_seq(R, C)
    cw = cw_ref[...]
    nw = nw_ref[...]
    tails = [tail_scr[i] for i in range(BB)]

    def conv_silu(x, lo):
        hi = lo + W
        acc = x * cw[GD_CONV - 1:GD_CONV, lo:hi]
        for j in range(1, GD_CONV):
            acc = acc + (_delayed_rows(x, [t[:, lo:hi] for t in tails], j, BB, C)
                         * cw[GD_CONV - 1 - j:GD_CONV - j, lo:hi])
        return acc * jax.nn.sigmoid(acc)

    xq, xk, xv = q_ref[...], k_ref[...], v_ref[...]
    q_all, k_all, v_all = conv_silu(xq, 0), conv_silu(xk, W), conv_silu(xv, 2 * W)
    for x, lo in ((xq, 0), (xk, W), (xv, 2 * W)):
        for i in range(BB):
            tail_scr[i, :, lo:lo + W] = x[(i + 1) * C - SUBLANES:(i + 1) * C]
    z_all = z_ref[...]
    gates = sm_ref[:, gate_lane:gate_lane + LANES]
    beta = jax.nn.sigmoid(gates[:, :H])
    g = -jnp.exp(gate_ref[0:1, :]) * jax.nn.softplus(gates[:, H:2 * H] + gate_ref[1:2, :])
    if t_valid < C:
        valid = step < t_valid
        beta, g, q_all, k_all, v_all = (jnp.where(valid, t, 0.0) for t in (beta, g, q_all, k_all, v_all))
    tri = _seq_tri(R, C)
    g_terms = _three_bf16_terms(g)
    Gc = sum(jnp.dot(tri, t, preferred_element_type=F32) for t in g_terms)
    Gr = sum(lax.dot_general(t, tri, _TT, preferred_element_type=F32) for t in g_terms)

    row2 = lax.broadcasted_iota(jnp.int32, (2 * C, C), 0)
    col2 = lax.broadcasted_iota(jnp.int32, (2 * C, C), 1)
    keep = jnp.where(row2 >= C, row2 - C, row2 - 1) >= col2

    l2 = lambda t: t * lax.rsqrt(jnp.sum(t * t, -1, keepdims=True) + L2_EPS)
    sl = lambda h: slice(h * N, (h + 1) * N)
    qn = [l2(q_all[:, sl(h)]) * (N ** -0.5) for h in range(H)]
    kn = [l2(k_all[:, sl(h)]) for h in range(H)]

    ch = [(i, h) for i in range(BB) for h in range(H)]
    rs = lambda i: slice(i * C, (i + 1) * C)
    S = [s_scr[i, h] for i, h in ch]
    gc = [Gc[rs(i), h:h + 1] for i, h in ch]
    g_last = [Gc[(i + 1) * C - 1:(i + 1) * C, h:h + 1] for i, h in ch]
    gc2 = [jnp.concatenate([t, t], axis=0) for t in gc]
    decay = [jnp.where(keep, jnp.exp(jnp.where(keep, gc2[j] - Gr[h:h + 1, rs(i)], 0.0)), 0.0)
             for j, (i, h) in enumerate(ch)]
    bh = [beta[rs(i), h:h + 1] for i, h in ch]
    kj = [kn[h][rs(i)] for i, h in ch]
    KQ = [jnp.concatenate([kj[j] * bh[j], qn[h][rs(i)]], axis=0) for j, (i, h) in enumerate(ch)]
    M = [_mm(KQ[j], kj[j], _NT) * decay[j] for j in range(len(ch))]
    SR = [_mm(KQ[j] * jnp.exp(gc2[j]), S[j], _NN) for j in range(len(ch))]
    T = _unit_lower_inverse([-m[:C] for m in M], C)
    v_new = [_mm(T[j], v_all[rs(i), sl(h)] * bh[j] - SR[j][:C], _NN) for j, (i, h) in enumerate(ch)]
    O = [SR[j][C:] + _mm(M[j][C:], v_new[j], _NN) for j in range(len(ch))]
    S = [S[j] * jnp.exp(g_last[j]) + _mm(kj[j] * jnp.exp(g_last[j] - gc[j]), v_new[j], _TN)
         for j in range(len(ch))]
    for j, (i, h) in enumerate(ch):
        zh = z_all[rs(i), sl(h)]
        o = O[j] * lax.rsqrt(jnp.mean(O[j] * O[j], -1, keepdims=True) + GD_NORM_EPS) * nw
        o_ref[rs(i), sl(h)] = (o * (zh * jax.nn.sigmoid(zh))).astype(o_ref.dtype)
        s_scr[i, h] = S[j]

    @pl.when(c == pl.num_programs(1) - 1)
    def _():
        st_ref[...] = s_scr[...]


def gdn_mixer(P, row0, conv_prev, s0, prm, *, chunk, seqs, t_valid):
    B, H, N, _ = s0.shape
    W = H * N
    SW = prm['SW']
    T = prm['T']
    nc = T // chunk
    blk = seqs * chunk
    assert T % chunk == 0 and B % seqs == 0 and row0 % blk == 0 and chunk >= SUBLANES and 2 * H <= LANES
    assert N == LANES and chunk & (chunk - 1) == 0
    rb0 = row0 // blk
    col = lambda j: pl.BlockSpec((blk, W), lambda i, c: (rb0 + i * nc + c, j))
    full = lambda a: pl.BlockSpec(a.shape, lambda i, c: (0,) * a.ndim)
    st = pl.BlockSpec((seqs, H, N, N), lambda i, c: (i, 0, 0, 0))
    consts = [prm['conv_w'], prm['gate'], prm['norm_w']]
    return pl.pallas_call(
        functools.partial(_gdn_kernel, C=chunk, H=H, N=N, BB=seqs, t_valid=t_valid, gate_lane=SW - LANES),
        grid=(B // seqs, nc),
        in_specs=[col(3), col(4), col(5), col(6),
                  pl.BlockSpec((blk, SW), lambda i, c: (rb0 + i * nc + c, 7 * W // SW)),
                  pl.BlockSpec((seqs, SUBLANES, 3 * W), lambda i, c: (i, 0, 0))]
                 + [full(a) for a in consts] + [st],
        out_specs=[pl.BlockSpec((blk, W), lambda i, c: (i * nc + c, 0)), st],
        out_shape=[jax.ShapeDtypeStruct((B * T, W), BF16), jax.ShapeDtypeStruct((B, H, N, N), F32)],
        scratch_shapes=[pltpu.VMEM((seqs, H, N, N), F32), pltpu.VMEM((seqs, SUBLANES, 3 * W), F32)],
        compiler_params=_cparams("parallel", "arbitrary"),
        name="gdn_mixer",
    )(P, P, P, P, P, conv_prev, *consts, s0)


def _xattn_kernel(q_ref, k_ref, v_ref, o_ref, *, H, Dh, BB, tq):
    scale = Dh ** -0.5
    pairs = [(b, h) for b in range(BB) for h in range(H)]
    sl = [slice(h * Dh, (h + 1) * Dh) for h in range(H)]
    k = [k_ref[b].astype(BF16) for b in range(BB)]
    v = [v_ref[b].astype(BF16) for b in range(BB)]
    q = [q_ref[b * tq:(b + 1) * tq, :] for b in range(BB)]
    s = [lax.dot_general(q[b][:, sl[h]], k[b][:, sl[h]], _NT, preferred_element_type=F32) * scale
         for b, h in pairs]
    e = [jnp.exp(t - jnp.max(t, -1, keepdims=True)) for t in s]
    pr = [(t / jnp.sum(t, -1, keepdims=True)).astype(BF16) for t in e]
    o = [jnp.dot(pr[j], v[b][:, sl[h]], preferred_element_type=F32) for j, (b, h) in enumerate(pairs)]
    for j, (b, h) in enumerate(pairs):
        o_ref[b * tq:(b + 1) * tq, sl[h]] = o[j].astype(o_ref.dtype)


def cross_attention(q, row0, mk, mv, *, heads, tq, seqs, q_tiles):
    B, Mm, Wd = mk.shape
    blk = seqs * tq
    assert B % seqs == 0 and row0 % blk == 0 and (seqs == 1 or q_tiles == 1)
    rb0 = row0 // blk
    kv = pl.BlockSpec((seqs, Mm, Wd), lambda i, t: (i, 0, 0))
    return pl.pallas_call(
        functools.partial(_xattn_kernel, H=heads, Dh=Wd // heads, BB=seqs, tq=tq),
        grid=(B // seqs, q_tiles),
        in_specs=[pl.BlockSpec((blk, Wd), lambda i, t: (rb0 + i * q_tiles + t, 0)), kv, kv],
        out_specs=pl.BlockSpec((blk, Wd), lambda i, t: (i * q_tiles + t, 0)),
        out_shape=jax.ShapeDtypeStruct((B * q_tiles * tq, Wd), BF16),
        compiler_params=_cparams("parallel", "arbitrary"),
        name="cross_attention",
    )(q, mk, mv)


def _lane_pad(a, width):
    return jnp.pad(a, [(0, 0)] * (a.ndim - 1) + [(0, width - a.shape[-1])])


def _row_pad(a, rows):
    return jnp.pad(a, [(0, rows - a.shape[0])] + [(0, 0)] * (a.ndim - 1))


def kernel(x_prompt, mem_prompt, x_sample, state_rwkv, state_shift, state_gdn, state_conv, cache_mem_k, cache_mem_v, w_in, mu_shift, rw_w0, rw_w2, rw_a0, rw_a2, rw_g2, rw_kk, rw_ka, rw_rk, rw_lnx_w, rw_lnx_b, gd_conv_w, gd_a_log, gd_dt_bias, gd_norm_w, w_out, ln1_g, ln1_b, xa_wq, xa_wk, xa_wv, xa_wo, ln2_g, ln2_b, ff_wg, ff_wu, ff_wd, ln3_g, ln3_b):
    depth = w_in.shape[0]
    Bp, Tp, D = x_prompt.shape
    Bs, Ts, _ = x_sample.shape
    Hr, Hg = state_rwkv.shape[2], state_gdn.shape[2]
    W = Hr * RW_HEAD
    assert W == Hg * GD_HEAD and Ts <= SAMPLE_STEPS and Ts >= GD_CONV - 1 and Tp % PROMPT_CHUNK == 0
    n_w, n_a, n_g = rw_w2.shape[1], rw_a2.shape[1], rw_g2.shape[1]
    sw, sa, sg = (_round_up(n, LANES) for n in (n_w, n_a, n_g))
    SW = sw + sa + sg + LANES
    n_mem, xa_heads = cache_mem_k.shape[2], cache_mem_k.shape[3]
    xa_width = xa_heads * cache_mem_k.shape[4]
    alpha = (2 * depth) ** 0.25
    Mp, Ms = Bp * Tp, Bs * SAMPLE_STEPS
    M = Mp + Ms
    tm = _pick_tile(M, (512, 384, 256, 128))
    tf = _pick_tile(ff_wg.shape[-1], (512, 256, 128))
    c_w, c_a, c_g = 3 * W, 3 * W + n_w, 3 * W + n_w + n_a
    c_gd = c_g + n_g

    def to_layout(t):
        parts = [t[..., :c_w], t[..., c_gd:c_gd + 4 * W],
                 _lane_pad(t[..., c_w:c_a], sw), _lane_pad(t[..., c_a:c_g], sa), _lane_pad(t[..., c_g:c_gd], sg),
                 _lane_pad(t[..., c_gd + 4 * W:], LANES)]
        return jnp.concatenate(parts, axis=-1)

    def rwkv_layout(t):
        parts = [t[..., :c_w], _lane_pad(t[..., c_w:c_a], sw), _lane_pad(t[..., c_a:c_g], sa),
                 _lane_pad(t[..., c_g:c_gd], sg + LANES)]
        return jnp.concatenate(parts, axis=-1)

    def rwkv_cols(t):
        s0 = 7 * W
        return jnp.concatenate([t[..., :3 * W], t[..., s0:s0 + n_w], t[..., s0 + sw:s0 + sw + n_a],
                                t[..., s0 + sw + sa:s0 + sw + sa + n_g]], axis=-1)

    w_proj = to_layout(w_in).astype(BF16)
    w_kv = jnp.concatenate([xa_wk, xa_wv], axis=2).astype(BF16)
    w_o, w_q, w_xo = w_out.astype(BF16), xa_wq.astype(BF16), xa_wo.astype(BF16)
    w_g, w_u, w_d = ff_wg.astype(BF16), ff_wu.astype(BF16), ff_wd.astype(BF16)

    x_s = jnp.pad(x_sample, ((0, 0), (0, SAMPLE_STEPS - Ts), (0, 0)))
    x = jnp.concatenate([x_prompt.reshape(Mp, D), x_s.reshape(Ms, D)], axis=0)
    xb = x.astype(BF16)
    mem = mem_prompt.reshape(Bp * n_mem, D).astype(BF16)
    zeros = lambda *s: jnp.zeros(s, F32)
    tq = _pick_tile(Tp, (512, 256, 128))

    outs = {n: [] for n in ('p_rw', 'p_sh', 'p_gd', 'p_cv', 'p_mk', 'p_mv', 's_rw', 's_sh', 's_gd', 's_cv')}
    for l in range(depth):
        rw = {'T': None, 'mu': rwkv_layout(mu_shift[l])[None],
              'vec': jnp.stack([rw_w0[l], rw_a0[l], rw_kk[l], rw_ka[l], rw_rk[l].reshape(W), rw_lnx_w[l],
                                rw_lnx_b[l], jnp.zeros((W,), F32)]),
              'w2': _row_pad(rw_w2[l], sw).astype(BF16), 'a2': _row_pad(rw_a2[l], sa).astype(BF16),
              'g2': _row_pad(rw_g2[l], sg).astype(BF16)}
        gd = {'T': None, 'SW': SW, 'conv_w': gd_conv_w[l], 'gate': jnp.stack([gd_a_log[l], gd_dt_bias[l]]),
              'norm_w': gd_norm_w[l][None]}

        kv = matmul(mem, w_kv, l, tm=_pick_tile(Bp * n_mem, (512, 256, 128)),
                    tn=_pick_tile(2 * xa_width, (512, 256, 128)))
        mk = kv[:, :xa_width].reshape(Bp, n_mem, xa_width)
        mv = kv[:, xa_width:].reshape(Bp, n_mem, xa_width)

        P = matmul(xb, w_proj, l, tm=tm, tn=_pick_tile(7 * W + SW, (512, 256, 128)))

        prev_s = rwkv_layout(state_shift[l])[:, None]
        conv_s = jnp.pad(state_conv[l], ((0, 0), (SUBLANES - (GD_CONV - 1), 0), (0, 0)))
        o_rw_p, rw_p = rwkv7_mixer(P, 0, zeros(Bp, 1, 3 * W + SW), zeros(Bp, Hr, RW_HEAD, RW_HEAD),
                                   dict(rw, T=Tp), chunk=PROMPT_CHUNK, seqs=1, t_valid=PROMPT_CHUNK)
        o_gd_p, gd_p = gdn_mixer(P, 0, zeros(Bp, SUBLANES, 3 * W), zeros(Bp, Hg, GD_HEAD, GD_HEAD),
                                 dict(gd, T=Tp), chunk=PROMPT_CHUNK, seqs=1, t_valid=PROMPT_CHUNK)
        o_rw_s, rw_s = rwkv7_mixer(P, Mp, prev_s, state_rwkv[l], dict(rw, T=SAMPLE_STEPS),
                                   chunk=SAMPLE_STEPS, seqs=SAMPLE_SEQS, t_valid=Ts)
        o_gd_s, gd_s = gdn_mixer(P, Mp, conv_s, state_gdn[l], dict(gd, T=SAMPLE_STEPS),
                                 chunk=SAMPLE_STEPS, seqs=SAMPLE_SEQS, t_valid=Ts)

        x, q = mix_out_ln_q(jnp.concatenate([o_rw_p, o_rw_s], axis=0), jnp.concatenate([o_gd_p, o_gd_s], axis=0),
                            w_o, w_q, l, x, ln1_g[l], ln1_b[l], alpha=alpha, tm=tm)

        att_p = cross_attention(q, 0, mk, mv, heads=xa_heads, tq=tq, seqs=1, q_tiles=Tp // tq)
        att_s = cross_attention(q, Mp, cache_mem_k[l].reshape(Bs, n_mem, xa_width),
                                cache_mem_v[l].reshape(Bs, n_mem, xa_width), heads=xa_heads, tq=SAMPLE_STEPS,
                                seqs=SAMPLE_SEQS, q_tiles=1)
        x, xb = matmul_ln(jnp.concatenate([att_p, att_s], axis=0), w_xo, l, x, ln2_g[l], ln2_b[l],
                          alpha=alpha, tm=tm)
        x, xb = ffn_ln(xb, w_g, w_u, w_d, l, x, ln3_g[l], ln3_b[l], alpha=alpha, tm=tm, tf=tf)

        last_p = lambda back: P[Tp - back:Mp:Tp]
        last_s = lambda back: P[Mp + Ts - back:M:SAMPLE_STEPS]
        conv_rows = lambda last: jnp.stack([last(b)[:, 3 * W:6 * W] for b in range(GD_CONV - 1, 0, -1)], axis=1)
        outs['p_rw'].append(rw_p); outs['p_gd'].append(gd_p); outs['s_rw'].append(rw_s); outs['s_gd'].append(gd_s)
        outs['p_sh'].append(rwkv_cols(last_p(1))); outs['s_sh'].append(rwkv_cols(last_s(1)))
        outs['p_cv'].append(conv_rows(last_p)); outs['s_cv'].append(conv_rows(last_s))
        outs['p_mk'].append(mk.reshape(Bp, n_mem, xa_heads, -1)); outs['p_mv'].append(mv.reshape(Bp, n_mem, xa_heads, -1))

    st = {n: jnp.stack(v) for n, v in outs.items()}
    return (x[:Mp].reshape(Bp, Tp, D), x[Mp:].reshape(Bs, SAMPLE_STEPS, D)[:, :Ts], st['p_rw'], st['p_sh'],
            st['p_gd'], st['p_cv'], st['p_mk'], st['p_mv'], st['s_rw'], st['s_sh'], st['s_gd'], st['s_cv'])
```

```python
import functools

import jax
import jax.numpy as jnp
from jax import lax
from jax.experimental import pallas as pl
from jax.experimental.pallas import tpu as pltpu

F32 = jnp.float32
BF16 = jnp.bfloat16

LANES = 128
SUBLANES = 8
RW_HEAD = 64
GD_HEAD = 128
GD_CONV = 4
LN_EPS = 1e-5
RW_GN_EPS = 64e-5
GD_NORM_EPS = 1e-6
L2_EPS = 1e-12

PROMPT_CHUNK = 64
SAMPLE_STEPS = SUBLANES
SAMPLE_SEQS = 8

VMEM_LIMIT_BYTES = 56 * 1024 * 1024

_NN = (((1,), (0,)), ((), ()))
_NT = (((1,), (1,)), ((), ()))
_TN = (((0,), (0,)), ((), ()))
_TT = (((0,), (1,)), ((), ()))


def _cparams(*sem):
    return pltpu.CompilerParams(dimension_semantics=sem, vmem_limit_bytes=VMEM_LIMIT_BYTES)


def _round_up(n, m):
    return -(-n // m) * m


def _pick_tile(n, candidates):
    for c in candidates:
        if n % c == 0:
            return c
    raise ValueError(f"no tile in {candidates} divides {n}")


def _mm_kernel(x_ref, w_ref, o_ref):
    o_ref[...] = jnp.dot(x_ref[...], w_ref[...], preferred_element_type=F32).astype(o_ref.dtype)


def matmul(x, w, layer, *, tm, tn, out_dtype=F32):
    M, K = x.shape
    N = w.shape[2]
    assert M % tm == 0 and N % tn == 0
    return pl.pallas_call(
        _mm_kernel,
        grid=(M // tm, N // tn),
        in_specs=[pl.BlockSpec((tm, K), lambda i, j: (i, 0)),
                  pl.BlockSpec((None, K, tn), lambda i, j: (layer, 0, j))],
        out_specs=pl.BlockSpec((tm, tn), lambda i, j: (i, j)),
        out_shape=jax.ShapeDtypeStruct((M, N), out_dtype),
        compiler_params=_cparams("parallel", "arbitrary"),
        name="matmul",
    )(x, w)


def _layer_norm_rows(y, g, b):
    mu = jnp.mean(y, -1, keepdims=True)
    yc = y - mu
    var = jnp.mean(yc * yc, -1, keepdims=True)
    return yc * lax.rsqrt(var + LN_EPS) * g + b


def _mix_out_kernel(a_ref, b_ref, wa_ref, wb_ref, res_ref, g_ref, beta_ref, wq_ref, x_ref, q_ref, *, alpha):
    h = jnp.dot(a_ref[...], wa_ref[...], preferred_element_type=F32)
    h += jnp.dot(b_ref[...], wb_ref[...], preferred_element_type=F32)
    x = _layer_norm_rows(alpha * res_ref[...] + h, g_ref[...], beta_ref[...])
    x_ref[...] = x
    q_ref[...] = jnp.dot(x.astype(BF16), wq_ref[...], preferred_element_type=F32).astype(BF16)


def mix_out_ln_q(o_a, o_b, w, wq, layer, res, g, b, *, alpha, tm):
    M, Ka = o_a.shape
    Kb = o_b.shape[1]
    D = w.shape[2]
    Q = wq.shape[2]
    assert M % tm == 0 and Ka == Kb and w.shape[1] == Ka + Kb
    rows = lambda n: pl.BlockSpec((tm, n), lambda i: (i, 0))
    vec = pl.BlockSpec((1, D), lambda i: (0, 0))
    return pl.pallas_call(
        functools.partial(_mix_out_kernel, alpha=alpha),
        grid=(M // tm,),
        in_specs=[rows(Ka), rows(Kb),
                  pl.BlockSpec((None, Ka, D), lambda i: (layer, 0, 0)),
                  pl.BlockSpec((None, Kb, D), lambda i: (layer, 1, 0)),
                  rows(D), vec, vec,
                  pl.BlockSpec((None, D, Q), lambda i: (layer, 0, 0))],
        out_specs=[rows(D), rows(Q)],
        out_shape=[jax.ShapeDtypeStruct((M, D), F32), jax.ShapeDtypeStruct((M, Q), BF16)],
        compiler_params=_cparams("parallel"),
        name="mix_out_ln_q",
    )(o_a, o_b, w, w, res, g.reshape(1, D), b.reshape(1, D), wq)


def _mm_ln_kernel(x_ref, w_ref, res_ref, g_ref, b_ref, o_ref, ob_ref, *, alpha):
    h = jnp.dot(x_ref[...], w_ref[...], preferred_element_type=F32)
    y = _layer_norm_rows(alpha * res_ref[...] + h, g_ref[...], b_ref[...])
    o_ref[...] = y
    ob_ref[...] = y.astype(BF16)


def matmul_ln(x, w, layer, res, g, b, *, alpha, tm):
    M, K = x.shape
    D = w.shape[2]
    assert M % tm == 0
    return pl.pallas_call(
        functools.partial(_mm_ln_kernel, alpha=alpha),
        grid=(M // tm,),
        in_specs=[pl.BlockSpec((tm, K), lambda i: (i, 0)),
                  pl.BlockSpec((None, K, D), lambda i: (layer, 0, 0)),
                  pl.BlockSpec((tm, D), lambda i: (i, 0)),
                  pl.BlockSpec((1, D), lambda i: (0, 0)),
                  pl.BlockSpec((1, D), lambda i: (0, 0))],
        out_specs=[pl.BlockSpec((tm, D), lambda i: (i, 0))] * 2,
        out_shape=[jax.ShapeDtypeStruct((M, D), F32), jax.ShapeDtypeStruct((M, D), BF16)],
        compiler_params=_cparams("parallel"),
        name="matmul_ln",
    )(x, w, res, g.reshape(1, D), b.reshape(1, D))


def _ffn_ln_kernel(x_ref, wg_ref, wu_ref, wd_ref, res_ref, g_ref, b_ref, o_ref, ob_ref, acc_ref, *, alpha):
    f = pl.program_id(1)

    @pl.when(f == 0)
    def _():
        acc_ref[...] = jnp.zeros_like(acc_ref)

    x = x_ref[...]
    gate = jnp.dot(x, wg_ref[...], preferred_element_type=F32)
    up = jnp.dot(x, wu_ref[...], preferred_element_type=F32)
    hidden = (gate * jax.nn.sigmoid(gate) * up).astype(BF16)
    acc_ref[...] += jnp.dot(hidden, wd_ref[...], preferred_element_type=F32)

    @pl.when(f == pl.num_programs(1) - 1)
    def _():
        y = _layer_norm_rows(alpha * res_ref[...] + acc_ref[...], g_ref[...], b_ref[...])
        o_ref[...] = y
        ob_ref[...] = y.astype(BF16)


def ffn_ln(x, wg, wu, wd, layer, res, g, b, *, alpha, tm, tf):
    M, D = x.shape
    Fd = wg.shape[2]
    assert M % tm == 0 and Fd % tf == 0
    return pl.pallas_call(
        functools.partial(_ffn_ln_kernel, alpha=alpha),
        grid=(M // tm, Fd // tf),
        in_specs=[pl.BlockSpec((tm, D), lambda i, f: (i, 0)),
                  pl.BlockSpec((None, D, tf), lambda i, f: (layer, 0, f)),
                  pl.BlockSpec((None, D, tf), lambda i, f: (layer, 0, f)),
                  pl.BlockSpec((None, tf, D), lambda i, f: (layer, f, 0)),
                  pl.BlockSpec((tm, D), lambda i, f: (i, 0)),
                  pl.BlockSpec((1, D), lambda i, f: (0, 0)),
                  pl.BlockSpec((1, D), lambda i, f: (0, 0))],
        out_specs=[pl.BlockSpec((tm, D), lambda i, f: (i, 0))] * 2,
        out_shape=[jax.ShapeDtypeStruct((M, D), F32), jax.ShapeDtypeStruct((M, D), BF16)],
        scratch_shapes=[pltpu.VMEM((tm, D), F32)],
        compiler_params=_cparams("parallel", "arbitrary"),
        name="ffn_ln",
    )(x, wg, wu, wd, res, g.reshape(1, D), b.reshape(1, D))


def _mm(a, b, dims):
    return lax.dot_general(a.astype(BF16), b.astype(BF16), dims, preferred_element_type=F32)


def _three_bf16_terms(a):
    a1 = a.astype(BF16)
    r1 = a - a1.astype(F32)
    a2 = r1.astype(BF16)
    return a1, a2, (r1 - a2.astype(F32)).astype(BF16)


def _unit_lower_inverse(xs, c):
    row = lax.broadcasted_iota(jnp.int32, (c, c), 0)
    col = lax.broadcasted_iota(jnp.int32, (c, c), 1)
    eye = jnp.where(row == col, 1.0, 0.0).astype(F32)
    ts = [eye + x for x in xs]
    ps = list(xs)
    span = 2
    while span < c:
        ps = [_mm(p, p, _NN) for p in ps]
        ts = [t + _mm(t, p, _NN) for t, p in zip(ts, ps)]
        span *= 2
    return ts


def _step_in_seq(R, C):
    return lax.broadcasted_iota(jnp.int32, (R, 1), 0) & (C - 1)


def _seq_tri(R, C):
    row = lax.broadcasted_iota(jnp.int32, (R, R), 0)
    col = lax.broadcasted_iota(jnp.int32, (R, R), 1)
    return jnp.where(row >= col, jnp.where((row & -C) == (col & -C), 1.0, 0.0), 0.0).astype(BF16)


def _per_seq_rows(rows, C):
    rows = [jnp.broadcast_to(t, (C, t.shape[-1])) for t in rows]
    return rows[0] if len(rows) == 1 else jnp.concatenate(rows, axis=0)


def _delayed_rows(x, tails, j, BB, C):
    xj = pltpu.roll(x, j, 0)
    row = lax.broadcasted_iota(jnp.int32, (SUBLANES, 1), 0)
    pieces = []
    for i in range(BB):
        pieces.append(jnp.where(row < j, pltpu.roll(tails[i], j, 0), xj[i * C:i * C + SUBLANES]))
        if C > SUBLANES:
            pieces.append(xj[i * C + SUBLANES:(i + 1) * C])
    return pieces[0] if len(pieces) == 1 else jnp.concatenate(pieces, axis=0)


def _half_sums(x, N):
    low = lax.broadcasted_iota(jnp.int32, (1, LANES), 1) < N
    out = []
    for j in range(x.shape[1] // LANES):
        xv = x[:, j * LANES:(j + 1) * LANES]
        lo = jnp.sum(jnp.where(low, xv, 0.0), -1, keepdims=True)
        out += [lo, jnp.sum(jnp.where(low, 0.0, xv), -1, keepdims=True)]
    return out


def _half_bcast(cols, N):
    low = lax.broadcasted_iota(jnp.int32, (1, LANES), 1) < N
    return jnp.concatenate([jnp.where(low, cols[2 * j], cols[2 * j + 1]) for j in range(len(cols) // 2)], axis=1)


def _rwkv_kernel(r_ref, k_ref, v_ref, sm_ref, prev_ref, mu_ref, vec_ref, w2_ref, a2_ref, g2_ref,
                 s0_ref, st_buf_ref, o_ref, st_ref, last_ref, s_scr, tail_scr, o_scr, *, C, H, N, BB, t_valid):
    c = pl.program_id(1)
    W = H * N
    R = BB * C
    sw, sa, sg = w2_ref.shape[0], a2_ref.shape[0], g2_ref.shape[0]

    @pl.when(c == 0)
    def _():
        s_scr[...] = s0_ref[...]
        tail_scr[...] = prev_ref[...]

    step = _step_in_seq(R, C)
    first = step == 0
    vec = vec_ref[...]
    w0, a0, k_k, k_a, r_k, ln_w, ln_b = (vec[j:j + 1] for j in range(7))
    mu = mu_ref[...]
    tail = _per_seq_rows([tail_scr[i] for i in range(BB)], C)

    def shifted(x, lo):
        hi = lo + x.shape[1]
        prev = jnp.where(first, tail[:, lo:hi], pltpu.roll(x, 1, 0))
        return x + (prev - x) * mu[:, lo:hi]

    xr, xk, xv, xs = r_ref[...], k_ref[...], v_ref[...], sm_ref[...]
    r, k, v, ms = shifted(xr, 0), shifted(xk, W), shifted(xv, 2 * W), shifted(xs, 3 * W)
    for x, lo in ((xr, 0), (xk, W), (xv, 2 * W), (xs, 3 * W)):
        for i in range(BB):
            tail_scr[i, :, lo:lo + x.shape[1]] = x[i * C + t_valid - 1:i * C + t_valid]

    lora_w = jnp.dot(jnp.tanh(ms[:, :sw]).astype(BF16), w2_ref[...], preferred_element_type=F32)
    lora_a = jnp.dot(ms[:, sw:sw + sa].astype(BF16), a2_ref[...], preferred_element_type=F32)
    gate = jnp.dot(jax.nn.sigmoid(ms[:, sw + sa:sw + sa + sg]).astype(BF16), g2_ref[...],
                   preferred_element_type=F32)
    lw = -jnp.exp(-jax.nn.softplus(-(w0 + lora_w)) - 0.5)
    a = jax.nn.sigmoid(a0 + lora_a)
    kk = k * k_k
    kk = kk * _half_bcast([lax.rsqrt(s + L2_EPS) for s in _half_sums(kk * kk, N)], N)
    k = k * (1.0 + (a - 1.0) * k_a)
    if t_valid < C:
        valid = step < t_valid
        r, k, v, kk, lw = (jnp.where(valid, t, 0.0) for t in (r, k, v, kk, lw))

    tri = _seq_tri(R, C)
    G = sum(jnp.dot(tri, t, preferred_element_type=F32) for t in _three_bf16_terms(lw))
    mid = C // 2 - 1
    g_mid = [G[i * C + mid:i * C + mid + 1] for i in range(BB)]
    g_last = [G[(i + 1) * C - 1:(i + 1) * C] for i in range(BB)]
    Gm = G - _per_seq_rows(g_mid, C)
    e_out = jnp.exp(-Gm)
    e_tail = jnp.exp(_per_seq_rows(g_last, C) - G)
    b_in = kk * a
    A_, R_ = -kk * jnp.exp(Gm - lw), r * jnp.exp(Gm)
    B_, K_ = b_in * e_out, k * e_out
    Bt, Kt = b_in * e_tail, k * e_tail
    e_mid, e_last = [jnp.exp(t) for t in g_mid], [jnp.exp(t) for t in g_last]

    row2 = lax.broadcasted_iota(jnp.int32, (2 * C, 2 * C), 0)
    col2 = lax.broadcasted_iota(jnp.int32, (2 * C, 2 * C), 1)
    rowc = jnp.where(row2 >= C, row2 - C, row2)
    colc = jnp.where(col2 >= C, col2 - C, col2)
    keep = jnp.where(row2 >= C, rowc, rowc - 1) >= colc

    ch = [(i, h) for i in range(BB) for h in range(H)]
    rs = lambda i: slice(i * C, (i + 1) * C)
    sl = lambda h: slice(h * N, (h + 1) * N)
    pair = lambda x, y, i, h: jnp.concatenate([x[rs(i), sl(h)], y[rs(i), sl(h)]], axis=0)
    S = [s_scr[i, h] for i, h in ch]
    AR = [pair(A_, R_, i, h) for i, h in ch]
    M = [jnp.where(keep, _mm(AR[j], pair(B_, K_, i, h), _NT), 0.0)
         for j, (i, h) in enumerate(ch)]
    SR = [_mm(AR[j], S[j] * e_mid[i][:, sl(h)], _NT) for j, (i, h) in enumerate(ch)]
    AV = [_mm(M[j][:C, C:], v[rs(i), sl(h)], _NN) for j, (i, h) in enumerate(ch)]
    T = _unit_lower_inverse([m[:C, :C] for m in M], C)
    U = [_mm(T[j], SR[j][:C] + AV[j], _NN) for j in range(len(ch))]
    UV = [jnp.concatenate([U[j], v[rs(i), sl(h)]], axis=0) for j, (i, h) in enumerate(ch)]
    O = [SR[j][C:] + _mm(M[j][C:], UV[j], _NN) for j in range(len(ch))]
    S = [S[j] * e_last[i][:, sl(h)] + _mm(UV[j], pair(Bt, Kt, i, h), _TN) for j, (i, h) in enumerate(ch)]
    for j, (i, h) in enumerate(ch):
        o_scr[rs(i), sl(h)] = O[j]
        s_scr[i, h] = S[j]

    o = o_scr[...]
    o = o - _half_bcast([s * (1.0 / N) for s in _half_sums(o, N)], N)
    o = o * _half_bcast([lax.rsqrt(s * (1.0 / N) + RW_GN_EPS) for s in _half_sums(o * o, N)], N) * ln_w + ln_b
    bonus = _half_bcast(_half_sums(r * k * r_k, N), N) * v
    o_ref[...] = ((o + bonus) * gate).astype(o_ref.dtype)

    @pl.when(c == pl.num_programs(1) - 1)
    def _():
        st_ref[...] = s_scr[...]
        last_ref[...] = tail_scr[...]


def _state_specs(states, s0_layer, layer, seqs):
    blk = (None, seqs) + states.shape[2:]
    return (pl.BlockSpec(blk, lambda i, c: (s0_layer, i, 0, 0, 0)),
            pl.BlockSpec(blk, lambda i, c: (layer, i, 0, 0, 0)))


def rwkv7_mixer(P, row0, prev, s0, s0_layer, st_buf, layer, prm, *, chunk, seqs, t_valid):
    _, B, H, N, _ = s0.shape
    W = H * N
    SW = prm['mu'].shape[1] - 3 * W
    T = prm['T']
    nc = T // chunk
    blk = seqs * chunk
    assert T % chunk == 0 and B % seqs == 0 and row0 % blk == 0 and (7 * W) % SW == 0
    assert 2 * N == LANES and chunk & (chunk - 1) == 0 and st_buf.shape[1:] == s0.shape[1:]
    rb0 = row0 // blk
    col = lambda j: pl.BlockSpec((blk, W), lambda i, c: (rb0 + i * nc + c, j))
    full = lambda a: pl.BlockSpec(a.shape, lambda i, c: (0,) * a.ndim)
    st_in, st_out = _state_specs(s0, s0_layer, layer, seqs)
    row = pl.BlockSpec((seqs, 1, 3 * W + SW), lambda i, c: (i, 0, 0))
    consts = [prm['mu'], prm['vec'], prm['w2'], prm['a2'], prm['g2']]
    return pl.pallas_call(
        functools.partial(_rwkv_kernel, C=chunk, H=H, N=N, BB=seqs, t_valid=t_valid),
        grid=(B // seqs, nc),
        in_specs=[col(0), col(1), col(2),
                  pl.BlockSpec((blk, SW), lambda i, c: (rb0 + i * nc + c, 7 * W // SW)), row]
                 + [full(a) for a in consts] + [st_in, pl.BlockSpec(memory_space=pl.ANY)],
        out_specs=[pl.BlockSpec((blk, W), lambda i, c: (i * nc + c, 0)), st_out, row],
        out_shape=[jax.ShapeDtypeStruct((B * T, W), BF16), jax.ShapeDtypeStruct(st_buf.shape, F32),
                   jax.ShapeDtypeStruct((B, 1, 3 * W + SW), F32)],
        scratch_shapes=[pltpu.VMEM((seqs, H, N, N), F32), pltpu.VMEM((seqs, 1, 3 * W + SW), F32),
                        pltpu.VMEM((blk, W), F32)],
        input_output_aliases={5 + len(consts) + 1: 1},
        compiler_params=_cparams("parallel", "arbitrary"),
        name="rwkv7_mixer",
    )(P, P, P, P, prev, *consts, s0, st_buf)


def _gdn_kernel(q_ref, k_ref, v_ref, z_ref, sm_ref, cprev_ref, cw_ref, gate_ref, nw_ref, s0_ref, st_buf_ref,
                o_ref, st_ref, last_ref, s_scr, tail_scr, *, C, H, N, BB, t_valid, gate_lane):
    c = pl.program_id(1)
    W = H * N
    R = BB * C

    @pl.when(c == 0)
    def _():
        s_scr[...] = s0_ref[...]
        tail_scr[...] = cprev_ref[...]

    step = _step_in_seq(R, C)
    cw = cw_ref[...]
    nw = nw_ref[...]
    tails = [tail_scr[i] for i in range(BB)]

    def conv_silu(x, lo):
        hi = lo + W
        acc = x * cw[GD_CONV - 1:GD_CONV, lo:hi]
        for j in range(1, GD_CONV):
            acc = acc + (_delayed_rows(x, [t[:, lo:hi] for t in tails], j, BB, C)
                         * cw[GD_CONV - 1 - j:GD_CONV - j, lo:hi])
        return acc * jax.nn.sigmoid(acc)

    xq, xk, xv = q_ref[...], k_ref[...], v_ref[...]
    q_all, k_all, v_all = conv_silu(xq, 0), conv_silu(xk, W), conv_silu(xv, 2 * W)
    for x, lo in ((xq, 0), (xk, W), (xv, 2 * W)):
        for i in range(BB):
            tail_scr[i, :, lo:lo + W] = x[(i + 1) * C - SUBLANES:(i + 1) * C]
    z_all = z_ref[...]
    gates = sm_ref[:, gate_lane:gate_lane + LANES]
    beta = jax.nn.sigmoid(gates[:, :H])
    g = -jnp.exp(gate_ref[0:1, :]) * jax.nn.softplus(gates[:, H:2 * H] + gate_ref[1:2, :])
    if t_valid < C:
        valid = step < t_valid
        beta, g, q_all, k_all, v_all = (jnp.where(valid, t, 0.0) for t in (beta, g, q_all, k_all, v_all))
    tri = _seq_tri(R, C)
    g_terms = _three_bf16_terms(g)
    Gc = sum(jnp.dot(tri, t, preferred_element_type=F32) for t in g_terms)
    Gr = sum(lax.dot_general(t, tri, _TT, preferred_element_type=F32) for t in g_terms)

    row2 = lax.broadcasted_iota(jnp.int32, (2 * C, C), 0)
    col2 = lax.broadcasted_iota(jnp.int32, (2 * C, C), 1)
    keep = jnp.where(row2 >= C, row2 - C, row2 - 1) >= col2

    l2 = lambda t: t * lax.rsqrt(jnp.sum(t * t, -1, keepdims=True) + L2_EPS)
    sl = lambda h: slice(h * N, (h + 1) * N)
    qn = [l2(q_all[:, sl(h)]) * (N ** -0.5) for h in range(H)]
    kn = [l2(k_all[:, sl(h)]) for h in range(H)]

    ch = [(i, h) for i in range(BB) for h in range(H)]
    rs = lambda i: slice(i * C, (i + 1) * C)
    S = [s_scr[i, h] for i, h in ch]
    gc = [Gc[rs(i), h:h + 1] for i, h in ch]
    g_last = [Gc[(i + 1) * C - 1:(i + 1) * C, h:h + 1] for i, h in ch]
    gc2 = [jnp.concatenate([t, t], axis=0) for t in gc]
    decay = [jnp.where(keep, jnp.exp(jnp.where(keep, gc2[j] - Gr[h:h + 1, rs(i)], 0.0)), 0.0)
             for j, (i, h) in enumerate(ch)]
    bh = [beta[rs(i), h:h + 1] for i, h in ch]
    kj = [kn[h][rs(i)] for i, h in ch]
    KQ = [jnp.concatenate([kj[j] * bh[j], qn[h][rs(i)]], axis=0) for j, (i, h) in enumerate(ch)]
    M = [_mm(KQ[j], kj[j], _NT) * decay[j] for j in range(len(ch))]
    SR = [_mm(KQ[j] * jnp.exp(gc2[j]), S[j], _NN) for j in range(len(ch))]
    T = _unit_lower_inverse([-m[:C] for m in M], C)
    v_new = [_mm(T[j], v_all[rs(i), sl(h)] * bh[j] - SR[j][:C], _NN) for j, (i, h) in enumerate(ch)]
    O = [SR[j][C:] + _mm(M[j][C:], v_new[j], _NN) for j in range(len(ch))]
    S = [S[j] * jnp.exp(g_last[j]) + _mm(kj[j] * jnp.exp(g_last[j] - gc[j]), v_new[j], _TN)
         for j in range(len(ch))]
    for j, (i, h) in enumerate(ch):
        zh = z_all[rs(i), sl(h)]
        o = O[j] * lax.rsqrt(jnp.mean(O[j] * O[j], -1, keepdims=True) + GD_NORM_EPS) * nw
        o_ref[rs(i), sl(h)] = (o * (zh * jax.nn.sigmoid(zh))).astype(o_ref.dtype)
        s_scr[i, h] = S[j]

    @pl.when(c == pl.num_programs(1) - 1)
    def _():
        st_ref[...] = s_scr[...]
        last_ref[...] = tail_scr[...]


def gdn_mixer(P, row0, conv_prev, s0, s0_layer, st_buf, layer, prm, *, chunk, seqs, t_valid):
    _, B, H, N, _ = s0.shape
    W = H * N
    SW = prm['SW']
    T = prm['T']
    nc = T // chunk
    blk = seqs * chunk
    assert T % chunk == 0 and B % seqs == 0 and row0 % blk == 0 and chunk >= SUBLANES and 2 * H <= LANES
    assert N == LANES and chunk & (chunk - 1) == 0 and st_buf.shape[1:] == s0.shape[1:]
    rb0 = row0 // blk
    col = lambda j: pl.BlockSpec((blk, W), lambda i, c: (rb0 + i * nc + c, j))
    full = lambda a: pl.BlockSpec(a.shape, lambda i, c: (0,) * a.ndim)
    st_in, st_out = _state_specs(s0, s0_layer, layer, seqs)
    rows = pl.BlockSpec((seqs, SUBLANES, 3 * W), lambda i, c: (i, 0, 0))
    consts = [prm['conv_w'], prm['gate'], prm['norm_w']]
    return pl.pallas_call(
        functools.partial(_gdn_kernel, C=chunk, H=H, N=N, BB=seqs, t_valid=t_valid, gate_lane=SW - LANES),
        grid=(B // seqs, nc),
        in_specs=[col(3), col(4), col(5), col(6),
                  pl.BlockSpec((blk, SW), lambda i, c: (rb0 + i * nc + c, 7 * W // SW)), rows]
                 + [full(a) for a in consts] + [st_in, pl.BlockSpec(memory_space=pl.ANY)],
        out_specs=[pl.BlockSpec((blk, W), lambda i, c: (i * nc + c, 0)), st_out, rows],
        out_shape=[jax.ShapeDtypeStruct((B * T, W), BF16), jax.ShapeDtypeStruct(st_buf.shape, F32),
                   jax.ShapeDtypeStruct((B, SUBLANES, 3 * W), F32)],
        scratch_shapes=[pltpu.VMEM((seqs, H, N, N), F32), pltpu.VMEM((seqs, SUBLANES, 3 * W), F32)],
        input_output_aliases={6 + len(consts) + 1: 1},
        compiler_params=_cparams("parallel", "arbitrary"),
        name="gdn_mixer",
    )(P, P, P, P, P, conv_prev, *consts, s0, st_buf)


def _xattn_kernel(q_ref, k_ref, v_ref, o_ref, *, H, Dh, BB, tq, head_rows):
    scale = Dh ** -0.5
    pairs = [(b, h) for b in range(BB) for h in range(H)]
    sl = [slice(h * Dh, (h + 1) * Dh) for h in range(H)]

    def head(ref, b, h):
        if head_rows:
            return ref[b, pl.ds(h, ref.shape[1] // H, stride=H), :].astype(BF16)
        return ref[:, sl[h]].astype(BF16)

    q = [q_ref[b * tq:(b + 1) * tq, :] for b in range(BB)]
    s = [lax.dot_general(q[b][:, sl[h]], head(k_ref, b, h), _NT, preferred_element_type=F32) * scale
         for b, h in pairs]
    e = [jnp.exp(t - jnp.max(t, -1, keepdims=True)) for t in s]
    pr = [(t / jnp.sum(t, -1, keepdims=True)).astype(BF16) for t in e]
    o = [jnp.dot(pr[j], head(v_ref, b, h), preferred_element_type=F32) for j, (b, h) in enumerate(pairs)]
    for j, (b, h) in enumerate(pairs):
        o_ref[b * tq:(b + 1) * tq, sl[h]] = o[j].astype(o_ref.dtype)


def cross_attention_prompt(q, kv, *, n_mem, heads, tq, q_tiles):
    Wd = kv.shape[1] // 2
    B = kv.shape[0] // n_mem
    return pl.pallas_call(
        functools.partial(_xattn_kernel, H=heads, Dh=Wd // heads, BB=1, tq=tq, head_rows=False),
        grid=(B, q_tiles),
        in_specs=[pl.BlockSpec((tq, Wd), lambda i, t: (i * q_tiles + t, 0)),
                  pl.BlockSpec((n_mem, Wd), lambda i, t: (i, 0)),
                  pl.BlockSpec((n_mem, Wd), lambda i, t: (i, 1))],
        out_specs=pl.BlockSpec((tq, Wd), lambda i, t: (i * q_tiles + t, 0)),
        out_shape=jax.ShapeDtypeStruct((B * q_tiles * tq, Wd), BF16),
        compiler_params=_cparams("parallel", "arbitrary"),
        name="cross_attention_prompt",
    )(q, kv, kv)


def cross_attention_sample(q, row0, mem_k, mem_v, layer, *, heads, tq, seqs):
    _, B, rows, Dh = mem_k.shape
    Wd = heads * Dh
    blk = seqs * tq
    assert B % seqs == 0 and row0 % blk == 0
    rb0 = row0 // blk
    kv = pl.BlockSpec((None, seqs, rows, Dh), lambda i: (layer, i, 0, 0))
    return pl.pallas_call(
        functools.partial(_xattn_kernel, H=heads, Dh=Dh, BB=seqs, tq=tq, head_rows=True),
        grid=(B // seqs,),
        in_specs=[pl.BlockSpec((blk, Wd), lambda i: (rb0 + i, 0)), kv, kv],
        out_specs=pl.BlockSpec((blk, Wd), lambda i: (i, 0)),
        out_shape=jax.ShapeDtypeStruct((B * tq, Wd), BF16),
        compiler_params=_cparams("parallel"),
        name="cross_attention_sample",
    )(q, mem_k, mem_v)


def _lane_pad(a, width):
    return jnp.pad(a, [(0, 0)] * (a.ndim - 1) + [(0, width - a.shape[-1])])


def _row_pad(a, rows):
    return jnp.pad(a, [(0, rows - a.shape[0])] + [(0, 0)] * (a.ndim - 1))


def kernel(x_prompt, mem_prompt, x_sample, state_rwkv, state_shift, state_gdn, state_conv, cache_mem_k, cache_mem_v, w_in, mu_shift, rw_w0, rw_w2, rw_a0, rw_a2, rw_g2, rw_kk, rw_ka, rw_rk, rw_lnx_w, rw_lnx_b, gd_conv_w, gd_a_log, gd_dt_bias, gd_norm_w, w_out, ln1_g, ln1_b, xa_wq, xa_wk, xa_wv, xa_wo, ln2_g, ln2_b, ff_wg, ff_wu, ff_wd, ln3_g, ln3_b):
    depth = w_in.shape[0]
    Bp, Tp, D = x_prompt.shape
    Bs, Ts, _ = x_sample.shape
    Hr, Hg = state_rwkv.shape[2], state_gdn.shape[2]
    W = Hr * RW_HEAD
    assert W == Hg * GD_HEAD and Ts <= SAMPLE_STEPS and Ts >= GD_CONV - 1 and Tp % PROMPT_CHUNK == 0
    n_w, n_a, n_g = rw_w2.shape[1], rw_a2.shape[1], rw_g2.shape[1]
    sw, sa, sg = (_round_up(n, LANES) for n in (n_w, n_a, n_g))
    SW = sw + sa + sg + LANES
    n_mem, xa_heads = cache_mem_k.shape[2], cache_mem_k.shape[3]
    xa_width = xa_heads * cache_mem_k.shape[4]
    alpha = (2 * depth) ** 0.25
    Mp, Ms = Bp * Tp, Bs * SAMPLE_STEPS
    M = Mp + Ms
    tm = _pick_tile(M, (512, 384, 256, 128))
    tf = _pick_tile(ff_wg.shape[-1], (512, 256, 128))
    c_w, c_a, c_g = 3 * W, 3 * W + n_w, 3 * W + n_w + n_a
    c_gd = c_g + n_g

    def to_layout(t):
        parts = [t[..., :c_w], t[..., c_gd:c_gd + 4 * W],
                 _lane_pad(t[..., c_w:c_a], sw), _lane_pad(t[..., c_a:c_g], sa), _lane_pad(t[..., c_g:c_gd], sg),
                 _lane_pad(t[..., c_gd + 4 * W:], LANES)]
        return jnp.concatenate(parts, axis=-1)

    def rwkv_layout(t):
        parts = [t[..., :c_w], _lane_pad(t[..., c_w:c_a], sw), _lane_pad(t[..., c_a:c_g], sa),
                 _lane_pad(t[..., c_g:c_gd], sg + LANES)]
        return jnp.concatenate(parts, axis=-1)

    def rwkv_cols(t):
        s0 = 3 * W
        return jnp.concatenate([t[..., :3 * W], t[..., s0:s0 + n_w], t[..., s0 + sw:s0 + sw + n_a],
                                t[..., s0 + sw + sa:s0 + sw + sa + n_g]], axis=-1)

    w_proj = to_layout(w_in).astype(BF16)
    w_kv = jnp.concatenate([xa_wk, xa_wv], axis=2).astype(BF16)
    w_o, w_q, w_xo = w_out.astype(BF16), xa_wq.astype(BF16), xa_wo.astype(BF16)
    w_g, w_u, w_d = ff_wg.astype(BF16), ff_wu.astype(BF16), ff_wd.astype(BF16)

    x_s = jnp.pad(x_sample, ((0, 0), (0, SAMPLE_STEPS - Ts), (0, 0)))
    x = jnp.concatenate([x_prompt.reshape(Mp, D), x_s.reshape(Ms, D)], axis=0)
    xb = x.astype(BF16)
    mem = mem_prompt.reshape(Bp * n_mem, D).astype(BF16)
    mem_k = cache_mem_k.reshape(depth, Bs, n_mem * xa_heads, -1)
    mem_v = cache_mem_v.reshape(depth, Bs, n_mem * xa_heads, -1)
    zeros = lambda *s: jnp.zeros(s, F32)
    tq = _pick_tile(Tp, (512, 256, 128))
    rw_p, gd_p = zeros(depth, Bp, Hr, RW_HEAD, RW_HEAD), zeros(depth, Bp, Hg, GD_HEAD, GD_HEAD)
    rw_s, gd_s = zeros(*state_rwkv.shape), zeros(*state_gdn.shape)
    rw_p0, gd_p0 = zeros(1, Bp, Hr, RW_HEAD, RW_HEAD), zeros(1, Bp, Hg, GD_HEAD, GD_HEAD)
    conv_at = SUBLANES - (SAMPLE_STEPS - Ts) - (GD_CONV - 1)

    outs = {n: [] for n in ('p_sh', 'p_cv', 'p_mk', 'p_mv', 's_sh', 's_cv')}
    for l in range(depth):
        rw = {'T': None, 'mu': rwkv_layout(mu_shift[l])[None],
              'vec': jnp.stack([rw_w0[l], rw_a0[l], rw_kk[l], rw_ka[l], rw_rk[l].reshape(W), rw_lnx_w[l],
                                rw_lnx_b[l], jnp.zeros((W,), F32)]),
              'w2': _row_pad(rw_w2[l], sw).astype(BF16), 'a2': _row_pad(rw_a2[l], sa).astype(BF16),
              'g2': _row_pad(rw_g2[l], sg).astype(BF16)}
        gd = {'T': None, 'SW': SW, 'conv_w': gd_conv_w[l], 'gate': jnp.stack([gd_a_log[l], gd_dt_bias[l]]),
              'norm_w': gd_norm_w[l][None]}

        kv = matmul(mem, w_kv, l, tm=_pick_tile(Bp * n_mem, (512, 256, 128)),
                    tn=_pick_tile(2 * xa_width, (512, 256, 128)))
        P = matmul(xb, w_proj, l, tm=tm, tn=_pick_tile(7 * W + SW, (512, 256, 128)))

        prev_s = rwkv_layout(state_shift[l])[:, None]
        conv_s = jnp.pad(state_conv[l], ((0, 0), (SUBLANES - (GD_CONV - 1), 0), (0, 0)))
        o_rw_p, rw_p, sh_p = rwkv7_mixer(P, 0, zeros(Bp, 1, 3 * W + SW), rw_p0, 0, rw_p, l, dict(rw, T=Tp),
                                         chunk=PROMPT_CHUNK, seqs=1, t_valid=PROMPT_CHUNK)
        o_gd_p, gd_p, cv_p = gdn_mixer(P, 0, zeros(Bp, SUBLANES, 3 * W), gd_p0, 0, gd_p, l, dict(gd, T=Tp),
                                       chunk=PROMPT_CHUNK, seqs=1, t_valid=PROMPT_CHUNK)
        o_rw_s, rw_s, sh_s = rwkv7_mixer(P, Mp, prev_s, state_rwkv, l, rw_s, l, dict(rw, T=SAMPLE_STEPS),
                                         chunk=SAMPLE_STEPS, seqs=SAMPLE_SEQS, t_valid=Ts)
        o_gd_s, gd_s, cv_s = gdn_mixer(P, Mp, conv_s, state_gdn, l, gd_s, l, dict(gd, T=SAMPLE_STEPS),
                                       chunk=SAMPLE_STEPS, seqs=SAMPLE_SEQS, t_valid=Ts)

        x, q = mix_out_ln_q(jnp.concatenate([o_rw_p, o_rw_s], axis=0), jnp.concatenate([o_gd_p, o_gd_s], axis=0),
                            w_o, w_q, l, x, ln1_g[l], ln1_b[l], alpha=alpha, tm=tm)

        att_p = cross_attention_prompt(q, kv, n_mem=n_mem, heads=xa_heads, tq=tq, q_tiles=Tp // tq)
        att_s = cross_attention_sample(q, Mp, mem_k, mem_v, l, heads=xa_heads, tq=SAMPLE_STEPS, seqs=SAMPLE_SEQS)
        x, xb = matmul_ln(jnp.concatenate([att_p, att_s], axis=0), w_xo, l, x, ln2_g[l], ln2_b[l],
                          alpha=alpha, tm=tm)
        x, xb = ffn_ln(xb, w_g, w_u, w_d, l, x, ln3_g[l], ln3_b[l], alpha=alpha, tm=tm, tf=tf)

        outs['p_sh'].append(rwkv_cols(sh_p[:, 0])); outs['s_sh'].append(rwkv_cols(sh_s[:, 0]))
        outs['p_cv'].append(cv_p[:, SUBLANES - (GD_CONV - 1):])
        outs['s_cv'].append(cv_s[:, conv_at:conv_at + GD_CONV - 1])
        outs['p_mk'].append(kv[:, :xa_width].reshape(Bp, n_mem, xa_heads, -1))
        outs['p_mv'].append(kv[:, xa_width:].reshape(Bp, n_mem, xa_heads, -1))

    st = {n: jnp.stack(v) for n, v in outs.items()}
    return (x[:Mp].reshape(Bp, Tp, D), x[Mp:].reshape(Bs, SAMPLE_STEPS, D)[:, :Ts], rw_p, st['p_sh'],
            gd_p, st['p_cv'], st['p_mk'], st['p_mv'], rw_s, st['s_sh'], gd_s, st['s_cv'])
```

```python
import functools

import jax
import jax.numpy as jnp
from jax import lax
from jax.experimental import pallas as pl
from jax.experimental.pallas import tpu as pltpu

F32 = jnp.float32
BF16 = jnp.bfloat16

LANES = 128
SUBLANES = 8
RW_HEAD = 64
GD_HEAD = 128
GD_CONV = 4
LN_EPS = 1e-5
RW_GN_EPS = 64e-5
GD_NORM_EPS = 1e-6
L2_EPS = 1e-12

PROMPT_CHUNK = 64
SAMPLE_STEPS = SUBLANES
SAMPLE_SEQS = 8
RW_PROMPT_CHUNKS = 2
GD_PROMPT_CHUNKS = 4

VMEM_LIMIT_BYTES = 56 * 1024 * 1024

_NN = (((1,), (0,)), ((), ()))
_NT = (((1,), (1,)), ((), ()))
_TN = (((0,), (0,)), ((), ()))
_TT = (((0,), (1,)), ((), ()))


def _cparams(*sem):
    return pltpu.CompilerParams(dimension_semantics=sem, vmem_limit_bytes=VMEM_LIMIT_BYTES)


def _round_up(n, m):
    return -(-n // m) * m


def _pick_tile(n, candidates):
    for c in candidates:
        if n % c == 0:
            return c
    raise ValueError(f"no tile in {candidates} divides {n}")


def _mm_kernel(x_ref, w_ref, o_ref):
    o_ref[...] = jnp.dot(x_ref[...], w_ref[...], preferred_element_type=F32).astype(o_ref.dtype)


def matmul(x, w, layer, *, tm, tn, out_dtype=F32):
    M, K = x.shape
    N = w.shape[2]
    assert M % tm == 0 and N % tn == 0
    return pl.pallas_call(
        _mm_kernel,
        grid=(M // tm, N // tn),
        in_specs=[pl.BlockSpec((tm, K), lambda i, j: (i, 0)),
                  pl.BlockSpec((None, K, tn), lambda i, j: (layer, 0, j))],
        out_specs=pl.BlockSpec((tm, tn), lambda i, j: (i, j)),
        out_shape=jax.ShapeDtypeStruct((M, N), out_dtype),
        compiler_params=_cparams("parallel", "arbitrary"),
        name="matmul",
    )(x, w)


def _layer_norm_rows(y, g, b):
    mu = jnp.mean(y, -1, keepdims=True)
    yc = y - mu
    var = jnp.mean(yc * yc, -1, keepdims=True)
    return yc * lax.rsqrt(var + LN_EPS) * g + b


def _mix_out_kernel(a_ref, b_ref, wa_ref, wb_ref, res_ref, g_ref, beta_ref, wq_ref, x_ref, q_ref, *, alpha):
    h = jnp.dot(a_ref[...], wa_ref[...], preferred_element_type=F32)
    h += jnp.dot(b_ref[...], wb_ref[...], preferred_element_type=F32)
    x = _layer_norm_rows(alpha * res_ref[...] + h, g_ref[...], beta_ref[...])
    x_ref[...] = x
    q_ref[...] = jnp.dot(x.astype(BF16), wq_ref[...], preferred_element_type=F32).astype(BF16)


def mix_out_ln_q(o_a, o_b, w, wq, layer, res, g, b, *, alpha, tm):
    M, Ka = o_a.shape
    Kb = o_b.shape[1]
    D = w.shape[2]
    Q = wq.shape[2]
    assert M % tm == 0 and Ka == Kb and w.shape[1] == Ka + Kb
    rows = lambda n: pl.BlockSpec((tm, n), lambda i: (i, 0))
    vec = pl.BlockSpec((1, D), lambda i: (0, 0))
    return pl.pallas_call(
        functools.partial(_mix_out_kernel, alpha=alpha),
        grid=(M // tm,),
        in_specs=[rows(Ka), rows(Kb),
                  pl.BlockSpec((None, Ka, D), lambda i: (layer, 0, 0)),
                  pl.BlockSpec((None, Kb, D), lambda i: (layer, 1, 0)),
                  rows(D), vec, vec,
                  pl.BlockSpec((None, D, Q), lambda i: (layer, 0, 0))],
        out_specs=[rows(D), rows(Q)],
        out_shape=[jax.ShapeDtypeStruct((M, D), F32), jax.ShapeDtypeStruct((M, Q), BF16)],
        compiler_params=_cparams("parallel"),
        name="mix_out_ln_q",
    )(o_a, o_b, w, w, res, g.reshape(1, D), b.reshape(1, D), wq)


def _mm_ln_kernel(x_ref, w_ref, res_ref, g_ref, b_ref, o_ref, ob_ref, *, alpha):
    h = jnp.dot(x_ref[...], w_ref[...], preferred_element_type=F32)
    y = _layer_norm_rows(alpha * res_ref[...] + h, g_ref[...], b_ref[...])
    o_ref[...] = y
    ob_ref[...] = y.astype(BF16)


def matmul_ln(x, w, layer, res, g, b, *, alpha, tm):
    M, K = x.shape
    D = w.shape[2]
    assert M % tm == 0
    return pl.pallas_call(
        functools.partial(_mm_ln_kernel, alpha=alpha),
        grid=(M // tm,),
        in_specs=[pl.BlockSpec((tm, K), lambda i: (i, 0)),
                  pl.BlockSpec((None, K, D), lambda i: (layer, 0, 0)),
                  pl.BlockSpec((tm, D), lambda i: (i, 0)),
                  pl.BlockSpec((1, D), lambda i: (0, 0)),
                  pl.BlockSpec((1, D), lambda i: (0, 0))],
        out_specs=[pl.BlockSpec((tm, D), lambda i: (i, 0))] * 2,
        out_shape=[jax.ShapeDtypeStruct((M, D), F32), jax.ShapeDtypeStruct((M, D), BF16)],
        compiler_params=_cparams("parallel"),
        name="matmul_ln",
    )(x, w, res, g.reshape(1, D), b.reshape(1, D))


def _ffn_ln_kernel(x_ref, wg_ref, wu_ref, wd_ref, res_ref, g_ref, b_ref, o_ref, ob_ref, acc_ref, *, alpha):
    f = pl.program_id(1)

    @pl.when(f == 0)
    def _():
        acc_ref[...] = jnp.zeros_like(acc_ref)

    x = x_ref[...]
    gate = jnp.dot(x, wg_ref[...], preferred_element_type=F32)
    up = jnp.dot(x, wu_ref[...], preferred_element_type=F32)
    hidden = (gate * jax.nn.sigmoid(gate) * up).astype(BF16)
    acc_ref[...] += jnp.dot(hidden, wd_ref[...], preferred_element_type=F32)

    @pl.when(f == pl.num_programs(1) - 1)
    def _():
        y = _layer_norm_rows(alpha * res_ref[...] + acc_ref[...], g_ref[...], b_ref[...])
        o_ref[...] = y
        ob_ref[...] = y.astype(BF16)


def ffn_ln(x, wg, wu, wd, layer, res, g, b, *, alpha, tm, tf):
    M, D = x.shape
    Fd = wg.shape[2]
    assert M % tm == 0 and Fd % tf == 0
    return pl.pallas_call(
        functools.partial(_ffn_ln_kernel, alpha=alpha),
        grid=(M // tm, Fd // tf),
        in_specs=[pl.BlockSpec((tm, D), lambda i, f: (i, 0)),
                  pl.BlockSpec((None, D, tf), lambda i, f: (layer, 0, f)),
                  pl.BlockSpec((None, D, tf), lambda i, f: (layer, 0, f)),
                  pl.BlockSpec((None, tf, D), lambda i, f: (layer, f, 0)),
                  pl.BlockSpec((tm, D), lambda i, f: (i, 0)),
                  pl.BlockSpec((1, D), lambda i, f: (0, 0)),
                  pl.BlockSpec((1, D), lambda i, f: (0, 0))],
        out_specs=[pl.BlockSpec((tm, D), lambda i, f: (i, 0))] * 2,
        out_shape=[jax.ShapeDtypeStruct((M, D), F32), jax.ShapeDtypeStruct((M, D), BF16)],
        scratch_shapes=[pltpu.VMEM((tm, D), F32)],
        compiler_params=_cparams("parallel", "arbitrary"),
        name="ffn_ln",
    )(x, wg, wu, wd, res, g.reshape(1, D), b.reshape(1, D))


def _mm(a, b, dims):
    return lax.dot_general(a.astype(BF16), b.astype(BF16), dims, preferred_element_type=F32)


def _three_bf16_terms(a):
    a1 = a.astype(BF16)
    r1 = a - a1.astype(F32)
    a2 = r1.astype(BF16)
    return a1, a2, (r1 - a2.astype(F32)).astype(BF16)


def _unit_lower_inverse(xs, c):
    row = lax.broadcasted_iota(jnp.int32, (c, c), 0)
    col = lax.broadcasted_iota(jnp.int32, (c, c), 1)
    eye = jnp.where(row == col, 1.0, 0.0).astype(F32)
    ts = [eye + x for x in xs]
    ps = list(xs)
    span = 2
    while span < c:
        ps = [_mm(p, p, _NN) for p in ps]
        ts = [t + _mm(t, p, _NN) for t, p in zip(ts, ps)]
        span *= 2
    return ts


def _step_in_seq(R, C):
    return lax.broadcasted_iota(jnp.int32, (R, 1), 0) & (C - 1)


def _seq_tri(R, C):
    row = lax.broadcasted_iota(jnp.int32, (R, R), 0)
    col = lax.broadcasted_iota(jnp.int32, (R, R), 1)
    return jnp.where(row >= col, jnp.where((row & -C) == (col & -C), 1.0, 0.0), 0.0).astype(BF16)


def _per_seq_rows(rows, C):
    rows = [jnp.broadcast_to(t, (C, t.shape[-1])) for t in rows]
    return rows[0] if len(rows) == 1 else jnp.concatenate(rows, axis=0)


def _delayed_rows(x, tails, j, BB, C):
    xj = pltpu.roll(x, j, 0)
    row = lax.broadcasted_iota(jnp.int32, (SUBLANES, 1), 0)
    pieces = []
    for i in range(BB):
        pieces.append(jnp.where(row < j, pltpu.roll(tails[i], j, 0), xj[i * C:i * C + SUBLANES]))
        if C > SUBLANES:
            pieces.append(xj[i * C + SUBLANES:(i + 1) * C])
    return pieces[0] if len(pieces) == 1 else jnp.concatenate(pieces, axis=0)


def _half_sums(x, N):
    low = lax.broadcasted_iota(jnp.int32, (1, LANES), 1) < N
    out = []
    for j in range(x.shape[1] // LANES):
        xv = x[:, j * LANES:(j + 1) * LANES]
        lo = jnp.sum(jnp.where(low, xv, 0.0), -1, keepdims=True)
        out += [lo, jnp.sum(jnp.where(low, 0.0, xv), -1, keepdims=True)]
    return out


def _half_bcast(cols, N):
    low = lax.broadcasted_iota(jnp.int32, (1, LANES), 1) < N
    return jnp.concatenate([jnp.where(low, cols[2 * j], cols[2 * j + 1]) for j in range(len(cols) // 2)], axis=1)


def _rwkv_kernel(r_ref, k_ref, v_ref, sm_ref, prev_ref, mu_ref, vec_ref, w2_ref, a2_ref, g2_ref,
                 s0_ref, st_buf_ref, o_ref, st_ref, last_ref, s_scr, tail_scr, o_scr, *, C, H, N, NS, L, t_valid):
    c = pl.program_id(1)
    W = H * N
    BB = NS * L
    R = BB * C
    Rs = L * C
    sw, sa, sg = w2_ref.shape[0], a2_ref.shape[0], g2_ref.shape[0]

    @pl.when(c == 0)
    def _():
        s_scr[...] = s0_ref[...]
        tail_scr[...] = prev_ref[...]

    first = _step_in_seq(R, Rs) == 0
    vec = vec_ref[...]
    w0, a0, k_k, k_a, r_k, ln_w, ln_b = (vec[j:j + 1] for j in range(7))
    mu = mu_ref[...]
    tail = _per_seq_rows([tail_scr[s] for s in range(NS)], Rs)

    def shifted(x, lo):
        hi = lo + x.shape[1]
        prev = jnp.where(first, tail[:, lo:hi], pltpu.roll(x, 1, 0))
        return x + (prev - x) * mu[:, lo:hi]

    xr, xk, xv, xs = r_ref[...], k_ref[...], v_ref[...], sm_ref[...]
    r, k, v, ms = shifted(xr, 0), shifted(xk, W), shifted(xv, 2 * W), shifted(xs, 3 * W)
    for x, lo in ((xr, 0), (xk, W), (xv, 2 * W), (xs, 3 * W)):
        for s in range(NS):
            end = s * Rs + (L - 1) * C + t_valid
            tail_scr[s, :, lo:lo + x.shape[1]] = x[end - 1:end]

    lora_w = jnp.dot(jnp.tanh(ms[:, :sw]).astype(BF16), w2_ref[...], preferred_element_type=F32)
    lora_a = jnp.dot(ms[:, sw:sw + sa].astype(BF16), a2_ref[...], preferred_element_type=F32)
    gate = jnp.dot(jax.nn.sigmoid(ms[:, sw + sa:sw + sa + sg]).astype(BF16), g2_ref[...],
                   preferred_element_type=F32)
    lw = -jnp.exp(-jax.nn.softplus(-(w0 + lora_w)) - 0.5)
    a = jax.nn.sigmoid(a0 + lora_a)
    kk = k * k_k
    kk = kk * _half_bcast([lax.rsqrt(s + L2_EPS) for s in _half_sums(kk * kk, N)], N)
    k = k * (1.0 + (a - 1.0) * k_a)
    if t_valid < C:
        valid = _step_in_seq(R, C) < t_valid
        r, k, v, kk, lw = (jnp.where(valid, t, 0.0) for t in (r, k, v, kk, lw))

    tri = _seq_tri(R, C)
    G = sum(jnp.dot(tri, t, preferred_element_type=F32) for t in _three_bf16_terms(lw))
    mid = C // 2 - 1
    g_mid = [G[i * C + mid:i * C + mid + 1] for i in range(BB)]
    g_last = [G[(i + 1) * C - 1:(i + 1) * C] for i in range(BB)]
    Gm = G - _per_seq_rows(g_mid, C)
    e_out = jnp.exp(-Gm)
    e_tail = jnp.exp(_per_seq_rows(g_last, C) - G)
    b_in = kk * a
    A_, R_ = -kk * jnp.exp(Gm - lw), r * jnp.exp(Gm)
    B_, K_ = b_in * e_out, k * e_out
    Bt, Kt = b_in * e_tail, k * e_tail
    e_mid, e_last = [jnp.exp(t) for t in g_mid], [jnp.exp(t) for t in g_last]

    row2 = lax.broadcasted_iota(jnp.int32, (2 * C, 2 * C), 0)
    col2 = lax.broadcasted_iota(jnp.int32, (2 * C, 2 * C), 1)
    rowc = jnp.where(row2 >= C, row2 - C, row2)
    colc = jnp.where(col2 >= C, col2 - C, col2)
    keep = jnp.where(row2 >= C, rowc, rowc - 1) >= colc

    rs = lambda i: slice(i * C, (i + 1) * C)
    sl = lambda h: slice(h * N, (h + 1) * N)
    pair = lambda x, y, i, h: jnp.concatenate([x[rs(i), sl(h)], y[rs(i), sl(h)]], axis=0)
    every = [(i, h) for i in range(BB) for h in range(H)]
    AR = {ih: pair(A_, R_, *ih) for ih in every}
    M = {ih: jnp.where(keep, _mm(AR[ih], pair(B_, K_, *ih), _NT), 0.0) for ih in every}
    AV = {(i, h): _mm(M[i, h][:C, C:], v[rs(i), sl(h)], _NN) for i, h in every}
    T = dict(zip(every, _unit_lower_inverse([M[ih][:C, :C] for ih in every], C)))
    S = {(s, h): s_scr[s, h] for s in range(NS) for h in range(H)}
    for lvl in range(L):
        ch = [(s, s * L + lvl, h) for s in range(NS) for h in range(H)]
        SR = {(i, h): _mm(AR[i, h], S[s, h] * e_mid[i][:, sl(h)], _NT) for s, i, h in ch}
        U = {(i, h): _mm(T[i, h], SR[i, h][:C] + AV[i, h], _NN) for _, i, h in ch}
        UV = {(i, h): jnp.concatenate([U[i, h], v[rs(i), sl(h)]], axis=0) for _, i, h in ch}
        O = {(i, h): SR[i, h][C:] + _mm(M[i, h][C:], UV[i, h], _NN) for _, i, h in ch}
        S = {(s, h): S[s, h] * e_last[i][:, sl(h)] + _mm(UV[i, h], pair(Bt, Kt, i, h), _TN) for s, i, h in ch}
        for _, i, h in ch:
            o_scr[rs(i), sl(h)] = O[i, h]
    for (s, h), val in S.items():
        s_scr[s, h] = val

    o = o_scr[...]
    o = o - _half_bcast([s * (1.0 / N) for s in _half_sums(o, N)], N)
    o = o * _half_bcast([lax.rsqrt(s * (1.0 / N) + RW_GN_EPS) for s in _half_sums(o * o, N)], N) * ln_w + ln_b
    bonus = _half_bcast(_half_sums(r * k * r_k, N), N) * v
    o_ref[...] = ((o + bonus) * gate).astype(o_ref.dtype)

    @pl.when(c == pl.num_programs(1) - 1)
    def _():
        st_ref[...] = s_scr[...]
        last_ref[...] = tail_scr[...]


def _state_specs(states, s0_layer, layer, seqs):
    blk = (None, seqs) + states.shape[2:]
    return (pl.BlockSpec(blk, lambda i, c: (s0_layer, i, 0, 0, 0)),
            pl.BlockSpec(blk, lambda i, c: (layer, i, 0, 0, 0)))


def rwkv7_mixer(P, row0, prev, s0, s0_layer, st_buf, layer, prm, *, chunk, seqs, chunks, t_valid):
    _, B, H, N, _ = s0.shape
    W = H * N
    SW = prm['mu'].shape[1] - 3 * W
    T = prm['T']
    blk = seqs * chunks * chunk
    nc = T // (chunks * chunk)
    assert T % (chunks * chunk) == 0 and B % seqs == 0 and row0 % blk == 0 and (7 * W) % SW == 0
    assert (seqs == 1 or nc == 1) and (chunks == 1 or t_valid == chunk)
    assert 2 * N == LANES and chunk & (chunk - 1) == 0 and st_buf.shape[1:] == s0.shape[1:]
    rb0 = row0 // blk
    col = lambda j: pl.BlockSpec((blk, W), lambda i, c: (rb0 + i * nc + c, j))
    full = lambda a: pl.BlockSpec(a.shape, lambda i, c: (0,) * a.ndim)
    st_in, st_out = _state_specs(s0, s0_layer, layer, seqs)
    row = pl.BlockSpec((seqs, 1, 3 * W + SW), lambda i, c: (i, 0, 0))
    consts = [prm['mu'], prm['vec'], prm['w2'], prm['a2'], prm['g2']]
    return pl.pallas_call(
        functools.partial(_rwkv_kernel, C=chunk, H=H, N=N, NS=seqs, L=chunks, t_valid=t_valid),
        grid=(B // seqs, nc),
        in_specs=[col(0), col(1), col(2),
                  pl.BlockSpec((blk, SW), lambda i, c: (rb0 + i * nc + c, 7 * W // SW)), row]
                 + [full(a) for a in consts] + [st_in, pl.BlockSpec(memory_space=pl.ANY)],
        out_specs=[pl.BlockSpec((blk, W), lambda i, c: (i * nc + c, 0)), st_out, row],
        out_shape=[jax.ShapeDtypeStruct((B * T, W), BF16), jax.ShapeDtypeStruct(st_buf.shape, F32),
                   jax.ShapeDtypeStruct((B, 1, 3 * W + SW), F32)],
        scratch_shapes=[pltpu.VMEM((seqs, H, N, N), F32), pltpu.VMEM((seqs, 1, 3 * W + SW), F32),
                        pltpu.VMEM((blk, W), F32)],
        input_output_aliases={5 + len(consts) + 1: 1},
        compiler_params=_cparams("parallel", "arbitrary"),
        name="rwkv7_mixer",
    )(P, P, P, P, prev, *consts, s0, st_buf)


def _gdn_kernel(q_ref, k_ref, v_ref, z_ref, sm_ref, cprev_ref, cw_ref, gate_ref, nw_ref, s0_ref, st_buf_ref,
                o_ref, st_ref, last_ref, s_scr, tail_scr, *, C, H, N, NS, L, t_valid, gate_lane):
    c = pl.program_id(1)
    W = H * N
    BB = NS * L
    R = BB * C
    Rs = L * C

    @pl.when(c == 0)
    def _():
        s_scr[...] = s0_ref[...]
        tail_scr[...] = cprev_ref[...]

    cw = cw_ref[...]
    nw = nw_ref[...]
    tails = [tail_scr[s] for s in range(NS)]

    def conv_silu(x, lo):
        hi = lo + W
        acc = x * cw[GD_CONV - 1:GD_CONV, lo:hi]
        for j in range(1, GD_CONV):
            acc = acc + (_delayed_rows(x, [t[:, lo:hi] for t in tails], j, NS, Rs)
                         * cw[GD_CONV - 1 - j:GD_CONV - j, lo:hi])
        return acc * jax.nn.sigmoid(acc)

    xq, xk, xv = q_ref[...], k_ref[...], v_ref[...]
    q_all, k_all, v_all = conv_silu(xq, 0), conv_silu(xk, W), conv_silu(xv, 2 * W)
    for x, lo in ((xq, 0), (xk, W), (xv, 2 * W)):
        for s in range(NS):
            tail_scr[s, :, lo:lo + W] = x[(s + 1) * Rs - SUBLANES:(s + 1) * Rs]
    z_all = z_ref[...]
    gates = sm_ref[:, gate_lane:gate_lane + LANES]
    beta = jax.nn.sigmoid(gates[:, :H])
    g = -jnp.exp(gate_ref[0:1, :]) * jax.nn.softplus(gates[:, H:2 * H] + gate_ref[1:2, :])
    if t_valid < C:
        valid = _step_in_seq(R, C) < t_valid
        beta, g, q_all, k_all, v_all = (jnp.where(valid, t, 0.0) for t in (beta, g, q_all, k_all, v_all))
    tri = _seq_tri(R, C)
    g_terms = _three_bf16_terms(g)
    Gc = sum(jnp.dot(tri, t, preferred_element_type=F32) for t in g_terms)
    Gr = sum(lax.dot_general(t, tri, _TT, preferred_element_type=F32) for t in g_terms)

    row2 = lax.broadcasted_iota(jnp.int32, (2 * C, C), 0)
    col2 = lax.broadcasted_iota(jnp.int32, (2 * C, C), 1)
    keep = jnp.where(row2 >= C, row2 - C, row2 - 1) >= col2

    l2 = lambda t: t * lax.rsqrt(jnp.sum(t * t, -1, keepdims=True) + L2_EPS)
    sl = lambda h: slice(h * N, (h + 1) * N)
    qn = [l2(q_all[:, sl(h)]) * (N ** -0.5) for h in range(H)]
    kn = [l2(k_all[:, sl(h)]) for h in range(H)]

    rs = lambda i: slice(i * C, (i + 1) * C)
    every = [(i, h) for i in range(BB) for h in range(H)]
    gc = {(i, h): Gc[rs(i), h:h + 1] for i, h in every}
    g_last = {(i, h): Gc[(i + 1) * C - 1:(i + 1) * C, h:h + 1] for i, h in every}
    gc2 = {ih: jnp.concatenate([gc[ih], gc[ih]], axis=0) for ih in every}
    decay = {(i, h): jnp.where(keep, jnp.exp(jnp.where(keep, gc2[i, h] - Gr[h:h + 1, rs(i)], 0.0)), 0.0)
             for i, h in every}
    bh = {(i, h): beta[rs(i), h:h + 1] for i, h in every}
    kj = {(i, h): kn[h][rs(i)] for i, h in every}
    KQ = {(i, h): jnp.concatenate([kj[i, h] * bh[i, h], qn[h][rs(i)]], axis=0) for i, h in every}
    M = {ih: _mm(KQ[ih], kj[ih], _NT) * decay[ih] for ih in every}
    T = dict(zip(every, _unit_lower_inverse([-M[ih][:C] for ih in every], C)))
    S = {(s, h): s_scr[s, h] for s in range(NS) for h in range(H)}
    for lvl in range(L):
        ch = [(s, s * L + lvl, h) for s in range(NS) for h in range(H)]
        SR = {(i, h): _mm(KQ[i, h] * jnp.exp(gc2[i, h]), S[s, h], _NN) for s, i, h in ch}
        v_new = {(i, h): _mm(T[i, h], v_all[rs(i), sl(h)] * bh[i, h] - SR[i, h][:C], _NN) for _, i, h in ch}
        O = {(i, h): SR[i, h][C:] + _mm(M[i, h][C:], v_new[i, h], _NN) for _, i, h in ch}
        S = {(s, h): S[s, h] * jnp.exp(g_last[i, h])
             + _mm(kj[i, h] * jnp.exp(g_last[i, h] - gc[i, h]), v_new[i, h], _TN) for s, i, h in ch}
        for _, i, h in ch:
            zh = z_all[rs(i), sl(h)]
            o = O[i, h] * lax.rsqrt(jnp.mean(O[i, h] * O[i, h], -1, keepdims=True) + GD_NORM_EPS) * nw
            o_ref[rs(i), sl(h)] = (o * (zh * jax.nn.sigmoid(zh))).astype(o_ref.dtype)
    for (s, h), val in S.items():
        s_scr[s, h] = val

    @pl.when(c == pl.num_programs(1) - 1)
    def _():
        st_ref[...] = s_scr[...]
        last_ref[...] = tail_scr[...]


def gdn_mixer(P, row0, conv_prev, s0, s0_layer, st_buf, layer, prm, *, chunk, seqs, chunks, t_valid):
    _, B, H, N, _ = s0.shape
    W = H * N
    SW = prm['SW']
    T = prm['T']
    blk = seqs * chunks * chunk
    nc = T // (chunks * chunk)
    assert T % (chunks * chunk) == 0 and B % seqs == 0 and row0 % blk == 0 and chunk >= SUBLANES
    assert (seqs == 1 or nc == 1) and (chunks == 1 or t_valid == chunk) and 2 * H <= LANES
    assert N == LANES and chunk & (chunk - 1) == 0 and st_buf.shape[1:] == s0.shape[1:]
    rb0 = row0 // blk
    col = lambda j: pl.BlockSpec((blk, W), lambda i, c: (rb0 + i * nc + c, j))
    full = lambda a: pl.BlockSpec(a.shape, lambda i, c: (0,) * a.ndim)
    st_in, st_out = _state_specs(s0, s0_layer, layer, seqs)
    rows = pl.BlockSpec((seqs, SUBLANES, 3 * W), lambda i, c: (i, 0, 0))
    consts = [prm['conv_w'], prm['gate'], prm['norm_w']]
    return pl.pallas_call(
        functools.partial(_gdn_kernel, C=chunk, H=H, N=N, NS=seqs, L=chunks, t_valid=t_valid,
                          gate_lane=SW - LANES),
        grid=(B // seqs, nc),
        in_specs=[col(3), col(4), col(5), col(6),
                  pl.BlockSpec((blk, SW), lambda i, c: (rb0 + i * nc + c, 7 * W // SW)), rows]
                 + [full(a) for a in consts] + [st_in, pl.BlockSpec(memory_space=pl.ANY)],
        out_specs=[pl.BlockSpec((blk, W), lambda i, c: (i * nc + c, 0)), st_out, rows],
        out_shape=[jax.ShapeDtypeStruct((B * T, W), BF16), jax.ShapeDtypeStruct(st_buf.shape, F32),
                   jax.ShapeDtypeStruct((B, SUBLANES, 3 * W), F32)],
        scratch_shapes=[pltpu.VMEM((seqs, H, N, N), F32), pltpu.VMEM((seqs, SUBLANES, 3 * W), F32)],
        input_output_aliases={6 + len(consts) + 1: 1},
        compiler_params=_cparams("parallel", "arbitrary"),
        name="gdn_mixer",
    )(P, P, P, P, P, conv_prev, *consts, s0, st_buf)


def _xattn_kernel(q_ref, k_ref, v_ref, o_ref, *, H, Dh, BB, tq, head_rows):
    scale = Dh ** -0.5
    pairs = [(b, h) for b in range(BB) for h in range(H)]
    sl = [slice(h * Dh, (h + 1) * Dh) for h in range(H)]

    def head(ref, b, h):
        if head_rows:
            return ref[b, pl.ds(h, ref.shape[1] // H, stride=H), :].astype(BF16)
        return ref[:, sl[h]].astype(BF16)

    q = [q_ref[b * tq:(b + 1) * tq, :] for b in range(BB)]
    s = [lax.dot_general(q[b][:, sl[h]], head(k_ref, b, h), _NT, preferred_element_type=F32) * scale
         for b, h in pairs]
    e = [jnp.exp(t - jnp.max(t, -1, keepdims=True)) for t in s]
    pr = [(t / jnp.sum(t, -1, keepdims=True)).astype(BF16) for t in e]
    o = [jnp.dot(pr[j], head(v_ref, b, h), preferred_element_type=F32) for j, (b, h) in enumerate(pairs)]
    for j, (b, h) in enumerate(pairs):
        o_ref[b * tq:(b + 1) * tq, sl[h]] = o[j].astype(o_ref.dtype)


def cross_attention_prompt(q, kv, *, n_mem, heads, tq, q_tiles):
    Wd = kv.shape[1] // 2
    B = kv.shape[0] // n_mem
    return pl.pallas_call(
        functools.partial(_xattn_kernel, H=heads, Dh=Wd // heads, BB=1, tq=tq, head_rows=False),
        grid=(B, q_tiles),
        in_specs=[pl.BlockSpec((tq, Wd), lambda i, t: (i * q_tiles + t, 0)),
                  pl.BlockSpec((n_mem, Wd), lambda i, t: (i, 0)),
                  pl.BlockSpec((n_mem, Wd), lambda i, t: (i, 1))],
        out_specs=pl.BlockSpec((tq, Wd), lambda i, t: (i * q_tiles + t, 0)),
        out_shape=jax.ShapeDtypeStruct((B * q_tiles * tq, Wd), BF16),
        compiler_params=_cparams("parallel", "arbitrary"),
        name="cross_attention_prompt",
    )(q, kv, kv)


def cross_attention_sample(q, row0, mem_k, mem_v, layer, *, heads, tq, seqs):
    _, B, rows, Dh = mem_k.shape
    Wd = heads * Dh
    blk = seqs * tq
    assert B % seqs == 0 and row0 % blk == 0
    rb0 = row0 // blk
    kv = pl.BlockSpec((None, seqs, rows, Dh), lambda i: (layer, i, 0, 0))
    return pl.pallas_call(
        functools.partial(_xattn_kernel, H=heads, Dh=Dh, BB=seqs, tq=tq, head_rows=True),
        grid=(B // seqs,),
        in_specs=[pl.BlockSpec((blk, Wd), lambda i: (rb0 + i, 0)), kv, kv],
        out_specs=pl.BlockSpec((blk, Wd), lambda i: (i, 0)),
        out_shape=jax.ShapeDtypeStruct((B * tq, Wd), BF16),
        compiler_params=_cparams("parallel"),
        name="cross_attention_sample",
    )(q, mem_k, mem_v)


def _lane_pad(a, width):
    return jnp.pad(a, [(0, 0)] * (a.ndim - 1) + [(0, width - a.shape[-1])])


def _row_pad(a, rows):
    return jnp.pad(a, [(0, rows - a.shape[0])] + [(0, 0)] * (a.ndim - 1))


def kernel(x_prompt, mem_prompt, x_sample, state_rwkv, state_shift, state_gdn, state_conv, cache_mem_k, cache_mem_v, w_in, mu_shift, rw_w0, rw_w2, rw_a0, rw_a2, rw_g2, rw_kk, rw_ka, rw_rk, rw_lnx_w, rw_lnx_b, gd_conv_w, gd_a_log, gd_dt_bias, gd_norm_w, w_out, ln1_g, ln1_b, xa_wq, xa_wk, xa_wv, xa_wo, ln2_g, ln2_b, ff_wg, ff_wu, ff_wd, ln3_g, ln3_b):
    depth = w_in.shape[0]
    Bp, Tp, D = x_prompt.shape
    Bs, Ts, _ = x_sample.shape
    Hr, Hg = state_rwkv.shape[2], state_gdn.shape[2]
    W = Hr * RW_HEAD
    assert W == Hg * GD_HEAD and Ts <= SAMPLE_STEPS and Ts >= GD_CONV - 1 and Tp % PROMPT_CHUNK == 0
    n_w, n_a, n_g = rw_w2.shape[1], rw_a2.shape[1], rw_g2.shape[1]
    sw, sa, sg = (_round_up(n, LANES) for n in (n_w, n_a, n_g))
    SW = sw + sa + sg + LANES
    n_mem, xa_heads = cache_mem_k.shape[2], cache_mem_k.shape[3]
    xa_width = xa_heads * cache_mem_k.shape[4]
    alpha = (2 * depth) ** 0.25
    Mp, Ms = Bp * Tp, Bs * SAMPLE_STEPS
    M = Mp + Ms
    tm = _pick_tile(M, (512, 384, 256, 128))
    tf = _pick_tile(ff_wg.shape[-1], (512, 256, 128))
    c_w, c_a, c_g = 3 * W, 3 * W + n_w, 3 * W + n_w + n_a
    c_gd = c_g + n_g

    def to_layout(t):
        parts = [t[..., :c_w], t[..., c_gd:c_gd + 4 * W],
                 _lane_pad(t[..., c_w:c_a], sw), _lane_pad(t[..., c_a:c_g], sa), _lane_pad(t[..., c_g:c_gd], sg),
                 _lane_pad(t[..., c_gd + 4 * W:], LANES)]
        return jnp.concatenate(parts, axis=-1)

    def rwkv_layout(t):
        parts = [t[..., :c_w], _lane_pad(t[..., c_w:c_a], sw), _lane_pad(t[..., c_a:c_g], sa),
                 _lane_pad(t[..., c_g:c_gd], sg + LANES)]
        return jnp.concatenate(parts, axis=-1)

    def rwkv_cols(t):
        s0 = 3 * W
        return jnp.concatenate([t[..., :3 * W], t[..., s0:s0 + n_w], t[..., s0 + sw:s0 + sw + n_a],
                                t[..., s0 + sw + sa:s0 + sw + sa + n_g]], axis=-1)

    w_proj = to_layout(w_in.astype(BF16))
    w_kv = jnp.concatenate([xa_wk, xa_wv], axis=2).astype(BF16)
    w_o, w_q, w_xo = w_out.astype(BF16), xa_wq.astype(BF16), xa_wo.astype(BF16)
    w_g, w_u, w_d = ff_wg.astype(BF16), ff_wu.astype(BF16), ff_wd.astype(BF16)

    x_s = jnp.pad(x_sample, ((0, 0), (0, SAMPLE_STEPS - Ts), (0, 0)))
    x = jnp.concatenate([x_prompt.reshape(Mp, D), x_s.reshape(Ms, D)], axis=0)
    xb = x.astype(BF16)
    mem = mem_prompt.reshape(Bp * n_mem, D).astype(BF16)
    mem_k = cache_mem_k.reshape(depth, Bs, n_mem * xa_heads, -1)
    mem_v = cache_mem_v.reshape(depth, Bs, n_mem * xa_heads, -1)
    zeros = lambda *s: jnp.zeros(s, F32)
    tq = _pick_tile(Tp, (512, 256, 128))
    rw_p, gd_p = zeros(depth, Bp, Hr, RW_HEAD, RW_HEAD), zeros(depth, Bp, Hg, GD_HEAD, GD_HEAD)
    rw_s, gd_s = zeros(*state_rwkv.shape), zeros(*state_gdn.shape)
    rw_p0, gd_p0 = zeros(1, Bp, Hr, RW_HEAD, RW_HEAD), zeros(1, Bp, Hg, GD_HEAD, GD_HEAD)
    conv_at = SUBLANES - (SAMPLE_STEPS - Ts) - (GD_CONV - 1)

    outs = {n: [] for n in ('p_sh', 'p_cv', 'p_mk', 'p_mv', 's_sh', 's_cv')}
    for l in range(depth):
        rw = {'T': None, 'mu': rwkv_layout(mu_shift[l])[None],
              'vec': jnp.stack([rw_w0[l], rw_a0[l], rw_kk[l], rw_ka[l], rw_rk[l].reshape(W), rw_lnx_w[l],
                                rw_lnx_b[l], jnp.zeros((W,), F32)]),
              'w2': _row_pad(rw_w2[l], sw).astype(BF16), 'a2': _row_pad(rw_a2[l], sa).astype(BF16),
              'g2': _row_pad(rw_g2[l], sg).astype(BF16)}
        gd = {'T': None, 'SW': SW, 'conv_w': gd_conv_w[l], 'gate': jnp.stack([gd_a_log[l], gd_dt_bias[l]]),
              'norm_w': gd_norm_w[l][None]}

        kv = matmul(mem, w_kv, l, tm=_pick_tile(Bp * n_mem, (512, 256, 128)),
                    tn=_pick_tile(2 * xa_width, (512, 256, 128)))
        P = matmul(xb, w_proj, l, tm=_pick_tile(M, (1024, 512, 384, 256, 128)),
                   tn=_pick_tile(7 * W + SW, (768, 512, 256, 128)))

        prev_s = rwkv_layout(state_shift[l])[:, None]
        conv_s = jnp.pad(state_conv[l], ((0, 0), (SUBLANES - (GD_CONV - 1), 0), (0, 0)))
        o_rw_p, rw_p, sh_p = rwkv7_mixer(P, 0, zeros(Bp, 1, 3 * W + SW), rw_p0, 0, rw_p, l, dict(rw, T=Tp),
                                         chunk=PROMPT_CHUNK, seqs=1, chunks=RW_PROMPT_CHUNKS,
                                         t_valid=PROMPT_CHUNK)
        o_gd_p, gd_p, cv_p = gdn_mixer(P, 0, zeros(Bp, SUBLANES, 3 * W), gd_p0, 0, gd_p, l, dict(gd, T=Tp),
                                       chunk=PROMPT_CHUNK, seqs=1, chunks=GD_PROMPT_CHUNKS, t_valid=PROMPT_CHUNK)
        o_rw_s, rw_s, sh_s = rwkv7_mixer(P, Mp, prev_s, state_rwkv, l, rw_s, l, dict(rw, T=SAMPLE_STEPS),
                                         chunk=SAMPLE_STEPS, seqs=SAMPLE_SEQS, chunks=1, t_valid=Ts)
        o_gd_s, gd_s, cv_s = gdn_mixer(P, Mp, conv_s, state_gdn, l, gd_s, l, dict(gd, T=SAMPLE_STEPS),
                                       chunk=SAMPLE_STEPS, seqs=SAMPLE_SEQS, chunks=1, t_valid=Ts)

        x, q = mix_out_ln_q(jnp.concatenate([o_rw_p, o_rw_s], axis=0), jnp.concatenate([o_gd_p, o_gd_s], axis=0),
                            w_o, w_q, l, x, ln1_g[l], ln1_b[l], alpha=alpha, tm=tm)

        att_p = cross_attention_prompt(q, kv, n_mem=n_mem, heads=xa_heads, tq=tq, q_tiles=Tp // tq)
        att_s = cross_attention_sample(q, Mp, mem_k, mem_v, l, heads=xa_heads, tq=SAMPLE_STEPS, seqs=SAMPLE_SEQS)
        x, xb = matmul_ln(jnp.concatenate([att_p, att_s], axis=0), w_xo, l, x, ln2_g[l], ln2_b[l],
                          alpha=alpha, tm=tm)
        x, xb = ffn_ln(xb, w_g, w_u, w_d, l, x, ln3_g[l], ln3_b[l], alpha=alpha, tm=tm, tf=tf)

        outs['p_sh'].append(rwkv_cols(sh_p[:, 0])); outs['s_sh'].append(rwkv_cols(sh_s[:, 0]))
        outs['p_cv'].append(cv_p[:, SUBLANES - (GD_CONV - 1):])
        outs['s_cv'].append(cv_s[:, conv_at:conv_at + GD_CONV - 1])
        outs['p_mk'].append(kv[:, :xa_width].reshape(Bp, n_mem, xa_heads, -1))
        outs['p_mv'].append(kv[:, xa_width:].reshape(Bp, n_mem, xa_heads, -1))

    st = {n: jnp.stack(v) for n, v in outs.items()}
    return (x[:Mp].reshape(Bp, Tp, D), x[Mp:].reshape(Bs, SAMPLE_STEPS, D)[:, :Ts], rw_p, st['p_sh'],
            gd_p, st['p_cv'], st['p_mk'], st['p_mv'], rw_s, st['s_sh'], gd_s, st['s_cv'])
```

```python
import functools

import jax
import jax.numpy as jnp
from jax import lax
from jax.experimental import pallas as pl
from jax.experimental.pallas import tpu as pltpu

F32 = jnp.float32
BF16 = jnp.bfloat16

LANES = 128
SUBLANES = 8
RW_HEAD = 64
GD_HEAD = 128
GD_CONV = 4
LN_EPS = 1e-5
RW_GN_EPS = 64e-5
GD_NORM_EPS = 1e-6
L2_EPS = 1e-12

PROMPT_CHUNK = 64
SAMPLE_STEPS = SUBLANES
SAMPLE_SEQS = 8
RW_PROMPT_CHUNKS = 2
GD_PROMPT_CHUNKS = 4

VMEM_LIMIT_BYTES = 56 * 1024 * 1024

_NN = (((1,), (0,)), ((), ()))
_NT = (((1,), (1,)), ((), ()))
_TN = (((0,), (0,)), ((), ()))
_TT = (((0,), (1,)), ((), ()))


def _cparams(*sem):
    return pltpu.CompilerParams(dimension_semantics=sem, vmem_limit_bytes=VMEM_LIMIT_BYTES)


def _round_up(n, m):
    return -(-n // m) * m


def _pick_tile(n, candidates):
    for c in candidates:
        if n % c == 0:
            return c
    raise ValueError(f"no tile in {candidates} divides {n}")


def _mm_kernel(x_ref, w_ref, o_ref):
    o_ref[...] = jnp.dot(x_ref[...], w_ref[...], preferred_element_type=F32).astype(o_ref.dtype)


def matmul(x, w, layer, *, tm, tn, out_dtype=F32):
    M, K = x.shape
    N = w.shape[2]
    assert M % tm == 0 and N % tn == 0
    return pl.pallas_call(
        _mm_kernel,
        grid=(M // tm, N // tn),
        in_specs=[pl.BlockSpec((tm, K), lambda i, j: (i, 0)),
                  pl.BlockSpec((None, K, tn), lambda i, j: (layer, 0, j))],
        out_specs=pl.BlockSpec((tm, tn), lambda i, j: (i, j)),
        out_shape=jax.ShapeDtypeStruct((M, N), out_dtype),
        compiler_params=_cparams("parallel", "arbitrary"),
        name="matmul",
    )(x, w)


def _layer_norm_rows(y, g, b):
    mu = jnp.mean(y, -1, keepdims=True)
    yc = y - mu
    var = jnp.mean(yc * yc, -1, keepdims=True)
    return yc * lax.rsqrt(var + LN_EPS) * g + b


def _group_rows(p_ref, s_ref, n_p):
    return jnp.where(pl.program_id(0) < n_p, p_ref[...], s_ref[...])


def _group_specs(tm, width, n_p):
    return (pl.BlockSpec((tm, width), lambda i, *_: (jnp.minimum(i, n_p - 1), 0)),
            pl.BlockSpec((tm, width), lambda i, *_: (jnp.maximum(i - n_p, 0), 0)))


def _mix_out_kernel(ap_ref, as_ref, bp_ref, bs_ref, wa_ref, wb_ref, res_ref, g_ref, beta_ref, wq_ref, x_ref, q_ref,
                    *, alpha, n_p):
    h = jnp.dot(_group_rows(ap_ref, as_ref, n_p), wa_ref[...], preferred_element_type=F32)
    h += jnp.dot(_group_rows(bp_ref, bs_ref, n_p), wb_ref[...], preferred_element_type=F32)
    x = _layer_norm_rows(alpha * res_ref[...] + h, g_ref[...], beta_ref[...])
    x_ref[...] = x
    q_ref[...] = jnp.dot(x.astype(BF16), wq_ref[...], preferred_element_type=F32).astype(BF16)


def mix_out_ln_q(o_a, o_b, w, wq, layer, res, g, b, *, alpha, tm):
    M, D = res.shape
    Ka, Kb = o_a[0].shape[1], o_b[0].shape[1]
    Q = wq.shape[2]
    n_p = o_a[0].shape[0] // tm
    assert M % tm == 0 and o_a[0].shape[0] % tm == 0 and o_a[1].shape[0] % tm == 0
    assert Ka == Kb and w.shape[1] == Ka + Kb and o_a[0].shape[0] + o_a[1].shape[0] == M
    rows = lambda n: pl.BlockSpec((tm, n), lambda i: (i, 0))
    vec = pl.BlockSpec((1, D), lambda i: (0, 0))
    return pl.pallas_call(
        functools.partial(_mix_out_kernel, alpha=alpha, n_p=n_p),
        grid=(M // tm,),
        in_specs=[*_group_specs(tm, Ka, n_p), *_group_specs(tm, Kb, n_p),
                  pl.BlockSpec((None, Ka, D), lambda i: (layer, 0, 0)),
                  pl.BlockSpec((None, Kb, D), lambda i: (layer, 1, 0)),
                  rows(D), vec, vec,
                  pl.BlockSpec((None, D, Q), lambda i: (layer, 0, 0))],
        out_specs=[rows(D), rows(Q)],
        out_shape=[jax.ShapeDtypeStruct((M, D), F32), jax.ShapeDtypeStruct((M, Q), BF16)],
        compiler_params=_cparams("parallel"),
        name="mix_out_ln_q",
    )(*o_a, *o_b, w, w, res, g.reshape(1, D), b.reshape(1, D), wq)


def _attn_ffn_kernel(ap_ref, as_ref, wo_ref, res_ref, g2_ref, b2_ref, wg_ref, wu_ref, wd_ref, g3_ref, b3_ref,
                     o_ref, ob_ref, x_scr, xb_scr, acc_ref, *, alpha, n_p):
    f = pl.program_id(1)

    @pl.when(f == 0)
    def _():
        h = jnp.dot(_group_rows(ap_ref, as_ref, n_p), wo_ref[...], preferred_element_type=F32)
        x = _layer_norm_rows(alpha * res_ref[...] + h, g2_ref[...], b2_ref[...])
        x_scr[...] = x
        xb_scr[...] = x.astype(BF16)
        acc_ref[...] = jnp.zeros_like(acc_ref)

    x = xb_scr[...]
    gate = jnp.dot(x, wg_ref[...], preferred_element_type=F32)
    up = jnp.dot(x, wu_ref[...], preferred_element_type=F32)
    hidden = (gate * jax.nn.sigmoid(gate) * up).astype(BF16)
    acc_ref[...] += jnp.dot(hidden, wd_ref[...], preferred_element_type=F32)

    @pl.when(f == pl.num_programs(1) - 1)
    def _():
        y = _layer_norm_rows(alpha * x_scr[...] + acc_ref[...], g3_ref[...], b3_ref[...])
        o_ref[...] = y
        ob_ref[...] = y.astype(BF16)


def attn_out_ffn(att, wo, wg, wu, wd, layer, res, g2, b2, g3, b3, *, alpha, tm, tf):
    M, D = res.shape
    Q = att[0].shape[1]
    Fd = wg.shape[2]
    n_p = att[0].shape[0] // tm
    assert M % tm == 0 and Fd % tf == 0 and att[0].shape[0] % tm == 0 and att[0].shape[0] + att[1].shape[0] == M
    vec = pl.BlockSpec((1, D), lambda i, f: (0, 0))
    rows = pl.BlockSpec((tm, D), lambda i, f: (i, 0))
    return pl.pallas_call(
        functools.partial(_attn_ffn_kernel, alpha=alpha, n_p=n_p),
        grid=(M // tm, Fd // tf),
        in_specs=[*_group_specs(tm, Q, n_p),
                  pl.BlockSpec((None, Q, D), lambda i, f: (layer, 0, 0)), rows, vec, vec,
                  pl.BlockSpec((None, D, tf), lambda i, f: (layer, 0, f)),
                  pl.BlockSpec((None, D, tf), lambda i, f: (layer, 0, f)),
                  pl.BlockSpec((None, tf, D), lambda i, f: (layer, f, 0)), vec, vec],
        out_specs=[rows, rows],
        out_shape=[jax.ShapeDtypeStruct((M, D), F32), jax.ShapeDtypeStruct((M, D), BF16)],
        scratch_shapes=[pltpu.VMEM((tm, D), F32), pltpu.VMEM((tm, D), BF16), pltpu.VMEM((tm, D), F32)],
        compiler_params=_cparams("parallel", "arbitrary"),
        name="attn_out_ffn",
    )(*att, wo, res, g2.reshape(1, D), b2.reshape(1, D), wg, wu, wd, g3.reshape(1, D), b3.reshape(1, D))


def _mm(a, b, dims):
    return lax.dot_general(a.astype(BF16), b.astype(BF16), dims, preferred_element_type=F32)


def _three_bf16_terms(a):
    a1 = a.astype(BF16)
    r1 = a - a1.astype(F32)
    a2 = r1.astype(BF16)
    return a1, a2, (r1 - a2.astype(F32)).astype(BF16)


def _unit_lower_inverse(xs, c):
    row = lax.broadcasted_iota(jnp.int32, (c, c), 0)
    col = lax.broadcasted_iota(jnp.int32, (c, c), 1)
    eye = jnp.where(row == col, 1.0, 0.0).astype(F32)
    ts = [eye + x for x in xs]
    ps = list(xs)
    span = 2
    while span < c:
        ps = [_mm(p, p, _NN) for p in ps]
        ts = [t + _mm(t, p, _NN) for t, p in zip(ts, ps)]
        span *= 2
    return ts


def _step_in_seq(R, C):
    return lax.broadcasted_iota(jnp.int32, (R, 1), 0) & (C - 1)


def _seq_tri(R, C):
    row = lax.broadcasted_iota(jnp.int32, (R, R), 0)
    col = lax.broadcasted_iota(jnp.int32, (R, R), 1)
    return jnp.where(row >= col, jnp.where((row & -C) == (col & -C), 1.0, 0.0), 0.0).astype(BF16)


def _per_seq_rows(rows, C):
    rows = [jnp.broadcast_to(t, (C, t.shape[-1])) for t in rows]
    return rows[0] if len(rows) == 1 else jnp.concatenate(rows, axis=0)


def _delayed_rows(x, tails, j, BB, C):
    xj = pltpu.roll(x, j, 0)
    row = lax.broadcasted_iota(jnp.int32, (SUBLANES, 1), 0)
    pieces = []
    for i in range(BB):
        pieces.append(jnp.where(row < j, pltpu.roll(tails[i], j, 0), xj[i * C:i * C + SUBLANES]))
        if C > SUBLANES:
            pieces.append(xj[i * C + SUBLANES:(i + 1) * C])
    return pieces[0] if len(pieces) == 1 else jnp.concatenate(pieces, axis=0)


def _half_sums(x, N):
    low = lax.broadcasted_iota(jnp.int32, (1, LANES), 1) < N
    out = []
    for j in range(x.shape[1] // LANES):
        xv = x[:, j * LANES:(j + 1) * LANES]
        lo = jnp.sum(jnp.where(low, xv, 0.0), -1, keepdims=True)
        out += [lo, jnp.sum(jnp.where(low, 0.0, xv), -1, keepdims=True)]
    return out


def _half_bcast(cols, N):
    low = lax.broadcasted_iota(jnp.int32, (1, LANES), 1) < N
    return jnp.concatenate([jnp.where(low, cols[2 * j], cols[2 * j + 1]) for j in range(len(cols) // 2)], axis=1)


def _rwkv_kernel(r_ref, k_ref, v_ref, sm_ref, prev_ref, mu_ref, vec_ref, w2_ref, a2_ref, g2_ref,
                 s0_ref, st_buf_ref, o_ref, st_ref, last_ref, s_scr, tail_scr, o_scr, *, C, H, N, NS, L, t_valid):
    c = pl.program_id(1)
    W = H * N
    BB = NS * L
    R = BB * C
    Rs = L * C
    sw, sa, sg = w2_ref.shape[0], a2_ref.shape[0], g2_ref.shape[0]

    @pl.when(c == 0)
    def _():
        s_scr[...] = s0_ref[...]
        tail_scr[...] = prev_ref[...]

    first = _step_in_seq(R, Rs) == 0
    vec = vec_ref[...]
    w0, a0, k_k, k_a, r_k, ln_w, ln_b = (vec[j:j + 1] for j in range(7))
    mu = mu_ref[...]
    tail = _per_seq_rows([tail_scr[s] for s in range(NS)], Rs)

    def shifted(x, lo):
        hi = lo + x.shape[1]
        prev = jnp.where(first, tail[:, lo:hi], pltpu.roll(x, 1, 0))
        return x + (prev - x) * mu[:, lo:hi]

    xr, xk, xv, xs = r_ref[...], k_ref[...], v_ref[...], sm_ref[...]
    r, k, v, ms = shifted(xr, 0), shifted(xk, W), shifted(xv, 2 * W), shifted(xs, 3 * W)
    for x, lo in ((xr, 0), (xk, W), (xv, 2 * W), (xs, 3 * W)):
        for s in range(NS):
            end = s * Rs + (L - 1) * C + t_valid
            tail_scr[s, :, lo:lo + x.shape[1]] = x[end - 1:end]

    lora_w = jnp.dot(jnp.tanh(ms[:, :sw]).astype(BF16), w2_ref[...], preferred_element_type=F32)
    lora_a = jnp.dot(ms[:, sw:sw + sa].astype(BF16), a2_ref[...], preferred_element_type=F32)
    gate = jnp.dot(jax.nn.sigmoid(ms[:, sw + sa:sw + sa + sg]).astype(BF16), g2_ref[...],
                   preferred_element_type=F32)
    lw = -jnp.exp(-jax.nn.softplus(-(w0 + lora_w)) - 0.5)
    a = jax.nn.sigmoid(a0 + lora_a)
    kk = k * k_k
    kk = kk * _half_bcast([lax.rsqrt(s + L2_EPS) for s in _half_sums(kk * kk, N)], N)
    k = k * (1.0 + (a - 1.0) * k_a)
    if t_valid < C:
        valid = _step_in_seq(R, C) < t_valid
        r, k, v, kk, lw = (jnp.where(valid, t, 0.0) for t in (r, k, v, kk, lw))

    tri = _seq_tri(R, C)
    G = sum(jnp.dot(tri, t, preferred_element_type=F32) for t in _three_bf16_terms(lw))
    mid = C // 2 - 1
    g_mid = [G[i * C + mid:i * C + mid + 1] for i in range(BB)]
    g_last = [G[(i + 1) * C - 1:(i + 1) * C] for i in range(BB)]
    Gm = G - _per_seq_rows(g_mid, C)
    e_out = jnp.exp(-Gm)
    e_tail = jnp.exp(_per_seq_rows(g_last, C) - G)
    b_in = kk * a
    A_, R_ = -kk * jnp.exp(Gm - lw), r * jnp.exp(Gm)
    B_, K_ = b_in * e_out, k * e_out
    Bt, Kt = b_in * e_tail, k * e_tail
    e_mid, e_last = [jnp.exp(t) for t in g_mid], [jnp.exp(t) for t in g_last]

    row2 = lax.broadcasted_iota(jnp.int32, (2 * C, 2 * C), 0)
    col2 = lax.broadcasted_iota(jnp.int32, (2 * C, 2 * C), 1)
    rowc = jnp.where(row2 >= C, row2 - C, row2)
    colc = jnp.where(col2 >= C, col2 - C, col2)
    keep = jnp.where(row2 >= C, rowc, rowc - 1) >= colc

    rs = lambda i: slice(i * C, (i + 1) * C)
    sl = lambda h: slice(h * N, (h + 1) * N)
    pair = lambda x, y, i, h: jnp.concatenate([x[rs(i), sl(h)], y[rs(i), sl(h)]], axis=0)
    every = [(i, h) for i in range(BB) for h in range(H)]
    AR = {ih: pair(A_, R_, *ih) for ih in every}
    M = {ih: jnp.where(keep, _mm(AR[ih], pair(B_, K_, *ih), _NT), 0.0) for ih in every}
    AV = {(i, h): _mm(M[i, h][:C, C:], v[rs(i), sl(h)], _NN) for i, h in every}
    T = dict(zip(every, _unit_lower_inverse([M[ih][:C, :C] for ih in every], C)))
    S = {(s, h): s_scr[s, h] for s in range(NS) for h in range(H)}
    for lvl in range(L):
        ch = [(s, s * L + lvl, h) for s in range(NS) for h in range(H)]
        SR = {(i, h): _mm(AR[i, h], S[s, h] * e_mid[i][:, sl(h)], _NT) for s, i, h in ch}
        U = {(i, h): _mm(T[i, h], SR[i, h][:C] + AV[i, h], _NN) for _, i, h in ch}
        UV = {(i, h): jnp.concatenate([U[i, h], v[rs(i), sl(h)]], axis=0) for _, i, h in ch}
        O = {(i, h): SR[i, h][C:] + _mm(M[i, h][C:], UV[i, h], _NN) for _, i, h in ch}
        S = {(s, h): S[s, h] * e_last[i][:, sl(h)] + _mm(UV[i, h], pair(Bt, Kt, i, h), _TN) for s, i, h in ch}
        for _, i, h in ch:
            o_scr[rs(i), sl(h)] = O[i, h]
    for (s, h), val in S.items():
        s_scr[s, h] = val

    o = o_scr[...]
    o = o - _half_bcast([s * (1.0 / N) for s in _half_sums(o, N)], N)
    o = o * _half_bcast([lax.rsqrt(s * (1.0 / N) + RW_GN_EPS) for s in _half_sums(o * o, N)], N) * ln_w + ln_b
    bonus = _half_bcast(_half_sums(r * k * r_k, N), N) * v
    o_ref[...] = ((o + bonus) * gate).astype(o_ref.dtype)

    @pl.when(c == pl.num_programs(1) - 1)
    def _():
        st_ref[...] = s_scr[...]
        last_ref[...] = tail_scr[...]


def _state_specs(states, s0_layer, layer, seqs):
    blk = (None, seqs) + states.shape[2:]
    return (pl.BlockSpec(blk, lambda i, c: (s0_layer, i, 0, 0, 0)),
            pl.BlockSpec(blk, lambda i, c: (layer, i, 0, 0, 0)))


def rwkv7_mixer(P, row0, prev, s0, s0_layer, st_buf, layer, prm, *, chunk, seqs, chunks, t_valid):
    _, B, H, N, _ = s0.shape
    W = H * N
    SW = prm['mu'].shape[-1] - 3 * W
    T = prm['T']
    blk = seqs * chunks * chunk
    nc = T // (chunks * chunk)
    assert T % (chunks * chunk) == 0 and B % seqs == 0 and row0 % blk == 0 and (7 * W) % SW == 0
    assert (seqs == 1 or nc == 1) and (chunks == 1 or t_valid == chunk)
    assert 2 * N == LANES and chunk & (chunk - 1) == 0 and st_buf.shape[1:] == s0.shape[1:]
    rb0 = row0 // blk
    col = lambda j: pl.BlockSpec((blk, W), lambda i, c: (rb0 + i * nc + c, j))
    full = lambda a: pl.BlockSpec((None,) + a.shape[1:], lambda i, c: (layer,) + (0,) * (a.ndim - 1))
    st_in, st_out = _state_specs(s0, s0_layer, layer, seqs)
    row = pl.BlockSpec((seqs, 1, 3 * W + SW), lambda i, c: (i, 0, 0))
    consts = [prm['mu'], prm['vec'], prm['w2'], prm['a2'], prm['g2']]
    return pl.pallas_call(
        functools.partial(_rwkv_kernel, C=chunk, H=H, N=N, NS=seqs, L=chunks, t_valid=t_valid),
        grid=(B // seqs, nc),
        in_specs=[col(0), col(1), col(2),
                  pl.BlockSpec((blk, SW), lambda i, c: (rb0 + i * nc + c, 7 * W // SW)),
                  pl.BlockSpec((None, seqs, 1, 3 * W + SW), lambda i, c: (s0_layer, i, 0, 0))]
                 + [full(a) for a in consts] + [st_in, pl.BlockSpec(memory_space=pl.ANY)],
        out_specs=[pl.BlockSpec((blk, W), lambda i, c: (i * nc + c, 0)), st_out, row],
        out_shape=[jax.ShapeDtypeStruct((B * T, W), BF16), jax.ShapeDtypeStruct(st_buf.shape, F32),
                   jax.ShapeDtypeStruct((B, 1, 3 * W + SW), F32)],
        scratch_shapes=[pltpu.VMEM((seqs, H, N, N), F32), pltpu.VMEM((seqs, 1, 3 * W + SW), F32),
                        pltpu.VMEM((blk, W), F32)],
        input_output_aliases={5 + len(consts) + 1: 1},
        compiler_params=_cparams("parallel", "arbitrary"),
        name="rwkv7_mixer",
    )(P, P, P, P, prev, *consts, s0, st_buf)


def _gdn_kernel(q_ref, k_ref, v_ref, z_ref, sm_ref, cprev_ref, cw_ref, gate_ref, nw_ref, s0_ref, st_buf_ref,
                o_ref, st_ref, last_ref, s_scr, tail_scr, *, C, H, N, NS, L, t_valid, gate_lane):
    c = pl.program_id(1)
    W = H * N
    BB = NS * L
    R = BB * C
    Rs = L * C

    @pl.when(c == 0)
    def _():
        s_scr[...] = s0_ref[...]
        tail_scr[...] = cprev_ref[...]

    cw = cw_ref[...]
    nw = nw_ref[...]
    tails = [tail_scr[s] for s in range(NS)]

    def conv_silu(x, lo):
        hi = lo + W
        acc = x * cw[GD_CONV - 1:GD_CONV, lo:hi]
        for j in range(1, GD_CONV):
            acc = acc + (_delayed_rows(x, [t[:, lo:hi] for t in tails], j, NS, Rs)
                         * cw[GD_CONV - 1 - j:GD_CONV - j, lo:hi])
        return acc * jax.nn.sigmoid(acc)

    xq, xk, xv = q_ref[...], k_ref[...], v_ref[...]
    q_all, k_all, v_all = conv_silu(xq, 0), conv_silu(xk, W), conv_silu(xv, 2 * W)
    for x, lo in ((xq, 0), (xk, W), (xv, 2 * W)):
        for s in range(NS):
            tail_scr[s, :, lo:lo + W] = x[(s + 1) * Rs - SUBLANES:(s + 1) * Rs]
    z_all = z_ref[...]
    gates = sm_ref[:, gate_lane:gate_lane + LANES]
    beta = jax.nn.sigmoid(gates[:, :H])
    g = -jnp.exp(gate_ref[0:1, :]) * jax.nn.softplus(gates[:, H:2 * H] + gate_ref[1:2, :])
    if t_valid < C:
        valid = _step_in_seq(R, C) < t_valid
        beta, g, q_all, k_all, v_all = (jnp.where(valid, t, 0.0) for t in (beta, g, q_all, k_all, v_all))
    tri = _seq_tri(R, C)
    g_terms = _three_bf16_terms(g)
    Gc = sum(jnp.dot(tri, t, preferred_element_type=F32) for t in g_terms)
    Gr = sum(lax.dot_general(t, tri, _TT, preferred_element_type=F32) for t in g_terms)

    row2 = lax.broadcasted_iota(jnp.int32, (2 * C, C), 0)
    col2 = lax.broadcasted_iota(jnp.int32, (2 * C, C), 1)
    keep = jnp.where(row2 >= C, row2 - C, row2 - 1) >= col2

    l2 = lambda t: t * lax.rsqrt(jnp.sum(t * t, -1, keepdims=True) + L2_EPS)
    sl = lambda h: slice(h * N, (h + 1) * N)
    qn = [l2(q_all[:, sl(h)]) * (N ** -0.5) for h in range(H)]
    kn = [l2(k_all[:, sl(h)]) for h in range(H)]

    rs = lambda i: slice(i * C, (i + 1) * C)
    every = [(i, h) for i in range(BB) for h in range(H)]
    gc = {(i, h): Gc[rs(i), h:h + 1] for i, h in every}
    g_last = {(i, h): Gc[(i + 1) * C - 1:(i + 1) * C, h:h + 1] for i, h in every}
    gc2 = {ih: jnp.concatenate([gc[ih], gc[ih]], axis=0) for ih in every}
    decay = {(i, h): jnp.where(keep, jnp.exp(jnp.where(keep, gc2[i, h] - Gr[h:h + 1, rs(i)], 0.0)), 0.0)
             for i, h in every}
    bh = {(i, h): beta[rs(i), h:h + 1] for i, h in every}
    kj = {(i, h): kn[h][rs(i)] for i, h in every}
    KQ = {(i, h): jnp.concatenate([kj[i, h] * bh[i, h], qn[h][rs(i)]], axis=0) for i, h in every}
    M = {ih: _mm(KQ[ih], kj[ih], _NT) * decay[ih] for ih in every}
    T = dict(zip(every, _unit_lower_inverse([-M[ih][:C] for ih in every], C)))
    S = {(s, h): s_scr[s, h] for s in range(NS) for h in range(H)}
    for lvl in range(L):
        ch = [(s, s * L + lvl, h) for s in range(NS) for h in range(H)]
        SR = {(i, h): _mm(KQ[i, h] * jnp.exp(gc2[i, h]), S[s, h], _NN) for s, i, h in ch}
        v_new = {(i, h): _mm(T[i, h], v_all[rs(i), sl(h)] * bh[i, h] - SR[i, h][:C], _NN) for _, i, h in ch}
        O = {(i, h): SR[i, h][C:] + _mm(M[i, h][C:], v_new[i, h], _NN) for _, i, h in ch}
        S = {(s, h): S[s, h] * jnp.exp(g_last[i, h])
             + _mm(kj[i, h] * jnp.exp(g_last[i, h] - gc[i, h]), v_new[i, h], _TN) for s, i, h in ch}
        for _, i, h in ch:
            zh = z_all[rs(i), sl(h)]
            o = O[i, h] * lax.rsqrt(jnp.mean(O[i, h] * O[i, h], -1, keepdims=True) + GD_NORM_EPS) * nw
            o_ref[rs(i), sl(h)] = (o * (zh * jax.nn.sigmoid(zh))).astype(o_ref.dtype)
    for (s, h), val in S.items():
        s_scr[s, h] = val

    @pl.when(c == pl.num_programs(1) - 1)
    def _():
        st_ref[...] = s_scr[...]
        last_ref[...] = tail_scr[...]


def gdn_mixer(P, row0, conv_prev, s0, s0_layer, st_buf, layer, prm, *, chunk, seqs, chunks, t_valid):
    _, B, H, N, _ = s0.shape
    W = H * N
    SW = prm['SW']
    T = prm['T']
    blk = seqs * chunks * chunk
    nc = T // (chunks * chunk)
    assert T % (chunks * chunk) == 0 and B % seqs == 0 and row0 % blk == 0 and chunk >= SUBLANES
    assert (seqs == 1 or nc == 1) and (chunks == 1 or t_valid == chunk) and 2 * H <= LANES
    assert N == LANES and chunk & (chunk - 1) == 0 and st_buf.shape[1:] == s0.shape[1:]
    rb0 = row0 // blk
    col = lambda j: pl.BlockSpec((blk, W), lambda i, c: (rb0 + i * nc + c, j))
    full = lambda a: pl.BlockSpec((None,) + a.shape[1:], lambda i, c: (layer,) + (0,) * (a.ndim - 1))
    st_in, st_out = _state_specs(s0, s0_layer, layer, seqs)
    rows = pl.BlockSpec((seqs, SUBLANES, 3 * W), lambda i, c: (i, 0, 0))
    consts = [prm['conv_w'], prm['gate'], prm['norm_w']]
    return pl.pallas_call(
        functools.partial(_gdn_kernel, C=chunk, H=H, N=N, NS=seqs, L=chunks, t_valid=t_valid,
                          gate_lane=SW - LANES),
        grid=(B // seqs, nc),
        in_specs=[col(3), col(4), col(5), col(6),
                  pl.BlockSpec((blk, SW), lambda i, c: (rb0 + i * nc + c, 7 * W // SW)),
                  pl.BlockSpec((None, seqs, SUBLANES, 3 * W), lambda i, c: (s0_layer, i, 0, 0))]
                 + [full(a) for a in consts] + [st_in, pl.BlockSpec(memory_space=pl.ANY)],
        out_specs=[pl.BlockSpec((blk, W), lambda i, c: (i * nc + c, 0)), st_out, rows],
        out_shape=[jax.ShapeDtypeStruct((B * T, W), BF16), jax.ShapeDtypeStruct(st_buf.shape, F32),
                   jax.ShapeDtypeStruct((B, SUBLANES, 3 * W), F32)],
        scratch_shapes=[pltpu.VMEM((seqs, H, N, N), F32), pltpu.VMEM((seqs, SUBLANES, 3 * W), F32)],
        input_output_aliases={6 + len(consts) + 1: 1},
        compiler_params=_cparams("parallel", "arbitrary"),
        name="gdn_mixer",
    )(P, P, P, P, P, conv_prev, *consts, s0, st_buf)


def _xattn_kernel(q_ref, k_ref, v_ref, o_ref, *, H, Dh, BB, tq, head_rows):
    scale = Dh ** -0.5
    pairs = [(b, h) for b in range(BB) for h in range(H)]
    sl = [slice(h * Dh, (h + 1) * Dh) for h in range(H)]

    def head(ref, b, h):
        if head_rows:
            return ref[b, pl.ds(h, ref.shape[1] // H, stride=H), :].astype(BF16)
        return ref[:, sl[h]].astype(BF16)

    q = [q_ref[b * tq:(b + 1) * tq, :] for b in range(BB)]
    s = [lax.dot_general(q[b][:, sl[h]], head(k_ref, b, h), _NT, preferred_element_type=F32) * scale
         for b, h in pairs]
    e = [jnp.exp(t - jnp.max(t, -1, keepdims=True)) for t in s]
    pr = [(t / jnp.sum(t, -1, keepdims=True)).astype(BF16) for t in e]
    o = [jnp.dot(pr[j], head(v_ref, b, h), preferred_element_type=F32) for j, (b, h) in enumerate(pairs)]
    for j, (b, h) in enumerate(pairs):
        o_ref[b * tq:(b + 1) * tq, sl[h]] = o[j].astype(o_ref.dtype)


def cross_attention_prompt(q, kv, *, n_mem, heads, tq, q_tiles):
    Wd = kv.shape[1] // 2
    B = kv.shape[0] // n_mem
    return pl.pallas_call(
        functools.partial(_xattn_kernel, H=heads, Dh=Wd // heads, BB=1, tq=tq, head_rows=False),
        grid=(B, q_tiles),
        in_specs=[pl.BlockSpec((tq, Wd), lambda i, t: (i * q_tiles + t, 0)),
                  pl.BlockSpec((n_mem, Wd), lambda i, t: (i, 0)),
                  pl.BlockSpec((n_mem, Wd), lambda i, t: (i, 1))],
        out_specs=pl.BlockSpec((tq, Wd), lambda i, t: (i * q_tiles + t, 0)),
        out_shape=jax.ShapeDtypeStruct((B * q_tiles * tq, Wd), BF16),
        compiler_params=_cparams("parallel", "arbitrary"),
        name="cross_attention_prompt",
    )(q, kv, kv)


def cross_attention_sample(q, row0, mem_k, mem_v, layer, *, heads, tq, seqs):
    _, B, rows, Dh = mem_k.shape
    Wd = heads * Dh
    blk = seqs * tq
    assert B % seqs == 0 and row0 % blk == 0
    rb0 = row0 // blk
    kv = pl.BlockSpec((None, seqs, rows, Dh), lambda i: (layer, i, 0, 0))
    return pl.pallas_call(
        functools.partial(_xattn_kernel, H=heads, Dh=Dh, BB=seqs, tq=tq, head_rows=True),
        grid=(B // seqs,),
        in_specs=[pl.BlockSpec((blk, Wd), lambda i: (rb0 + i, 0)), kv, kv],
        out_specs=pl.BlockSpec((blk, Wd), lambda i: (i, 0)),
        out_shape=jax.ShapeDtypeStruct((B * tq, Wd), BF16),
        compiler_params=_cparams("parallel"),
        name="cross_attention_sample",
    )(q, mem_k, mem_v)


def _lane_pad(a, width):
    return jnp.pad(a, [(0, 0)] * (a.ndim - 1) + [(0, width - a.shape[-1])])


def kernel(x_prompt, mem_prompt, x_sample, state_rwkv, state_shift, state_gdn, state_conv, cache_mem_k, cache_mem_v, w_in, mu_shift, rw_w0, rw_w2, rw_a0, rw_a2, rw_g2, rw_kk, rw_ka, rw_rk, rw_lnx_w, rw_lnx_b, gd_conv_w, gd_a_log, gd_dt_bias, gd_norm_w, w_out, ln1_g, ln1_b, xa_wq, xa_wk, xa_wv, xa_wo, ln2_g, ln2_b, ff_wg, ff_wu, ff_wd, ln3_g, ln3_b):
    depth = w_in.shape[0]
    Bp, Tp, D = x_prompt.shape
    Bs, Ts, _ = x_sample.shape
    Hr, Hg = state_rwkv.shape[2], state_gdn.shape[2]
    W = Hr * RW_HEAD
    assert W == Hg * GD_HEAD and Ts <= SAMPLE_STEPS and Ts >= GD_CONV - 1 and Tp % PROMPT_CHUNK == 0
    n_w, n_a, n_g = rw_w2.shape[1], rw_a2.shape[1], rw_g2.shape[1]
    sw, sa, sg = (_round_up(n, LANES) for n in (n_w, n_a, n_g))
    SW = sw + sa + sg + LANES
    n_mem, xa_heads = cache_mem_k.shape[2], cache_mem_k.shape[3]
    xa_width = xa_heads * cache_mem_k.shape[4]
    alpha = (2 * depth) ** 0.25
    Mp, Ms = Bp * Tp, Bs * SAMPLE_STEPS
    M = Mp + Ms
    tm = _pick_tile(M, (512, 384, 256, 128))
    tf = _pick_tile(ff_wg.shape[-1], (512, 256, 128))
    c_w, c_a, c_g = 3 * W, 3 * W + n_w, 3 * W + n_w + n_a
    c_gd = c_g + n_g

    def to_layout(t):
        parts = [t[..., :c_w], t[..., c_gd:c_gd + 4 * W],
                 _lane_pad(t[..., c_w:c_a], sw), _lane_pad(t[..., c_a:c_g], sa), _lane_pad(t[..., c_g:c_gd], sg),
                 _lane_pad(t[..., c_gd + 4 * W:], LANES)]
        return jnp.concatenate(parts, axis=-1)

    def rwkv_layout(t):
        parts = [t[..., :c_w], _lane_pad(t[..., c_w:c_a], sw), _lane_pad(t[..., c_a:c_g], sa),
                 _lane_pad(t[..., c_g:c_gd], sg + LANES)]
        return jnp.concatenate(parts, axis=-1)

    def rwkv_cols(t):
        s0 = 3 * W
        return jnp.concatenate([t[..., :3 * W], t[..., s0:s0 + n_w], t[..., s0 + sw:s0 + sw + n_a],
                                t[..., s0 + sw + sa:s0 + sw + sa + n_g]], axis=-1)

    w_proj = to_layout(w_in.astype(BF16))
    w_kv = jnp.concatenate([xa_wk, xa_wv], axis=2).astype(BF16)
    w_o, w_q, w_xo = w_out.astype(BF16), xa_wq.astype(BF16), xa_wo.astype(BF16)
    w_g, w_u, w_d = ff_wg.astype(BF16), ff_wu.astype(BF16), ff_wd.astype(BF16)

    x_s = jnp.pad(x_sample, ((0, 0), (0, SAMPLE_STEPS - Ts), (0, 0)))
    x = jnp.concatenate([x_prompt.reshape(Mp, D), x_s.reshape(Ms, D)], axis=0)
    xb = x.astype(BF16)
    mem = mem_prompt.reshape(Bp * n_mem, D).astype(BF16)
    mem_k = cache_mem_k.reshape(depth, Bs, n_mem * xa_heads, -1)
    mem_v = cache_mem_v.reshape(depth, Bs, n_mem * xa_heads, -1)
    zeros = lambda *s: jnp.zeros(s, F32)
    tq = _pick_tile(Tp, (512, 256, 128))
    rw_p, gd_p = zeros(depth, Bp, Hr, RW_HEAD, RW_HEAD), zeros(depth, Bp, Hg, GD_HEAD, GD_HEAD)
    rw_s, gd_s = zeros(*state_rwkv.shape), zeros(*state_gdn.shape)
    rw_p0, gd_p0 = zeros(1, Bp, Hr, RW_HEAD, RW_HEAD), zeros(1, Bp, Hg, GD_HEAD, GD_HEAD)
    conv_at = SUBLANES - (SAMPLE_STEPS - Ts) - (GD_CONV - 1)

    outs = {n: [] for n in ('p_sh', 'p_cv', 'p_mk', 'p_mv', 's_sh', 's_cv')}
    lora = lambda w, rows: jnp.pad(w, ((0, 0), (0, rows - w.shape[1]), (0, 0))).astype(BF16)
    rw = {'T': None, 'mu': rwkv_layout(mu_shift)[:, None],
          'vec': jnp.stack([rw_w0, rw_a0, rw_kk, rw_ka, rw_rk.reshape(depth, W), rw_lnx_w, rw_lnx_b,
                            jnp.zeros((depth, W), F32)], axis=1),
          'w2': lora(rw_w2, sw), 'a2': lora(rw_a2, sa), 'g2': lora(rw_g2, sg)}
    gd = {'T': None, 'SW': SW, 'conv_w': gd_conv_w, 'gate': jnp.stack([gd_a_log, gd_dt_bias], axis=1),
          'norm_w': gd_norm_w[:, None]}
    prev_s = rwkv_layout(state_shift)[:, :, None]
    conv_s = jnp.pad(state_conv, ((0, 0), (0, 0), (SUBLANES - (GD_CONV - 1), 0), (0, 0)))
    prev_p, conv_p = zeros(1, Bp, 1, 3 * W + SW), zeros(1, Bp, SUBLANES, 3 * W)

    for l in range(depth):
        kv = matmul(mem, w_kv, l, tm=_pick_tile(Bp * n_mem, (512, 256, 128)),
                    tn=_pick_tile(2 * xa_width, (512, 256, 128)))
        P = matmul(xb, w_proj, l, tm=_pick_tile(M, (1024, 512, 384, 256, 128)),
                   tn=_pick_tile(7 * W + SW, (768, 512, 256, 128)))

        o_rw_p, rw_p, sh_p = rwkv7_mixer(P, 0, prev_p, rw_p0, 0, rw_p, l, dict(rw, T=Tp),
                                         chunk=PROMPT_CHUNK, seqs=1, chunks=RW_PROMPT_CHUNKS,
                                         t_valid=PROMPT_CHUNK)
        o_gd_p, gd_p, cv_p = gdn_mixer(P, 0, conv_p, gd_p0, 0, gd_p, l, dict(gd, T=Tp),
                                       chunk=PROMPT_CHUNK, seqs=1, chunks=GD_PROMPT_CHUNKS, t_valid=PROMPT_CHUNK)
        o_rw_s, rw_s, sh_s = rwkv7_mixer(P, Mp, prev_s, state_rwkv, l, rw_s, l, dict(rw, T=SAMPLE_STEPS),
                                         chunk=SAMPLE_STEPS, seqs=SAMPLE_SEQS, chunks=1, t_valid=Ts)
        o_gd_s, gd_s, cv_s = gdn_mixer(P, Mp, conv_s, state_gdn, l, gd_s, l, dict(gd, T=SAMPLE_STEPS),
                                       chunk=SAMPLE_STEPS, seqs=SAMPLE_SEQS, chunks=1, t_valid=Ts)

        x, q = mix_out_ln_q((o_rw_p, o_rw_s), (o_gd_p, o_gd_s), w_o, w_q, l, x, ln1_g[l], ln1_b[l],
                            alpha=alpha, tm=tm)

        att_p = cross_attention_prompt(q, kv, n_mem=n_mem, heads=xa_heads, tq=tq, q_tiles=Tp // tq)
        att_s = cross_attention_sample(q, Mp, mem_k, mem_v, l, heads=xa_heads, tq=SAMPLE_STEPS, seqs=SAMPLE_SEQS)
        x, xb = attn_out_ffn((att_p, att_s), w_xo, w_g, w_u, w_d, l, x, ln2_g[l], ln2_b[l], ln3_g[l], ln3_b[l],
                             alpha=alpha, tm=tm, tf=tf)

        outs['p_sh'].append(rwkv_cols(sh_p[:, 0])); outs['s_sh'].append(rwkv_cols(sh_s[:, 0]))
        outs['p_cv'].append(cv_p[:, SUBLANES - (GD_CONV - 1):])
        outs['s_cv'].append(cv_s[:, conv_at:conv_at + GD_CONV - 1])
        outs['p_mk'].append(kv[:, :xa_width].reshape(Bp, n_mem, xa_heads, -1))
        outs['p_mv'].append(kv[:, xa_width:].reshape(Bp, n_mem, xa_heads, -1))

    st = {n: jnp.stack(v) for n, v in outs.items()}
    return (x[:Mp].reshape(Bp, Tp, D), x[Mp:].reshape(Bs, SAMPLE_STEPS, D)[:, :Ts], rw_p, st['p_sh'],
            gd_p, st['p_cv'], st['p_mk'], st['p_mv'], rw_s, st['s_sh'], gd_s, st['s_cv'])
```

```python
import functools
import math

import jax
import jax.numpy as jnp
from jax import lax
from jax.experimental import pallas as pl
from jax.experimental.pallas import tpu as pltpu

F32 = jnp.float32
BF16 = jnp.bfloat16

LANES = 128
SUBLANES = 8
RW_HEAD = 64
GD_HEAD = 128
GD_CONV = 4
LN_EPS = 1e-5
RW_GN_EPS = 64e-5
GD_NORM_EPS = 1e-6
L2_EPS = 1e-12

PROMPT_CHUNK = 64
SAMPLE_STEPS = SUBLANES
SAMPLE_SEQS = 8
RW_PROMPT_CHUNKS = 2
GD_PROMPT_CHUNKS = 4

VMEM_LIMIT_BYTES = 60 * 1024 * 1024

_NN = (((1,), (0,)), ((), ()))
_NT = (((1,), (1,)), ((), ()))
_TN = (((0,), (0,)), ((), ()))
_TT = (((0,), (1,)), ((), ()))


def _cparams(*sem):
    return pltpu.CompilerParams(dimension_semantics=sem, vmem_limit_bytes=VMEM_LIMIT_BYTES)


def _round_up(n, m):
    return -(-n // m) * m


def _pick_tile(n, candidates):
    for c in candidates:
        if n % c == 0:
            return c
    raise ValueError(f"no tile in {candidates} divides {n}")


def _mm_kernel(x_ref, w_ref, o_ref):
    o_ref[...] = jnp.dot(x_ref[...], w_ref[...], preferred_element_type=F32).astype(o_ref.dtype)


def matmul(x, w, layer, *, tm, tn, out_dtype=F32):
    M, K = x.shape
    N = w.shape[2]
    assert M % tm == 0 and N % tn == 0
    return pl.pallas_call(
        _mm_kernel,
        grid=(M // tm, N // tn),
        in_specs=[pl.BlockSpec((tm, K), lambda i, j: (i, 0)),
                  pl.BlockSpec((None, K, tn), lambda i, j: (layer, 0, j))],
        out_specs=pl.BlockSpec((tm, tn), lambda i, j: (i, j)),
        out_shape=jax.ShapeDtypeStruct((M, N), out_dtype),
        compiler_params=_cparams("parallel", "arbitrary"),
        name="matmul",
    )(x, w)


def _group_rows(p_ref, s_ref, n_p):
    return jnp.where(pl.program_id(0) < n_p, p_ref[...], s_ref[...])


def _group_specs(tm, width, n_p):
    return (pl.BlockSpec((tm, width), lambda i, *_: (jnp.minimum(i, n_p - 1), 0)),
            pl.BlockSpec((tm, width), lambda i, *_: (jnp.maximum(i - n_p, 0), 0)))


def _n_prompt_tiles(pair, tm):
    assert pair[0].shape[0] % tm == 0 and pair[1].shape[0] % tm == 0 and pair[0].shape[1:] == pair[1].shape[1:]
    return pair[0].shape[0] // tm


def _project_kernel(xp_ref, xs_ref, w_ref, o_ref, *, n_p):
    o_ref[...] = jnp.dot(_group_rows(xp_ref, xs_ref, n_p), w_ref[...], preferred_element_type=F32)


def project_rows(x, w, layer, *, tm, tn):
    n_p = _n_prompt_tiles(x, tm)
    M = x[0].shape[0] + x[1].shape[0]
    K, N = w.shape[1:]
    assert N % tn == 0
    return pl.pallas_call(
        functools.partial(_project_kernel, n_p=n_p),
        grid=(M // tm, N // tn),
        in_specs=[*_group_specs(tm, K, n_p), pl.BlockSpec((None, K, tn), lambda i, j: (layer, 0, j))],
        out_specs=pl.BlockSpec((tm, tn), lambda i, j: (i, j)),
        out_shape=jax.ShapeDtypeStruct((M, N), F32),
        compiler_params=_cparams("parallel", "arbitrary"),
        name="project_rows",
    )(*x, w)


def _layer_norm_rows(y, g, b):
    mu = jnp.mean(y, -1, keepdims=True)
    yc = y - mu
    var = jnp.mean(yc * yc, -1, keepdims=True)
    return yc * lax.rsqrt(var + LN_EPS) * g + b


def _mix_out_kernel(ap_ref, as_ref, bp_ref, bs_ref, wa_ref, wb_ref, rp_ref, rs_ref, g_ref, beta_ref, wq_ref,
                    x_ref, q_ref, *, alpha, n_p):
    h = jnp.dot(_group_rows(ap_ref, as_ref, n_p), wa_ref[...], preferred_element_type=F32)
    h += jnp.dot(_group_rows(bp_ref, bs_ref, n_p), wb_ref[...], preferred_element_type=F32)
    x = _layer_norm_rows(alpha * _group_rows(rp_ref, rs_ref, n_p) + h, g_ref[...], beta_ref[...])
    x_ref[...] = x
    q_ref[...] = jnp.dot(x.astype(BF16), wq_ref[...], preferred_element_type=F32).astype(BF16)


def mix_out_ln_q(o_a, o_b, w, wq, layer, res, g, b, *, alpha, tm):
    n_p = _n_prompt_tiles(res, tm)
    assert _n_prompt_tiles(o_a, tm) == n_p and _n_prompt_tiles(o_b, tm) == n_p
    M = res[0].shape[0] + res[1].shape[0]
    D = res[0].shape[1]
    Ka, Kb = o_a[0].shape[1], o_b[0].shape[1]
    Q = wq.shape[2]
    assert Ka == Kb and w.shape[1] == Ka + Kb
    rows = lambda n: pl.BlockSpec((tm, n), lambda i: (i, 0))
    vec = pl.BlockSpec((1, D), lambda i: (0, 0))
    return pl.pallas_call(
        functools.partial(_mix_out_kernel, alpha=alpha, n_p=n_p),
        grid=(M // tm,),
        in_specs=[*_group_specs(tm, Ka, n_p), *_group_specs(tm, Kb, n_p),
                  pl.BlockSpec((None, Ka, D), lambda i: (layer, 0, 0)),
                  pl.BlockSpec((None, Kb, D), lambda i: (layer, 1, 0)),
                  *_group_specs(tm, D, n_p), vec, vec,
                  pl.BlockSpec((None, D, Q), lambda i: (layer, 0, 0))],
        out_specs=[rows(D), rows(Q)],
        out_shape=[jax.ShapeDtypeStruct((M, D), F32), jax.ShapeDtypeStruct((M, Q), BF16)],
        compiler_params=_cparams("parallel"),
        name="mix_out_ln_q",
    )(*o_a, *o_b, w, w, *res, g.reshape(1, D), b.reshape(1, D), wq)


def _attn_ffn_kernel(ap_ref, as_ref, wo_ref, res_ref, g2_ref, b2_ref, wg_ref, wu_ref, wd_ref, g3_ref, b3_ref,
                     op_ref, os_ref, obp_ref, obs_ref, x_scr, xb_scr, acc_ref, *, alpha, n_p):
    i = pl.program_id(0)
    f = pl.program_id(1)

    @pl.when(f == 0)
    def _():
        h = jnp.dot(_group_rows(ap_ref, as_ref, n_p), wo_ref[...], preferred_element_type=F32)
        x = _layer_norm_rows(alpha * res_ref[...] + h, g2_ref[...], b2_ref[...])
        x_scr[...] = x
        xb_scr[...] = x.astype(BF16)
        acc_ref[...] = jnp.zeros_like(acc_ref)

    x = xb_scr[...]
    gate = jnp.dot(x, wg_ref[...], preferred_element_type=F32)
    up = jnp.dot(x, wu_ref[...], preferred_element_type=F32)
    hidden = (gate * jax.nn.sigmoid(gate) * up).astype(BF16)
    acc_ref[...] += jnp.dot(hidden, wd_ref[...], preferred_element_type=F32)

    last = f == pl.num_programs(1) - 1

    def finish(o_ref, ob_ref):
        y = _layer_norm_rows(alpha * x_scr[...] + acc_ref[...], g3_ref[...], b3_ref[...])
        o_ref[...] = y
        ob_ref[...] = y.astype(BF16)

    pl.when(last & (i < n_p))(lambda: finish(op_ref, obp_ref))
    pl.when(last & (i >= n_p))(lambda: finish(os_ref, obs_ref))


def attn_out_ffn(att, wo, wg, wu, wd, layer, res, g2, b2, g3, b3, *, alpha, tm, tf):
    M, D = res.shape
    Q = att[0].shape[1]
    Fd = wg.shape[2]
    n_p = _n_prompt_tiles(att, tm)
    Mp, Ms = att[0].shape[0], att[1].shape[0]
    assert Fd % tf == 0 and Mp + Ms == M
    vec = pl.BlockSpec((1, D), lambda i, f: (0, 0))
    once = pl.Buffered(1)
    rows = pl.BlockSpec((tm, D), lambda i, f: (i, 0), pipeline_mode=once)
    out_p, out_s = _group_specs(tm, D, n_p)
    out_s = pl.BlockSpec(out_s.block_shape, out_s.index_map, pipeline_mode=once)
    return pl.pallas_call(
        functools.partial(_attn_ffn_kernel, alpha=alpha, n_p=n_p),
        grid=(M // tm, Fd // tf),
        in_specs=[*_group_specs(tm, Q, n_p),
                  pl.BlockSpec((None, Q, D), lambda i, f: (layer, 0, 0), pipeline_mode=once), rows, vec, vec,
                  pl.BlockSpec((None, D, tf), lambda i, f: (layer, 0, f)),
                  pl.BlockSpec((None, D, tf), lambda i, f: (layer, 0, f)),
                  pl.BlockSpec((None, tf, D), lambda i, f: (layer, f, 0)), vec, vec],
        out_specs=[out_p, out_s, out_p, out_s],
        out_shape=[jax.ShapeDtypeStruct((Mp, D), F32), jax.ShapeDtypeStruct((Ms, D), F32),
                   jax.ShapeDtypeStruct((Mp, D), BF16), jax.ShapeDtypeStruct((Ms, D), BF16)],
        scratch_shapes=[pltpu.VMEM((tm, D), F32), pltpu.VMEM((tm, D), BF16), pltpu.VMEM((tm, D), F32)],
        compiler_params=_cparams("arbitrary", "arbitrary"),
        name="attn_out_ffn",
    )(*att, wo, res, g2.reshape(1, D), b2.reshape(1, D), wg, wu, wd, g3.reshape(1, D), b3.reshape(1, D))


def _mm(a, b, dims):
    return lax.dot_general(a.astype(BF16), b.astype(BF16), dims, preferred_element_type=F32)


def _three_bf16_terms(a):
    a1 = a.astype(BF16)
    r1 = a - a1.astype(F32)
    a2 = r1.astype(BF16)
    return a1, a2, (r1 - a2.astype(F32)).astype(BF16)


def _unit_lower_inverse(xs, c):
    row = lax.broadcasted_iota(jnp.int32, (c, c), 0)
    col = lax.broadcasted_iota(jnp.int32, (c, c), 1)
    eye = jnp.where(row == col, 1.0, 0.0).astype(F32)
    ts = [eye + x for x in xs]
    ps = list(xs)
    span = 2
    while span < c:
        ps = [_mm(p, p, _NN) for p in ps]
        ts = [t + _mm(t, p, _NN) for t, p in zip(ts, ps)]
        span *= 2
    return ts


def _step_in_seq(R, C):
    return lax.broadcasted_iota(jnp.int32, (R, 1), 0) & (C - 1)


def _seq_tri(R, C):
    row = lax.broadcasted_iota(jnp.int32, (R, R), 0)
    col = lax.broadcasted_iota(jnp.int32, (R, R), 1)
    return jnp.where(row >= col, jnp.where((row & -C) == (col & -C), 1.0, 0.0), 0.0).astype(BF16)


def _per_seq_rows(rows, C):
    rows = [jnp.broadcast_to(t, (C, t.shape[-1])) for t in rows]
    return rows[0] if len(rows) == 1 else jnp.concatenate(rows, axis=0)


def _delayed_rows(x, tails, j, BB, C):
    xj = pltpu.roll(x, j, 0)
    row = lax.broadcasted_iota(jnp.int32, (SUBLANES, 1), 0)
    pieces = []
    for i in range(BB):
        pieces.append(jnp.where(row < j, pltpu.roll(tails[i], j, 0), xj[i * C:i * C + SUBLANES]))
        if C > SUBLANES:
            pieces.append(xj[i * C + SUBLANES:(i + 1) * C])
    return pieces[0] if len(pieces) == 1 else jnp.concatenate(pieces, axis=0)


def _half_sums(x, N):
    low = lax.broadcasted_iota(jnp.int32, (1, LANES), 1) < N
    out = []
    for j in range(x.shape[1] // LANES):
        xv = x[:, j * LANES:(j + 1) * LANES]
        lo = jnp.sum(jnp.where(low, xv, 0.0), -1, keepdims=True)
        out += [lo, jnp.sum(jnp.where(low, 0.0, xv), -1, keepdims=True)]
    return out


def _half_bcast(cols, N):
    low = lax.broadcasted_iota(jnp.int32, (1, LANES), 1) < N
    return jnp.concatenate([jnp.where(low, cols[2 * j], cols[2 * j + 1]) for j in range(len(cols) // 2)], axis=1)


def _rwkv_kernel(r_ref, k_ref, v_ref, sm_ref, prev_ref, mu_ref, vec_ref, w2_ref, a2_ref, g2_ref,
                 s0_ref, st_buf_ref, o_ref, st_ref, last_ref, s_scr, tail_scr, o_scr, *, C, H, N, NS, L, t_valid):
    c = pl.program_id(1)
    W = H * N
    BB = NS * L
    R = BB * C
    Rs = L * C
    sw, sa, sg = w2_ref.shape[0], a2_ref.shape[0], g2_ref.shape[0]

    @pl.when(c == 0)
    def _():
        s_scr[...] = s0_ref[...]
        tail_scr[...] = prev_ref[...]

    first = _step_in_seq(R, Rs) == 0
    vec = vec_ref[...]
    w0, a0, k_k, k_a, r_k, ln_w, ln_b = (vec[j:j + 1] for j in range(7))
    mu = mu_ref[...]
    tail = _per_seq_rows([tail_scr[s] for s in range(NS)], Rs)

    def shifted(x, lo):
        hi = lo + x.shape[1]
        prev = jnp.where(first, tail[:, lo:hi], pltpu.roll(x, 1, 0))
        return x + (prev - x) * mu[:, lo:hi]

    xr, xk, xv, xs = r_ref[...], k_ref[...], v_ref[...], sm_ref[...]
    r, k, v, ms = shifted(xr, 0), shifted(xk, W), shifted(xv, 2 * W), shifted(xs, 3 * W)
    for x, lo in ((xr, 0), (xk, W), (xv, 2 * W), (xs, 3 * W)):
        for s in range(NS):
            end = s * Rs + (L - 1) * C + t_valid
            tail_scr[s, :, lo:lo + x.shape[1]] = x[end - 1:end]

    lora_w = jnp.dot(jnp.tanh(ms[:, :sw]).astype(BF16), w2_ref[...], preferred_element_type=F32)
    lora_a = jnp.dot(ms[:, sw:sw + sa].astype(BF16), a2_ref[...], preferred_element_type=F32)
    gate = jnp.dot(jax.nn.sigmoid(ms[:, sw + sa:sw + sa + sg]).astype(BF16), g2_ref[...],
                   preferred_element_type=F32)
    lw = -jnp.exp(-jax.nn.softplus(-(w0 + lora_w)) - 0.5)
    a = jax.nn.sigmoid(a0 + lora_a)
    kk = k * k_k
    kk = kk * _half_bcast([lax.rsqrt(s + L2_EPS) for s in _half_sums(kk * kk, N)], N)
    k = k * (1.0 + (a - 1.0) * k_a)
    if t_valid < C:
        valid = _step_in_seq(R, C) < t_valid
        r, k, v, kk, lw = (jnp.where(valid, t, 0.0) for t in (r, k, v, kk, lw))

    tri = _seq_tri(R, C)
    G = sum(jnp.dot(tri, t, preferred_element_type=F32) for t in _three_bf16_terms(lw))
    mid = C // 2 - 1
    g_mid = [G[i * C + mid:i * C + mid + 1] for i in range(BB)]
    g_last = [G[(i + 1) * C - 1:(i + 1) * C] for i in range(BB)]
    Gm = G - _per_seq_rows(g_mid, C)
    e_out = jnp.exp(-Gm)
    e_tail = jnp.exp(_per_seq_rows(g_last, C) - G)
    b_in = kk * a
    A_, R_ = -kk * jnp.exp(Gm - lw), r * jnp.exp(Gm)
    B_, K_ = b_in * e_out, k * e_out
    Bt, Kt = b_in * e_tail, k * e_tail
    e_mid, e_last = [jnp.exp(t) for t in g_mid], [jnp.exp(t) for t in g_last]

    row2 = lax.broadcasted_iota(jnp.int32, (2 * C, 2 * C), 0)
    col2 = lax.broadcasted_iota(jnp.int32, (2 * C, 2 * C), 1)
    rowc = jnp.where(row2 >= C, row2 - C, row2)
    colc = jnp.where(col2 >= C, col2 - C, col2)
    keep = jnp.where(row2 >= C, rowc, rowc - 1) >= colc

    rs = lambda i: slice(i * C, (i + 1) * C)
    sl = lambda h: slice(h * N, (h + 1) * N)
    pair = lambda x, y, i, h: jnp.concatenate([x[rs(i), sl(h)], y[rs(i), sl(h)]], axis=0)
    every = [(i, h) for i in range(BB) for h in range(H)]
    AR = {ih: pair(A_, R_, *ih) for ih in every}
    M = {ih: jnp.where(keep, _mm(AR[ih], pair(B_, K_, *ih), _NT), 0.0) for ih in every}
    AV = {(i, h): _mm(M[i, h][:C, C:], v[rs(i), sl(h)], _NN) for i, h in every}
    T = dict(zip(every, _unit_lower_inverse([M[ih][:C, :C] for ih in every], C)))
    S = {(s, h): s_scr[s, h] for s in range(NS) for h in range(H)}
    for lvl in range(L):
        ch = [(s, s * L + lvl, h) for s in range(NS) for h in range(H)]
        SR = {(i, h): _mm(AR[i, h], S[s, h] * e_mid[i][:, sl(h)], _NT) for s, i, h in ch}
        U = {(i, h): _mm(T[i, h], SR[i, h][:C] + AV[i, h], _NN) for _, i, h in ch}
        UV = {(i, h): jnp.concatenate([U[i, h], v[rs(i), sl(h)]], axis=0) for _, i, h in ch}
        O = {(i, h): SR[i, h][C:] + _mm(M[i, h][C:], UV[i, h], _NN) for _, i, h in ch}
        S = {(s, h): S[s, h] * e_last[i][:, sl(h)] + _mm(UV[i, h], pair(Bt, Kt, i, h), _TN) for s, i, h in ch}
        for _, i, h in ch:
            o_scr[rs(i), sl(h)] = O[i, h]
    for (s, h), val in S.items():
        s_scr[s, h] = val

    o = o_scr[...]
    o = o - _half_bcast([s * (1.0 / N) for s in _half_sums(o, N)], N)
    o = o * _half_bcast([lax.rsqrt(s * (1.0 / N) + RW_GN_EPS) for s in _half_sums(o * o, N)], N) * ln_w + ln_b
    bonus = _half_bcast(_half_sums(r * k * r_k, N), N) * v
    o_ref[...] = ((o + bonus) * gate).astype(o_ref.dtype)

    @pl.when(c == pl.num_programs(1) - 1)
    def _():
        st_ref[...] = s_scr[...]
        last_ref[...] = tail_scr[...]


def _state_specs(states, s0_layer, layer, seqs):
    blk = (None, seqs) + states.shape[2:]
    return (pl.BlockSpec(blk, lambda i, c: (s0_layer, i, 0, 0, 0)),
            pl.BlockSpec(blk, lambda i, c: (layer, i, 0, 0, 0)))


def rwkv7_mixer(P, row0, prev, s0, s0_layer, st_buf, layer, prm, *, chunk, seqs, chunks, t_valid):
    _, B, H, N, _ = s0.shape
    W = H * N
    SW = prm['mu'].shape[-1] - 3 * W
    T = prm['T']
    blk = seqs * chunks * chunk
    nc = T // (chunks * chunk)
    assert T % (chunks * chunk) == 0 and B % seqs == 0 and row0 % blk == 0 and (7 * W) % SW == 0
    assert (seqs == 1 or nc == 1) and (chunks == 1 or t_valid == chunk)
    assert 2 * N == LANES and chunk & (chunk - 1) == 0 and st_buf.shape[1:] == s0.shape[1:]
    rb0 = row0 // blk
    col = lambda j: pl.BlockSpec((blk, W), lambda i, c: (rb0 + i * nc + c, j))
    full = lambda a: pl.BlockSpec((None,) + a.shape[1:], lambda i, c: (layer,) + (0,) * (a.ndim - 1))
    st_in, st_out = _state_specs(s0, s0_layer, layer, seqs)
    row = pl.BlockSpec((seqs, 1, 3 * W + SW), lambda i, c: (i, 0, 0))
    consts = [prm['mu'], prm['vec'], prm['w2'], prm['a2'], prm['g2']]
    return pl.pallas_call(
        functools.partial(_rwkv_kernel, C=chunk, H=H, N=N, NS=seqs, L=chunks, t_valid=t_valid),
        grid=(B // seqs, nc),
        in_specs=[col(0), col(1), col(2),
                  pl.BlockSpec((blk, SW), lambda i, c: (rb0 + i * nc + c, 7 * W // SW)),
                  pl.BlockSpec((None, seqs, 1, 3 * W + SW), lambda i, c: (s0_layer, i, 0, 0))]
                 + [full(a) for a in consts] + [st_in, pl.BlockSpec(memory_space=pl.ANY)],
        out_specs=[pl.BlockSpec((blk, W), lambda i, c: (i * nc + c, 0)), st_out, row],
        out_shape=[jax.ShapeDtypeStruct((B * T, W), BF16), jax.ShapeDtypeStruct(st_buf.shape, F32),
                   jax.ShapeDtypeStruct((B, 1, 3 * W + SW), F32)],
        scratch_shapes=[pltpu.VMEM((seqs, H, N, N), F32), pltpu.VMEM((seqs, 1, 3 * W + SW), F32),
                        pltpu.VMEM((blk, W), F32)],
        input_output_aliases={5 + len(consts) + 1: 1},
        compiler_params=_cparams("parallel", "arbitrary"),
        name="rwkv7_mixer",
    )(P, P, P, P, prev, *consts, s0, st_buf)


def _gdn_kernel(q_ref, k_ref, v_ref, z_ref, sm_ref, cprev_ref, cw_ref, gate_ref, nw_ref, s0_ref, st_buf_ref,
                o_ref, st_ref, last_ref, s_scr, tail_scr, *, C, H, N, NS, L, t_valid, gate_lane):
    c = pl.program_id(1)
    W = H * N
    BB = NS * L
    R = BB * C
    Rs = L * C

    @pl.when(c == 0)
    def _():
        s_scr[...] = s0_ref[...]
        tail_scr[...] = cprev_ref[...]

    cw = cw_ref[...]
    nw = nw_ref[...]
    tails = [tail_scr[s] for s in range(NS)]

    def conv_silu(x, lo):
        hi = lo + W
        acc = x * cw[GD_CONV - 1:GD_CONV, lo:hi]
        for j in range(1, GD_CONV):
            acc = acc + (_delayed_rows(x, [t[:, lo:hi] for t in tails], j, NS, Rs)
                         * cw[GD_CONV - 1 - j:GD_CONV - j, lo:hi])
        return acc * jax.nn.sigmoid(acc)

    xq, xk, xv = q_ref[...], k_ref[...], v_ref[...]
    q_all, k_all, v_all = conv_silu(xq, 0), conv_silu(xk, W), conv_silu(xv, 2 * W)
    for x, lo in ((xq, 0), (xk, W), (xv, 2 * W)):
        for s in range(NS):
            tail_scr[s, :, lo:lo + W] = x[(s + 1) * Rs - SUBLANES:(s + 1) * Rs]
    z_all = z_ref[...]
    gates = sm_ref[:, gate_lane:gate_lane + LANES]
    beta = jax.nn.sigmoid(gates[:, :H])
    g = -jnp.exp(gate_ref[0:1, :]) * jax.nn.softplus(gates[:, H:2 * H] + gate_ref[1:2, :])
    if t_valid < C:
        valid = _step_in_seq(R, C) < t_valid
        beta, g, q_all, k_all, v_all = (jnp.where(valid, t, 0.0) for t in (beta, g, q_all, k_all, v_all))
    tri = _seq_tri(R, C)
    g_terms = _three_bf16_terms(g)
    Gc = sum(jnp.dot(tri, t, preferred_element_type=F32) for t in g_terms)
    Gr = sum(lax.dot_general(t, tri, _TT, preferred_element_type=F32) for t in g_terms)

    row2 = lax.broadcasted_iota(jnp.int32, (2 * C, C), 0)
    col2 = lax.broadcasted_iota(jnp.int32, (2 * C, C), 1)
    keep = jnp.where(row2 >= C, row2 - C, row2 - 1) >= col2

    l2 = lambda t: t * lax.rsqrt(jnp.sum(t * t, -1, keepdims=True) + L2_EPS)
    sl = lambda h: slice(h * N, (h + 1) * N)
    qn = [l2(q_all[:, sl(h)]) * (N ** -0.5) for h in range(H)]
    kn = [l2(k_all[:, sl(h)]) for h in range(H)]

    rs = lambda i: slice(i * C, (i + 1) * C)
    every = [(i, h) for i in range(BB) for h in range(H)]
    gc = {(i, h): Gc[rs(i), h:h + 1] for i, h in every}
    g_last = {(i, h): Gc[(i + 1) * C - 1:(i + 1) * C, h:h + 1] for i, h in every}
    gc2 = {ih: jnp.concatenate([gc[ih], gc[ih]], axis=0) for ih in every}
    decay = {(i, h): jnp.where(keep, jnp.exp(jnp.where(keep, gc2[i, h] - Gr[h:h + 1, rs(i)], 0.0)), 0.0)
             for i, h in every}
    bh = {(i, h): beta[rs(i), h:h + 1] for i, h in every}
    kj = {(i, h): kn[h][rs(i)] for i, h in every}
    KQ = {(i, h): jnp.concatenate([kj[i, h] * bh[i, h], qn[h][rs(i)]], axis=0) for i, h in every}
    M = {ih: _mm(KQ[ih], kj[ih], _NT) * decay[ih] for ih in every}
    T = dict(zip(every, _unit_lower_inverse([-M[ih][:C] for ih in every], C)))
    S = {(s, h): s_scr[s, h] for s in range(NS) for h in range(H)}
    for lvl in range(L):
        ch = [(s, s * L + lvl, h) for s in range(NS) for h in range(H)]
        SR = {(i, h): _mm(KQ[i, h] * jnp.exp(gc2[i, h]), S[s, h], _NN) for s, i, h in ch}
        v_new = {(i, h): _mm(T[i, h], v_all[rs(i), sl(h)] * bh[i, h] - SR[i, h][:C], _NN) for _, i, h in ch}
        O = {(i, h): SR[i, h][C:] + _mm(M[i, h][C:], v_new[i, h], _NN) for _, i, h in ch}
        S = {(s, h): S[s, h] * jnp.exp(g_last[i, h])
             + _mm(kj[i, h] * jnp.exp(g_last[i, h] - gc[i, h]), v_new[i, h], _TN) for s, i, h in ch}
        for _, i, h in ch:
            zh = z_all[rs(i), sl(h)]
            o = O[i, h] * lax.rsqrt(jnp.mean(O[i, h] * O[i, h], -1, keepdims=True) + GD_NORM_EPS) * nw
            o_ref[rs(i), sl(h)] = (o * (zh * jax.nn.sigmoid(zh))).astype(o_ref.dtype)
    for (s, h), val in S.items():
        s_scr[s, h] = val

    @pl.when(c == pl.num_programs(1) - 1)
    def _():
        st_ref[...] = s_scr[...]
        last_ref[...] = tail_scr[...]


def gdn_mixer(P, row0, conv_prev, s0, s0_layer, st_buf, layer, prm, *, chunk, seqs, chunks, t_valid):
    _, B, H, N, _ = s0.shape
    W = H * N
    SW = prm['SW']
    T = prm['T']
    blk = seqs * chunks * chunk
    nc = T // (chunks * chunk)
    assert T % (chunks * chunk) == 0 and B % seqs == 0 and row0 % blk == 0 and chunk >= SUBLANES
    assert (seqs == 1 or nc == 1) and (chunks == 1 or t_valid == chunk) and 2 * H <= LANES
    assert N == LANES and chunk & (chunk - 1) == 0 and st_buf.shape[1:] == s0.shape[1:]
    rb0 = row0 // blk
    col = lambda j: pl.BlockSpec((blk, W), lambda i, c: (rb0 + i * nc + c, j))
    full = lambda a: pl.BlockSpec((None,) + a.shape[1:], lambda i, c: (layer,) + (0,) * (a.ndim - 1))
    st_in, st_out = _state_specs(s0, s0_layer, layer, seqs)
    rows = pl.BlockSpec((seqs, SUBLANES, 3 * W), lambda i, c: (i, 0, 0))
    consts = [prm['conv_w'], prm['gate'], prm['norm_w']]
    return pl.pallas_call(
        functools.partial(_gdn_kernel, C=chunk, H=H, N=N, NS=seqs, L=chunks, t_valid=t_valid,
                          gate_lane=SW - LANES),
        grid=(B // seqs, nc),
        in_specs=[col(3), col(4), col(5), col(6),
                  pl.BlockSpec((blk, SW), lambda i, c: (rb0 + i * nc + c, 7 * W // SW)),
                  pl.BlockSpec((None, seqs, SUBLANES, 3 * W), lambda i, c: (s0_layer, i, 0, 0))]
                 + [full(a) for a in consts] + [st_in, pl.BlockSpec(memory_space=pl.ANY)],
        out_specs=[pl.BlockSpec((blk, W), lambda i, c: (i * nc + c, 0)), st_out, rows],
        out_shape=[jax.ShapeDtypeStruct((B * T, W), BF16), jax.ShapeDtypeStruct(st_buf.shape, F32),
                   jax.ShapeDtypeStruct((B, SUBLANES, 3 * W), F32)],
        scratch_shapes=[pltpu.VMEM((seqs, H, N, N), F32), pltpu.VMEM((seqs, SUBLANES, 3 * W), F32)],
        input_output_aliases={6 + len(consts) + 1: 1},
        compiler_params=_cparams("parallel", "arbitrary"),
        name="gdn_mixer",
    )(P, P, P, P, P, conv_prev, *consts, s0, st_buf)


def _xattn_kernel(q_ref, k_ref, v_ref, o_ref, *, H, Dh, BB, tq, head_rows):
    scale = Dh ** -0.5
    pairs = [(b, h) for b in range(BB) for h in range(H)]
    sl = [slice(h * Dh, (h + 1) * Dh) for h in range(H)]

    def head(ref, b, h):
        if head_rows:
            return ref[b, pl.ds(h, ref.shape[1] // H, stride=H), :].astype(BF16)
        return ref[:, sl[h]].astype(BF16)

    q = [q_ref[b * tq:(b + 1) * tq, :] for b in range(BB)]
    s = [lax.dot_general(q[b][:, sl[h]], head(k_ref, b, h), _NT, preferred_element_type=F32) * scale
         for b, h in pairs]
    e = [jnp.exp(t - jnp.max(t, -1, keepdims=True)) for t in s]
    pr = [(t / jnp.sum(t, -1, keepdims=True)).astype(BF16) for t in e]
    o = [jnp.dot(pr[j], head(v_ref, b, h), preferred_element_type=F32) for j, (b, h) in enumerate(pairs)]
    for j, (b, h) in enumerate(pairs):
        o_ref[b * tq:(b + 1) * tq, sl[h]] = o[j].astype(o_ref.dtype)


def cross_attention_prompt(q, kv, *, n_mem, heads, tq, q_tiles):
    Wd = kv.shape[1] // 2
    B = kv.shape[0] // n_mem
    return pl.pallas_call(
        functools.partial(_xattn_kernel, H=heads, Dh=Wd // heads, BB=1, tq=tq, head_rows=False),
        grid=(B, q_tiles),
        in_specs=[pl.BlockSpec((tq, Wd), lambda i, t: (i * q_tiles + t, 0)),
                  pl.BlockSpec((n_mem, Wd), lambda i, t: (i, 0)),
                  pl.BlockSpec((n_mem, Wd), lambda i, t: (i, 1))],
        out_specs=pl.BlockSpec((tq, Wd), lambda i, t: (i * q_tiles + t, 0)),
        out_shape=jax.ShapeDtypeStruct((B * q_tiles * tq, Wd), BF16),
        compiler_params=_cparams("parallel", "arbitrary"),
        name="cross_attention_prompt",
    )(q, kv, kv)


def cross_attention_sample(q, row0, mem_k, mem_v, layer, *, heads, tq, seqs):
    _, B, rows, Dh = mem_k.shape
    Wd = heads * Dh
    blk = seqs * tq
    assert B % seqs == 0 and row0 % blk == 0
    rb0 = row0 // blk
    kv = pl.BlockSpec((None, seqs, rows, Dh), lambda i: (layer, i, 0, 0))
    return pl.pallas_call(
        functools.partial(_xattn_kernel, H=heads, Dh=Dh, BB=seqs, tq=tq, head_rows=True),
        grid=(B // seqs,),
        in_specs=[pl.BlockSpec((blk, Wd), lambda i: (rb0 + i, 0)), kv, kv],
        out_specs=pl.BlockSpec((blk, Wd), lambda i: (i, 0)),
        out_shape=jax.ShapeDtypeStruct((B * tq, Wd), BF16),
        compiler_params=_cparams("parallel"),
        name="cross_attention_sample",
    )(q, mem_k, mem_v)


def _lane_pad(a, width):
    return jnp.pad(a, [(0, 0)] * (a.ndim - 1) + [(0, width - a.shape[-1])])


def kernel(x_prompt, mem_prompt, x_sample, state_rwkv, state_shift, state_gdn, state_conv, cache_mem_k, cache_mem_v, w_in, mu_shift, rw_w0, rw_w2, rw_a0, rw_a2, rw_g2, rw_kk, rw_ka, rw_rk, rw_lnx_w, rw_lnx_b, gd_conv_w, gd_a_log, gd_dt_bias, gd_norm_w, w_out, ln1_g, ln1_b, xa_wq, xa_wk, xa_wv, xa_wo, ln2_g, ln2_b, ff_wg, ff_wu, ff_wd, ln3_g, ln3_b):
    depth = w_in.shape[0]
    Bp, Tp, D = x_prompt.shape
    Bs, Ts, _ = x_sample.shape
    Hr, Hg = state_rwkv.shape[2], state_gdn.shape[2]
    W = Hr * RW_HEAD
    assert W == Hg * GD_HEAD and Ts <= SAMPLE_STEPS and Ts >= GD_CONV - 1 and Tp % PROMPT_CHUNK == 0
    n_w, n_a, n_g = rw_w2.shape[1], rw_a2.shape[1], rw_g2.shape[1]
    sw, sa, sg = (_round_up(n, LANES) for n in (n_w, n_a, n_g))
    SW = sw + sa + sg + LANES
    n_mem, xa_heads = cache_mem_k.shape[2], cache_mem_k.shape[3]
    xa_width = xa_heads * cache_mem_k.shape[4]
    alpha = (2 * depth) ** 0.25
    Mp, Ms = Bp * Tp, Bs * SAMPLE_STEPS
    M = Mp + Ms
    tm = _pick_tile(math.gcd(Mp, Ms), (512, 384, 256, 128))
    tf = _pick_tile(ff_wg.shape[-1], (512, 256, 128))
    c_w, c_a, c_g = 3 * W, 3 * W + n_w, 3 * W + n_w + n_a
    c_gd = c_g + n_g

    def to_layout(t):
        parts = [t[..., :c_w], t[..., c_gd:c_gd + 4 * W],
                 _lane_pad(t[..., c_w:c_a], sw), _lane_pad(t[..., c_a:c_g], sa), _lane_pad(t[..., c_g:c_gd], sg),
                 _lane_pad(t[..., c_gd + 4 * W:], LANES)]
        return jnp.concatenate(parts, axis=-1)

    def rwkv_layout(t):
        parts = [t[..., :c_w], _lane_pad(t[..., c_w:c_a], sw), _lane_pad(t[..., c_a:c_g], sa),
                 _lane_pad(t[..., c_g:c_gd], sg + LANES)]
        return jnp.concatenate(parts, axis=-1)

    def rwkv_cols(t):
        s0 = 3 * W
        return jnp.concatenate([t[..., :3 * W], t[..., s0:s0 + n_w], t[..., s0 + sw:s0 + sw + n_a],
                                t[..., s0 + sw + sa:s0 + sw + sa + n_g]], axis=-1)

    w_proj = to_layout(w_in.astype(BF16))
    w_kv = jnp.concatenate([xa_wk, xa_wv], axis=2).astype(BF16)
    w_o, w_q, w_xo = w_out.astype(BF16), xa_wq.astype(BF16), xa_wo.astype(BF16)
    w_g, w_u, w_d = ff_wg.astype(BF16), ff_wu.astype(BF16), ff_wd.astype(BF16)

    x = (x_prompt.reshape(Mp, D), jnp.pad(x_sample, ((0, 0), (0, SAMPLE_STEPS - Ts), (0, 0))).reshape(Ms, D))
    xb = (x[0].astype(BF16), x[1].astype(BF16))
    mem = mem_prompt.reshape(Bp * n_mem, D).astype(BF16)
    mem_k = cache_mem_k.reshape(depth, Bs, n_mem * xa_heads, -1)
    mem_v = cache_mem_v.reshape(depth, Bs, n_mem * xa_heads, -1)
    zeros = lambda *s: jnp.zeros(s, F32)
    tq = _pick_tile(Tp, (512, 256, 128))
    rw_p, gd_p = zeros(depth, Bp, Hr, RW_HEAD, RW_HEAD), zeros(depth, Bp, Hg, GD_HEAD, GD_HEAD)
    rw_s, gd_s = zeros(*state_rwkv.shape), zeros(*state_gdn.shape)
    rw_p0, gd_p0 = zeros(1, Bp, Hr, RW_HEAD, RW_HEAD), zeros(1, Bp, Hg, GD_HEAD, GD_HEAD)
    conv_at = SUBLANES - (SAMPLE_STEPS - Ts) - (GD_CONV - 1)

    outs = {n: [] for n in ('p_sh', 'p_cv', 'p_mk', 'p_mv', 's_sh', 's_cv')}
    lora = lambda w, rows: jnp.pad(w, ((0, 0), (0, rows - w.shape[1]), (0, 0))).astype(BF16)
    rw = {'T': None, 'mu': rwkv_layout(mu_shift)[:, None],
          'vec': jnp.stack([rw_w0, rw_a0, rw_kk, rw_ka, rw_rk.reshape(depth, W), rw_lnx_w, rw_lnx_b,
                            jnp.zeros((depth, W), F32)], axis=1),
          'w2': lora(rw_w2, sw), 'a2': lora(rw_a2, sa), 'g2': lora(rw_g2, sg)}
    gd = {'T': None, 'SW': SW, 'conv_w': gd_conv_w, 'gate': jnp.stack([gd_a_log, gd_dt_bias], axis=1),
          'norm_w': gd_norm_w[:, None]}
    prev_s = rwkv_layout(state_shift)[:, :, None]
    conv_s = jnp.pad(state_conv, ((0, 0), (0, 0), (SUBLANES - (GD_CONV - 1), 0), (0, 0)))
    prev_p, conv_p = zeros(1, Bp, 1, 3 * W + SW), zeros(1, Bp, SUBLANES, 3 * W)

    for l in range(depth):
        kv = matmul(mem, w_kv, l, tm=_pick_tile(Bp * n_mem, (512, 256, 128)),
                    tn=_pick_tile(2 * xa_width, (512, 256, 128)))
        P = project_rows(xb, w_proj, l, tm=_pick_tile(math.gcd(Mp, Ms), (1024, 512, 384, 256, 128)),
                         tn=_pick_tile(7 * W + SW, (768, 512, 256, 128)))

        o_rw_p, rw_p, sh_p = rwkv7_mixer(P, 0, prev_p, rw_p0, 0, rw_p, l, dict(rw, T=Tp),
                                         chunk=PROMPT_CHUNK, seqs=1, chunks=RW_PROMPT_CHUNKS,
                                         t_valid=PROMPT_CHUNK)
        o_gd_p, gd_p, cv_p = gdn_mixer(P, 0, conv_p, gd_p0, 0, gd_p, l, dict(gd, T=Tp),
                                       chunk=PROMPT_CHUNK, seqs=1, chunks=GD_PROMPT_CHUNKS, t_valid=PROMPT_CHUNK)
        o_rw_s, rw_s, sh_s = rwkv7_mixer(P, Mp, prev_s, state_rwkv, l, rw_s, l, dict(rw, T=SAMPLE_STEPS),
                                         chunk=SAMPLE_STEPS, seqs=SAMPLE_SEQS, chunks=1, t_valid=Ts)
        o_gd_s, gd_s, cv_s = gdn_mixer(P, Mp, conv_s, state_gdn, l, gd_s, l, dict(gd, T=SAMPLE_STEPS),
                                       chunk=SAMPLE_STEPS, seqs=SAMPLE_SEQS, chunks=1, t_valid=Ts)

        x1, q = mix_out_ln_q((o_rw_p, o_rw_s), (o_gd_p, o_gd_s), w_o, w_q, l, x, ln1_g[l], ln1_b[l],
                             alpha=alpha, tm=tm)

        att_p = cross_attention_prompt(q, kv, n_mem=n_mem, heads=xa_heads, tq=tq, q_tiles=Tp // tq)
        att_s = cross_attention_sample(q, Mp, mem_k, mem_v, l, heads=xa_heads, tq=SAMPLE_STEPS, seqs=SAMPLE_SEQS)
        y = attn_out_ffn((att_p, att_s), w_xo, w_g, w_u, w_d, l, x1, ln2_g[l], ln2_b[l], ln3_g[l], ln3_b[l],
                         alpha=alpha, tm=tm, tf=tf)
        x, xb = y[:2], y[2:]

        outs['p_sh'].append(rwkv_cols(sh_p[:, 0])); outs['s_sh'].append(rwkv_cols(sh_s[:, 0]))
        outs['p_cv'].append(cv_p[:, SUBLANES - (GD_CONV - 1):])
        outs['s_cv'].append(cv_s[:, conv_at:conv_at + GD_CONV - 1])
        outs['p_mk'].append(kv[:, :xa_width].reshape(Bp, n_mem, xa_heads, -1))
        outs['p_mv'].append(kv[:, xa_width:].reshape(Bp, n_mem, xa_heads, -1))

    st = {n: jnp.stack(v) for n, v in outs.items()}
    return (x[0].reshape(Bp, Tp, D), x[1].reshape(Bs, SAMPLE_STEPS, D)[:, :Ts], rw_p, st['p_sh'],
            gd_p, st['p_cv'], st['p_mk'], st['p_mv'], rw_s, st['s_sh'], gd_s, st['s_cv'])
```

```python
import functools

import jax
import jax.numpy as jnp
from jax import lax
from jax.experimental import pallas as pl
from jax.experimental.pallas import tpu as pltpu

F32 = jnp.float32
BF16 = jnp.bfloat16

LANES = 128
SUBLANES = 8
RW_HEAD = 64
GD_HEAD = 128
GD_CONV = 4
LN_EPS = 1e-5
RW_GN_EPS = 64e-5
GD_NORM_EPS = 1e-6
L2_EPS = 1e-12

PROMPT_CHUNK = 64
SAMPLE_STEPS = SUBLANES
SAMPLE_SEQS = 8
RW_PROMPT_CHUNKS = 2
GD_PROMPT_CHUNKS = 4

VMEM_LIMIT_BYTES = 56 * 1024 * 1024

_NN = (((1,), (0,)), ((), ()))
_NT = (((1,), (1,)), ((), ()))
_TN = (((0,), (0,)), ((), ()))
_TT = (((0,), (1,)), ((), ()))


def _cparams(*sem):
    return pltpu.CompilerParams(dimension_semantics=sem, vmem_limit_bytes=VMEM_LIMIT_BYTES)


def _round_up(n, m):
    return -(-n // m) * m


def _pick_tile(n, candidates):
    for c in candidates:
        if n % c == 0:
            return c
    raise ValueError(f"no tile in {candidates} divides {n}")


def _mm_kernel(x_ref, w_ref, o_ref):
    o_ref[...] = jnp.dot(x_ref[...], w_ref[...], preferred_element_type=F32).astype(o_ref.dtype)


def matmul(x, w, layer, *, tm, tn, out_dtype=F32):
    M, K = x.shape
    N = w.shape[2]
    assert M % tm == 0 and N % tn == 0
    return pl.pallas_call(
        _mm_kernel,
        grid=(M // tm, N // tn),
        in_specs=[pl.BlockSpec((tm, K), lambda i, j: (i, 0)),
                  pl.BlockSpec((None, K, tn), lambda i, j: (layer, 0, j))],
        out_specs=pl.BlockSpec((tm, tn), lambda i, j: (i, j)),
        out_shape=jax.ShapeDtypeStruct((M, N), out_dtype),
        compiler_params=_cparams("parallel", "arbitrary"),
        name="matmul",
    )(x, w)


def _group_rows(p_ref, s_ref, n_p):
    return jnp.where(pl.program_id(0) < n_p, p_ref[...], s_ref[...])


def _group_specs(tm, width, n_p):
    return (pl.BlockSpec((tm, width), lambda i, *_: (jnp.minimum(i, n_p - 1), 0)),
            pl.BlockSpec((tm, width), lambda i, *_: (jnp.maximum(i - n_p, 0), 0)))


def _layer_norm_rows(y, g, b):
    mu = jnp.mean(y, -1, keepdims=True)
    yc = y - mu
    var = jnp.mean(yc * yc, -1, keepdims=True)
    return yc * lax.rsqrt(var + LN_EPS) * g + b


def _mix_out_kernel(ap_ref, as_ref, bp_ref, bs_ref, wa_ref, wb_ref, res_ref, g_ref, beta_ref, wq_ref, x_ref, q_ref,
                    *, alpha, n_p):
    h = jnp.dot(_group_rows(ap_ref, as_ref, n_p), wa_ref[...], preferred_element_type=F32)
    h += jnp.dot(_group_rows(bp_ref, bs_ref, n_p), wb_ref[...], preferred_element_type=F32)
    x = _layer_norm_rows(alpha * res_ref[...] + h, g_ref[...], beta_ref[...])
    x_ref[...] = x
    q_ref[...] = jnp.dot(x.astype(BF16), wq_ref[...], preferred_element_type=F32).astype(BF16)


def mix_out_ln_q(o_a, o_b, w, wq, layer, res, g, b, *, alpha, tm):
    M, D = res.shape
    Ka, Kb = o_a[0].shape[1], o_b[0].shape[1]
    Q = wq.shape[2]
    n_p = o_a[0].shape[0] // tm
    assert M % tm == 0 and o_a[0].shape[0] % tm == 0 and o_a[1].shape[0] % tm == 0
    assert Ka == Kb and w.shape[1] == Ka + Kb and o_a[0].shape[0] + o_a[1].shape[0] == M
    rows = lambda n: pl.BlockSpec((tm, n), lambda i: (i, 0))
    vec = pl.BlockSpec((1, D), lambda i: (0, 0))
    return pl.pallas_call(
        functools.partial(_mix_out_kernel, alpha=alpha, n_p=n_p),
        grid=(M // tm,),
        in_specs=[*_group_specs(tm, Ka, n_p), *_group_specs(tm, Kb, n_p),
                  pl.BlockSpec((None, Ka, D), lambda i: (layer, 0, 0)),
                  pl.BlockSpec((None, Kb, D), lambda i: (layer, 1, 0)),
                  rows(D), vec, vec,
                  pl.BlockSpec((None, D, Q), lambda i: (layer, 0, 0))],
        out_specs=[rows(D), rows(Q)],
        out_shape=[jax.ShapeDtypeStruct((M, D), F32), jax.ShapeDtypeStruct((M, Q), BF16)],
        compiler_params=_cparams("parallel"),
        name="mix_out_ln_q",
    )(*o_a, *o_b, w, w, res, g.reshape(1, D), b.reshape(1, D), wq)


def _attn_ffn_kernel(ap_ref, as_ref, wo_ref, res_ref, g2_ref, b2_ref, wg_ref, wu_ref, wd_ref, g3_ref, b3_ref,
                     o_ref, ob_ref, x_scr, xb_scr, acc_ref, *, alpha, n_p):
    f = pl.program_id(1)

    @pl.when(f == 0)
    def _():
        h = jnp.dot(_group_rows(ap_ref, as_ref, n_p), wo_ref[...], preferred_element_type=F32)
        x = _layer_norm_rows(alpha * res_ref[...] + h, g2_ref[...], b2_ref[...])
        x_scr[...] = x
        xb_scr[...] = x.astype(BF16)
        acc_ref[...] = jnp.zeros_like(acc_ref)

    x = xb_scr[...]
    gate = jnp.dot(x, wg_ref[...], preferred_element_type=F32)
    up = jnp.dot(x, wu_ref[...], preferred_element_type=F32)
    hidden = (gate * jax.nn.sigmoid(gate) * up).astype(BF16)
    acc_ref[...] += jnp.dot(hidden, wd_ref[...], preferred_element_type=F32)

    @pl.when(f == pl.num_programs(1) - 1)
    def _():
        y = _layer_norm_rows(alpha * x_scr[...] + acc_ref[...], g3_ref[...], b3_ref[...])
        o_ref[...] = y
        ob_ref[...] = y.astype(BF16)


def attn_out_ffn(att, wo, wg, wu, wd, layer, res, g2, b2, g3, b3, *, alpha, tm, tf):
    M, D = res.shape
    Q = att[0].shape[1]
    Fd = wg.shape[2]
    n_p = att[0].shape[0] // tm
    assert M % tm == 0 and Fd % tf == 0 and att[0].shape[0] % tm == 0 and att[0].shape[0] + att[1].shape[0] == M
    vec = pl.BlockSpec((1, D), lambda i, f: (0, 0))
    rows = pl.BlockSpec((tm, D), lambda i, f: (i, 0))
    return pl.pallas_call(
        functools.partial(_attn_ffn_kernel, alpha=alpha, n_p=n_p),
        grid=(M // tm, Fd // tf),
        in_specs=[*_group_specs(tm, Q, n_p),
                  pl.BlockSpec((None, Q, D), lambda i, f: (layer, 0, 0)), rows, vec, vec,
                  pl.BlockSpec((None, D, tf), lambda i, f: (layer, 0, f)),
                  pl.BlockSpec((None, D, tf), lambda i, f: (layer, 0, f)),
                  pl.BlockSpec((None, tf, D), lambda i, f: (layer, f, 0)), vec, vec],
        out_specs=[rows, rows],
        out_shape=[jax.ShapeDtypeStruct((M, D), F32), jax.ShapeDtypeStruct((M, D), BF16)],
        scratch_shapes=[pltpu.VMEM((tm, D), F32), pltpu.VMEM((tm, D), BF16), pltpu.VMEM((tm, D), F32)],
        compiler_params=_cparams("parallel", "arbitrary"),
        name="attn_out_ffn",
    )(*att, wo, res, g2.reshape(1, D), b2.reshape(1, D), wg, wu, wd, g3.reshape(1, D), b3.reshape(1, D))


def _mm(a, b, dims):
    return lax.dot_general(a.astype(BF16), b.astype(BF16), dims, preferred_element_type=F32)


def _three_bf16_terms(a):
    a1 = a.astype(BF16)
    r1 = a - a1.astype(F32)
    a2 = r1.astype(BF16)
    return a1, a2, (r1 - a2.astype(F32)).astype(BF16)


def _unit_lower_inverse(xs, c, n_live):
    row = lax.broadcasted_iota(jnp.int32, (c, c), 0)
    col = lax.broadcasted_iota(jnp.int32, (c, c), 1)
    eye = jnp.where(row == col, 1.0, 0.0).astype(F32)
    ts = [eye + x for x in xs]
    ps = list(xs)
    span = 2
    while span < n_live:
        ps = [_mm(p, p, _NN) for p in ps]
        ts = [t + _mm(t, p, _NN) for t, p in zip(ts, ps)]
        span *= 2
    return ts


def _step_in_seq(R, C):
    return lax.broadcasted_iota(jnp.int32, (R, 1), 0) & (C - 1)


def _seq_tri(R, C):
    row = lax.broadcasted_iota(jnp.int32, (R, R), 0)
    col = lax.broadcasted_iota(jnp.int32, (R, R), 1)
    return jnp.where(row >= col, jnp.where((row & -C) == (col & -C), 1.0, 0.0), 0.0).astype(BF16)


def _per_seq_rows(rows, C):
    rows = [jnp.broadcast_to(t, (C, t.shape[-1])) for t in rows]
    return rows[0] if len(rows) == 1 else jnp.concatenate(rows, axis=0)


def _delayed_rows(x, tails, j, BB, C):
    xj = pltpu.roll(x, j, 0)
    row = lax.broadcasted_iota(jnp.int32, (SUBLANES, 1), 0)
    pieces = []
    for i in range(BB):
        pieces.append(jnp.where(row < j, pltpu.roll(tails[i], j, 0), xj[i * C:i * C + SUBLANES]))
        if C > SUBLANES:
            pieces.append(xj[i * C + SUBLANES:(i + 1) * C])
    return pieces[0] if len(pieces) == 1 else jnp.concatenate(pieces, axis=0)


def _half_sums(x, N):
    low = lax.broadcasted_iota(jnp.int32, (1, LANES), 1) < N
    out = []
    for j in range(x.shape[1] // LANES):
        xv = x[:, j * LANES:(j + 1) * LANES]
        lo = jnp.sum(jnp.where(low, xv, 0.0), -1, keepdims=True)
        out += [lo, jnp.sum(jnp.where(low, 0.0, xv), -1, keepdims=True)]
    return out


def _half_bcast(cols, N):
    low = lax.broadcasted_iota(jnp.int32, (1, LANES), 1) < N
    return jnp.concatenate([jnp.where(low, cols[2 * j], cols[2 * j + 1]) for j in range(len(cols) // 2)], axis=1)


def _write_states(st_ref, s_scr):
    if len(st_ref.shape) == len(s_scr.shape):
        st_ref[...] = s_scr[...]
    else:
        for j in range(st_ref.shape[0]):
            st_ref[j] = s_scr[...]


def _rwkv_kernel(*refs, C, H, N, NS, L, t_valid):
    r_ref, k_ref, v_ref, sm_ref, prev_ref, mu_ref, vec_ref, w2_ref, a2_ref, g2_ref, s0_ref = refs[:11]
    o_ref, st_ref, last_ref, s_scr, tail_scr, o_scr = refs[-6:]
    c = pl.program_id(1)
    W = H * N
    BB = NS * L
    R = BB * C
    Rs = L * C
    sw, sa, sg = w2_ref.shape[0], a2_ref.shape[0], g2_ref.shape[0]

    @pl.when(c == 0)
    def _():
        s_scr[...] = s0_ref[...]
        tail_scr[...] = prev_ref[...]

    first = _step_in_seq(R, Rs) == 0
    vec = vec_ref[...]
    w0, a0, k_k, k_a, r_k, ln_w, ln_b = (vec[j:j + 1] for j in range(7))
    mu = mu_ref[...]
    tail = _per_seq_rows([tail_scr[s] for s in range(NS)], Rs)

    def shifted(x, lo):
        hi = lo + x.shape[1]
        prev = jnp.where(first, tail[:, lo:hi], pltpu.roll(x, 1, 0))
        return x + (prev - x) * mu[:, lo:hi]

    xr, xk, xv, xs = r_ref[...], k_ref[...], v_ref[...], sm_ref[...]
    r, k, v, ms = shifted(xr, 0), shifted(xk, W), shifted(xv, 2 * W), shifted(xs, 3 * W)
    for x, lo in ((xr, 0), (xk, W), (xv, 2 * W), (xs, 3 * W)):
        for s in range(NS):
            end = s * Rs + (L - 1) * C + t_valid
            tail_scr[s, :, lo:lo + x.shape[1]] = x[end - 1:end]

    lora_w = jnp.dot(jnp.tanh(ms[:, :sw]).astype(BF16), w2_ref[...], preferred_element_type=F32)
    lora_a = jnp.dot(ms[:, sw:sw + sa].astype(BF16), a2_ref[...], preferred_element_type=F32)
    gate = jnp.dot(jax.nn.sigmoid(ms[:, sw + sa:sw + sa + sg]).astype(BF16), g2_ref[...],
                   preferred_element_type=F32)
    lw = -jnp.exp(-jax.nn.softplus(-(w0 + lora_w)) - 0.5)
    a = jax.nn.sigmoid(a0 + lora_a)
    kk = k * k_k
    kk = kk * _half_bcast([lax.rsqrt(s + L2_EPS) for s in _half_sums(kk * kk, N)], N)
    k = k * (1.0 + (a - 1.0) * k_a)
    if t_valid < C:
        valid = _step_in_seq(R, C) < t_valid
        r, k, v, kk, lw = (jnp.where(valid, t, 0.0) for t in (r, k, v, kk, lw))

    tri = _seq_tri(R, C)
    G = sum(jnp.dot(tri, t, preferred_element_type=F32) for t in _three_bf16_terms(lw))
    mid = C // 2 - 1
    g_mid = [G[i * C + mid:i * C + mid + 1] for i in range(BB)]
    g_last = [G[(i + 1) * C - 1:(i + 1) * C] for i in range(BB)]
    Gm = G - _per_seq_rows(g_mid, C)
    e_out = jnp.exp(-Gm)
    e_tail = jnp.exp(_per_seq_rows(g_last, C) - G)
    b_in = kk * a
    A_, R_ = -kk * jnp.exp(Gm - lw), r * jnp.exp(Gm)
    B_, K_ = b_in * e_out, k * e_out
    Bt, Kt = b_in * e_tail, k * e_tail
    e_mid, e_last = [jnp.exp(t) for t in g_mid], [jnp.exp(t) for t in g_last]

    row2 = lax.broadcasted_iota(jnp.int32, (2 * C, 2 * C), 0)
    col2 = lax.broadcasted_iota(jnp.int32, (2 * C, 2 * C), 1)
    rowc = jnp.where(row2 >= C, row2 - C, row2)
    colc = jnp.where(col2 >= C, col2 - C, col2)
    keep = jnp.where(row2 >= C, rowc, rowc - 1) >= colc

    rs = lambda i: slice(i * C, (i + 1) * C)
    sl = lambda h: slice(h * N, (h + 1) * N)
    pair = lambda x, y, i, h: jnp.concatenate([x[rs(i), sl(h)], y[rs(i), sl(h)]], axis=0)
    every = [(i, h) for i in range(BB) for h in range(H)]
    AR = {ih: pair(A_, R_, *ih) for ih in every}
    M = {ih: jnp.where(keep, _mm(AR[ih], pair(B_, K_, *ih), _NT), 0.0) for ih in every}
    AV = {(i, h): _mm(M[i, h][:C, C:], v[rs(i), sl(h)], _NN) for i, h in every}
    T = dict(zip(every, _unit_lower_inverse([M[ih][:C, :C] for ih in every], C, t_valid)))
    S = {(s, h): s_scr[s, h] for s in range(NS) for h in range(H)}
    for lvl in range(L):
        ch = [(s, s * L + lvl, h) for s in range(NS) for h in range(H)]
        SR = {(i, h): _mm(AR[i, h], S[s, h] * e_mid[i][:, sl(h)], _NT) for s, i, h in ch}
        U = {(i, h): _mm(T[i, h], SR[i, h][:C] + AV[i, h], _NN) for _, i, h in ch}
        UV = {(i, h): jnp.concatenate([U[i, h], v[rs(i), sl(h)]], axis=0) for _, i, h in ch}
        O = {(i, h): SR[i, h][C:] + _mm(M[i, h][C:], UV[i, h], _NN) for _, i, h in ch}
        S = {(s, h): S[s, h] * e_last[i][:, sl(h)] + _mm(UV[i, h], pair(Bt, Kt, i, h), _TN) for s, i, h in ch}
        for _, i, h in ch:
            o_scr[rs(i), sl(h)] = O[i, h]
    for (s, h), val in S.items():
        s_scr[s, h] = val

    o = o_scr[...]
    o = o - _half_bcast([s * (1.0 / N) for s in _half_sums(o, N)], N)
    o = o * _half_bcast([lax.rsqrt(s * (1.0 / N) + RW_GN_EPS) for s in _half_sums(o * o, N)], N) * ln_w + ln_b
    bonus = _half_bcast(_half_sums(r * k * r_k, N), N) * v
    o_ref[...] = ((o + bonus) * gate).astype(o_ref.dtype)

    @pl.when(c == pl.num_programs(1) - 1)
    def _():
        _write_states(st_ref, s_scr)
        last_ref[...] = tail_scr[...]


def _state_io(s0, s0_layer, st_buf, layer, n_layers, seqs):
    blk = (None, seqs) + s0.shape[2:]
    st_in = pl.BlockSpec(blk, lambda i, c: (s0_layer, i, 0, 0, 0))
    if st_buf is None:
        every = pl.BlockSpec((n_layers, seqs) + s0.shape[2:], lambda i, c: (0, i, 0, 0, 0))
        return [st_in], [s0], every, jax.ShapeDtypeStruct((n_layers,) + s0.shape[1:], F32), None
    assert st_buf.shape == (n_layers,) + s0.shape[1:]
    st_out = pl.BlockSpec(blk, lambda i, c: (layer, i, 0, 0, 0))
    return ([st_in, pl.BlockSpec(memory_space=pl.ANY)], [s0, st_buf], st_out,
            jax.ShapeDtypeStruct(st_buf.shape, F32), 1)


def rwkv7_mixer(P, row0, prev, s0, s0_layer, st_buf, layer, n_layers, prm, *, chunk, seqs, chunks, t_valid):
    _, B, H, N, _ = s0.shape
    W = H * N
    SW = prm['mu'].shape[-1] - 3 * W
    T = prm['T']
    blk = seqs * chunks * chunk
    nc = T // (chunks * chunk)
    assert T % (chunks * chunk) == 0 and B % seqs == 0 and row0 % blk == 0 and (7 * W) % SW == 0
    assert (seqs == 1 or nc == 1) and (chunks == 1 or t_valid == chunk)
    assert 2 * N == LANES and chunk & (chunk - 1) == 0
    rb0 = row0 // blk
    col = lambda j: pl.BlockSpec((blk, W), lambda i, c: (rb0 + i * nc + c, j))
    full = lambda a: pl.BlockSpec((None,) + a.shape[1:], lambda i, c: (layer,) + (0,) * (a.ndim - 1))
    st_specs, st_args, st_out, st_shape, alias_at = _state_io(s0, s0_layer, st_buf, layer, n_layers, seqs)
    row = pl.BlockSpec((seqs, 1, 3 * W + SW), lambda i, c: (i, 0, 0))
    consts = [prm['mu'], prm['vec'], prm['w2'], prm['a2'], prm['g2']]
    n_in = 5 + len(consts)
    return pl.pallas_call(
        functools.partial(_rwkv_kernel, C=chunk, H=H, N=N, NS=seqs, L=chunks, t_valid=t_valid),
        grid=(B // seqs, nc),
        in_specs=[col(0), col(1), col(2),
                  pl.BlockSpec((blk, SW), lambda i, c: (rb0 + i * nc + c, 7 * W // SW)),
                  pl.BlockSpec((None, seqs, 1, 3 * W + SW), lambda i, c: (s0_layer, i, 0, 0))]
                 + [full(a) for a in consts] + st_specs,
        out_specs=[pl.BlockSpec((blk, W), lambda i, c: (i * nc + c, 0)), st_out, row],
        out_shape=[jax.ShapeDtypeStruct((B * T, W), BF16), st_shape,
                   jax.ShapeDtypeStruct((B, 1, 3 * W + SW), F32)],
        scratch_shapes=[pltpu.VMEM((seqs, H, N, N), F32), pltpu.VMEM((seqs, 1, 3 * W + SW), F32),
                        pltpu.VMEM((blk, W), F32)],
        input_output_aliases={} if alias_at is None else {n_in + alias_at: 1},
        compiler_params=_cparams("parallel", "arbitrary"),
        name="rwkv7_mixer",
    )(P, P, P, P, prev, *consts, *st_args)


def _gdn_kernel(*refs, C, H, N, NS, L, t_valid, gate_lane):
    q_ref, k_ref, v_ref, z_ref, sm_ref, cprev_ref, cw_ref, gate_ref, nw_ref, s0_ref = refs[:10]
    o_ref, st_ref, last_ref, s_scr, tail_scr = refs[-5:]
    c = pl.program_id(1)
    W = H * N
    BB = NS * L
    R = BB * C
    Rs = L * C

    @pl.when(c == 0)
    def _():
        s_scr[...] = s0_ref[...]
        tail_scr[...] = cprev_ref[...]

    cw = cw_ref[...]
    nw = nw_ref[...]
    tails = [tail_scr[s] for s in range(NS)]

    def conv_silu(x, lo):
        hi = lo + W
        acc = x * cw[GD_CONV - 1:GD_CONV, lo:hi]
        for j in range(1, GD_CONV):
            acc = acc + (_delayed_rows(x, [t[:, lo:hi] for t in tails], j, NS, Rs)
                         * cw[GD_CONV - 1 - j:GD_CONV - j, lo:hi])
        return acc * jax.nn.sigmoid(acc)

    xq, xk, xv = q_ref[...], k_ref[...], v_ref[...]
    q_all, k_all, v_all = conv_silu(xq, 0), conv_silu(xk, W), conv_silu(xv, 2 * W)
    for x, lo in ((xq, 0), (xk, W), (xv, 2 * W)):
        for s in range(NS):
            tail_scr[s, :, lo:lo + W] = x[(s + 1) * Rs - SUBLANES:(s + 1) * Rs]
    z_all = z_ref[...]
    gates = sm_ref[:, gate_lane:gate_lane + LANES]
    beta = jax.nn.sigmoid(gates[:, :H])
    g = -jnp.exp(gate_ref[0:1, :]) * jax.nn.softplus(gates[:, H:2 * H] + gate_ref[1:2, :])
    if t_valid < C:
        valid = _step_in_seq(R, C) < t_valid
        beta, g, q_all, k_all, v_all = (jnp.where(valid, t, 0.0) for t in (beta, g, q_all, k_all, v_all))
    tri = _seq_tri(R, C)
    g_terms = _three_bf16_terms(g)
    Gc = sum(jnp.dot(tri, t, preferred_element_type=F32) for t in g_terms)
    Gr = sum(lax.dot_general(t, tri, _TT, preferred_element_type=F32) for t in g_terms)

    row2 = lax.broadcasted_iota(jnp.int32, (2 * C, C), 0)
    col2 = lax.broadcasted_iota(jnp.int32, (2 * C, C), 1)
    keep = jnp.where(row2 >= C, row2 - C, row2 - 1) >= col2

    l2 = lambda t: t * lax.rsqrt(jnp.sum(t * t, -1, keepdims=True) + L2_EPS)
    sl = lambda h: slice(h * N, (h + 1) * N)
    qn = [l2(q_all[:, sl(h)]) * (N ** -0.5) for h in range(H)]
    kn = [l2(k_all[:, sl(h)]) for h in range(H)]

    rs = lambda i: slice(i * C, (i + 1) * C)
    every = [(i, h) for i in range(BB) for h in range(H)]
    gc = {(i, h): Gc[rs(i), h:h + 1] for i, h in every}
    g_last = {(i, h): Gc[(i + 1) * C - 1:(i + 1) * C, h:h + 1] for i, h in every}
    gc2 = {ih: jnp.concatenate([gc[ih], gc[ih]], axis=0) for ih in every}
    decay = {(i, h): jnp.where(keep, jnp.exp(jnp.where(keep, gc2[i, h] - Gr[h:h + 1, rs(i)], 0.0)), 0.0)
             for i, h in every}
    bh = {(i, h): beta[rs(i), h:h + 1] for i, h in every}
    kj = {(i, h): kn[h][rs(i)] for i, h in every}
    KQ = {(i, h): jnp.concatenate([kj[i, h] * bh[i, h], qn[h][rs(i)]], axis=0) for i, h in every}
    M = {ih: _mm(KQ[ih], kj[ih], _NT) * decay[ih] for ih in every}
    T = dict(zip(every, _unit_lower_inverse([-M[ih][:C] for ih in every], C, t_valid)))
    S = {(s, h): s_scr[s, h] for s in range(NS) for h in range(H)}
    for lvl in range(L):
        ch = [(s, s * L + lvl, h) for s in range(NS) for h in range(H)]
        SR = {(i, h): _mm(KQ[i, h] * jnp.exp(gc2[i, h]), S[s, h], _NN) for s, i, h in ch}
        v_new = {(i, h): _mm(T[i, h], v_all[rs(i), sl(h)] * bh[i, h] - SR[i, h][:C], _NN) for _, i, h in ch}
        O = {(i, h): SR[i, h][C:] + _mm(M[i, h][C:], v_new[i, h], _NN) for _, i, h in ch}
        S = {(s, h): S[s, h] * jnp.exp(g_last[i, h])
             + _mm(kj[i, h] * jnp.exp(g_last[i, h] - gc[i, h]), v_new[i, h], _TN) for s, i, h in ch}
        for _, i, h in ch:
            zh = z_all[rs(i), sl(h)]
            o = O[i, h] * lax.rsqrt(jnp.mean(O[i, h] * O[i, h], -1, keepdims=True) + GD_NORM_EPS) * nw
            o_ref[rs(i), sl(h)] = (o * (zh * jax.nn.sigmoid(zh))).astype(o_ref.dtype)
    for (s, h), val in S.items():
        s_scr[s, h] = val

    @pl.when(c == pl.num_programs(1) - 1)
    def _():
        _write_states(st_ref, s_scr)
        last_ref[...] = tail_scr[...]


def gdn_mixer(P, row0, conv_prev, s0, s0_layer, st_buf, layer, n_layers, prm, *, chunk, seqs, chunks, t_valid):
    _, B, H, N, _ = s0.shape
    W = H * N
    SW = prm['SW']
    T = prm['T']
    blk = seqs * chunks * chunk
    nc = T // (chunks * chunk)
    assert T % (chunks * chunk) == 0 and B % seqs == 0 and row0 % blk == 0 and chunk >= SUBLANES
    assert (seqs == 1 or nc == 1) and (chunks == 1 or t_valid == chunk) and 2 * H <= LANES
    assert N == LANES and chunk & (chunk - 1) == 0
    rb0 = row0 // blk
    col = lambda j: pl.BlockSpec((blk, W), lambda i, c: (rb0 + i * nc + c, j))
    full = lambda a: pl.BlockSpec((None,) + a.shape[1:], lambda i, c: (layer,) + (0,) * (a.ndim - 1))
    st_specs, st_args, st_out, st_shape, alias_at = _state_io(s0, s0_layer, st_buf, layer, n_layers, seqs)
    rows = pl.BlockSpec((seqs, SUBLANES, 3 * W), lambda i, c: (i, 0, 0))
    consts = [prm['conv_w'], prm['gate'], prm['norm_w']]
    n_in = 6 + len(consts)
    return pl.pallas_call(
        functools.partial(_gdn_kernel, C=chunk, H=H, N=N, NS=seqs, L=chunks, t_valid=t_valid,
                          gate_lane=SW - LANES),
        grid=(B // seqs, nc),
        in_specs=[col(3), col(4), col(5), col(6),
                  pl.BlockSpec((blk, SW), lambda i, c: (rb0 + i * nc + c, 7 * W // SW)),
                  pl.BlockSpec((None, seqs, SUBLANES, 3 * W), lambda i, c: (s0_layer, i, 0, 0))]
                 + [full(a) for a in consts] + st_specs,
        out_specs=[pl.BlockSpec((blk, W), lambda i, c: (i * nc + c, 0)), st_out, rows],
        out_shape=[jax.ShapeDtypeStruct((B * T, W), BF16), st_shape,
                   jax.ShapeDtypeStruct((B, SUBLANES, 3 * W), F32)],
        scratch_shapes=[pltpu.VMEM((seqs, H, N, N), F32), pltpu.VMEM((seqs, SUBLANES, 3 * W), F32)],
        input_output_aliases={} if alias_at is None else {n_in + alias_at: 1},
        compiler_params=_cparams("parallel", "arbitrary"),
        name="gdn_mixer",
    )(P, P, P, P, P, conv_prev, *consts, *st_args)


def _xattn_kernel(q_ref, k_ref, v_ref, o_ref, *, H, Dh, BB, tq, head_rows):
    scale = Dh ** -0.5
    pairs = [(b, h) for b in range(BB) for h in range(H)]
    sl = [slice(h * Dh, (h + 1) * Dh) for h in range(H)]

    def head(ref, b, h):
        if head_rows:
            return ref[b, pl.ds(h, ref.shape[1] // H, stride=H), :].astype(BF16)
        return ref[:, sl[h]].astype(BF16)

    q = [q_ref[b * tq:(b + 1) * tq, :] for b in range(BB)]
    s = [lax.dot_general(q[b][:, sl[h]], head(k_ref, b, h), _NT, preferred_element_type=F32) * scale
         for b, h in pairs]
    e = [jnp.exp(t - jnp.max(t, -1, keepdims=True)) for t in s]
    pr = [(t / jnp.sum(t, -1, keepdims=True)).astype(BF16) for t in e]
    o = [jnp.dot(pr[j], head(v_ref, b, h), preferred_element_type=F32) for j, (b, h) in enumerate(pairs)]
    for j, (b, h) in enumerate(pairs):
        o_ref[b * tq:(b + 1) * tq, sl[h]] = o[j].astype(o_ref.dtype)


def cross_attention_prompt(q, kv, *, n_mem, heads, tq, q_tiles):
    Wd = kv.shape[1] // 2
    B = kv.shape[0] // n_mem
    return pl.pallas_call(
        functools.partial(_xattn_kernel, H=heads, Dh=Wd // heads, BB=1, tq=tq, head_rows=False),
        grid=(B, q_tiles),
        in_specs=[pl.BlockSpec((tq, Wd), lambda i, t: (i * q_tiles + t, 0)),
                  pl.BlockSpec((n_mem, Wd), lambda i, t: (i, 0)),
                  pl.BlockSpec((n_mem, Wd), lambda i, t: (i, 1))],
        out_specs=pl.BlockSpec((tq, Wd), lambda i, t: (i * q_tiles + t, 0)),
        out_shape=jax.ShapeDtypeStruct((B * q_tiles * tq, Wd), BF16),
        compiler_params=_cparams("parallel", "arbitrary"),
        name="cross_attention_prompt",
    )(q, kv, kv)


def cross_attention_sample(q, row0, mem_k, mem_v, layer, *, heads, tq, seqs):
    _, B, rows, Dh = mem_k.shape
    Wd = heads * Dh
    blk = seqs * tq
    assert B % seqs == 0 and row0 % blk == 0
    rb0 = row0 // blk
    kv = pl.BlockSpec((None, seqs, rows, Dh), lambda i: (layer, i, 0, 0))
    return pl.pallas_call(
        functools.partial(_xattn_kernel, H=heads, Dh=Dh, BB=seqs, tq=tq, head_rows=True),
        grid=(B // seqs,),
        in_specs=[pl.BlockSpec((blk, Wd), lambda i: (rb0 + i, 0)), kv, kv],
        out_specs=pl.BlockSpec((blk, Wd), lambda i: (i, 0)),
        out_shape=jax.ShapeDtypeStruct((B * tq, Wd), BF16),
        compiler_params=_cparams("parallel"),
        name="cross_attention_sample",
    )(q, mem_k, mem_v)


def _lane_pad(a, width):
    return jnp.pad(a, [(0, 0)] * (a.ndim - 1) + [(0, width - a.shape[-1])])


def kernel(x_prompt, mem_prompt, x_sample, state_rwkv, state_shift, state_gdn, state_conv, cache_mem_k, cache_mem_v, w_in, mu_shift, rw_w0, rw_w2, rw_a0, rw_a2, rw_g2, rw_kk, rw_ka, rw_rk, rw_lnx_w, rw_lnx_b, gd_conv_w, gd_a_log, gd_dt_bias, gd_norm_w, w_out, ln1_g, ln1_b, xa_wq, xa_wk, xa_wv, xa_wo, ln2_g, ln2_b, ff_wg, ff_wu, ff_wd, ln3_g, ln3_b):
    depth = w_in.shape[0]
    Bp, Tp, D = x_prompt.shape
    Bs, Ts, _ = x_sample.shape
    Hr, Hg = state_rwkv.shape[2], state_gdn.shape[2]
    W = Hr * RW_HEAD
    assert W == Hg * GD_HEAD and Ts <= SAMPLE_STEPS and Ts >= GD_CONV - 1 and Tp % PROMPT_CHUNK == 0
    n_w, n_a, n_g = rw_w2.shape[1], rw_a2.shape[1], rw_g2.shape[1]
    sw, sa, sg = (_round_up(n, LANES) for n in (n_w, n_a, n_g))
    SW = sw + sa + sg + LANES
    n_mem, xa_heads = cache_mem_k.shape[2], cache_mem_k.shape[3]
    xa_width = xa_heads * cache_mem_k.shape[4]
    alpha = (2 * depth) ** 0.25
    Mp, Ms = Bp * Tp, Bs * SAMPLE_STEPS
    M = Mp + Ms
    tm = _pick_tile(M, (512, 384, 256, 128))
    tf = _pick_tile(ff_wg.shape[-1], (512, 256, 128))
    c_w, c_a, c_g = 3 * W, 3 * W + n_w, 3 * W + n_w + n_a
    c_gd = c_g + n_g

    def to_layout(t):
        parts = [t[..., :c_w], t[..., c_gd:c_gd + 4 * W],
                 _lane_pad(t[..., c_w:c_a], sw), _lane_pad(t[..., c_a:c_g], sa), _lane_pad(t[..., c_g:c_gd], sg),
                 _lane_pad(t[..., c_gd + 4 * W:], LANES)]
        return jnp.concatenate(parts, axis=-1)

    def rwkv_layout(t):
        parts = [t[..., :c_w], _lane_pad(t[..., c_w:c_a], sw), _lane_pad(t[..., c_a:c_g], sa),
                 _lane_pad(t[..., c_g:c_gd], sg + LANES)]
        return jnp.concatenate(parts, axis=-1)

    def rwkv_cols(t):
        s0 = 3 * W
        return jnp.concatenate([t[..., :3 * W], t[..., s0:s0 + n_w], t[..., s0 + sw:s0 + sw + n_a],
                                t[..., s0 + sw + sa:s0 + sw + sa + n_g]], axis=-1)

    w_proj = to_layout(w_in.astype(BF16))
    w_kv = jnp.concatenate([xa_wk, xa_wv], axis=2).astype(BF16)
    w_o, w_q, w_xo = w_out.astype(BF16), xa_wq.astype(BF16), xa_wo.astype(BF16)
    w_g, w_u, w_d = ff_wg.astype(BF16), ff_wu.astype(BF16), ff_wd.astype(BF16)

    x_s = jnp.pad(x_sample, ((0, 0), (0, SAMPLE_STEPS - Ts), (0, 0)))
    x = jnp.concatenate([x_prompt.reshape(Mp, D), x_s.reshape(Ms, D)], axis=0)
    xb = x.astype(BF16)
    mem = mem_prompt.reshape(Bp * n_mem, D).astype(BF16)
    mem_k = cache_mem_k.reshape(depth, Bs, n_mem * xa_heads, -1)
    mem_v = cache_mem_v.reshape(depth, Bs, n_mem * xa_heads, -1)
    zeros = lambda *s: jnp.zeros(s, F32)
    tq = _pick_tile(Tp, (512, 256, 128))
    rw_p = gd_p = rw_s = gd_s = None
    rw_p0, gd_p0 = zeros(1, Bp, Hr, RW_HEAD, RW_HEAD), zeros(1, Bp, Hg, GD_HEAD, GD_HEAD)
    conv_at = SUBLANES - (SAMPLE_STEPS - Ts) - (GD_CONV - 1)

    outs = {n: [] for n in ('p_sh', 'p_cv', 'p_mk', 'p_mv', 's_sh', 's_cv')}
    lora = lambda w, rows: jnp.pad(w, ((0, 0), (0, rows - w.shape[1]), (0, 0))).astype(BF16)
    rw = {'T': None, 'mu': rwkv_layout(mu_shift)[:, None],
          'vec': jnp.stack([rw_w0, rw_a0, rw_kk, rw_ka, rw_rk.reshape(depth, W), rw_lnx_w, rw_lnx_b,
                            jnp.zeros((depth, W), F32)], axis=1),
          'w2': lora(rw_w2, sw), 'a2': lora(rw_a2, sa), 'g2': lora(rw_g2, sg)}
    gd = {'T': None, 'SW': SW, 'conv_w': gd_conv_w, 'gate': jnp.stack([gd_a_log, gd_dt_bias], axis=1),
          'norm_w': gd_norm_w[:, None]}
    prev_s = rwkv_layout(state_shift)[:, :, None]
    conv_s = jnp.pad(state_conv, ((0, 0), (0, 0), (SUBLANES - (GD_CONV - 1), 0), (0, 0)))
    prev_p, conv_p = zeros(1, Bp, 1, 3 * W + SW), zeros(1, Bp, SUBLANES, 3 * W)

    for l in range(depth):
        kv = matmul(mem, w_kv, l, tm=_pick_tile(Bp * n_mem, (512, 256, 128)),
                    tn=_pick_tile(2 * xa_width, (512, 256, 128)))
        P = matmul(xb, w_proj, l, tm=_pick_tile(M, (1024, 512, 384, 256, 128)),
                   tn=_pick_tile(7 * W + SW, (768, 512, 256, 128)))

        o_rw_p, rw_p, sh_p = rwkv7_mixer(P, 0, prev_p, rw_p0, 0, rw_p, l, depth, dict(rw, T=Tp),
                                         chunk=PROMPT_CHUNK, seqs=1, chunks=RW_PROMPT_CHUNKS,
                                         t_valid=PROMPT_CHUNK)
        o_gd_p, gd_p, cv_p = gdn_mixer(P, 0, conv_p, gd_p0, 0, gd_p, l, depth, dict(gd, T=Tp),
                                       chunk=PROMPT_CHUNK, seqs=1, chunks=GD_PROMPT_CHUNKS, t_valid=PROMPT_CHUNK)
        o_rw_s, rw_s, sh_s = rwkv7_mixer(P, Mp, prev_s, state_rwkv, l, rw_s, l, depth, dict(rw, T=SAMPLE_STEPS),
                                         chunk=SAMPLE_STEPS, seqs=SAMPLE_SEQS, chunks=1, t_valid=Ts)
        o_gd_s, gd_s, cv_s = gdn_mixer(P, Mp, conv_s, state_gdn, l, gd_s, l, depth, dict(gd, T=SAMPLE_STEPS),
                                       chunk=SAMPLE_STEPS, seqs=SAMPLE_SEQS, chunks=1, t_valid=Ts)

        x, q = mix_out_ln_q((o_rw_p, o_rw_s), (o_gd_p, o_gd_s), w_o, w_q, l, x, ln1_g[l], ln1_b[l],
                            alpha=alpha, tm=tm)

        att_p = cross_attention_prompt(q, kv, n_mem=n_mem, heads=xa_heads, tq=tq, q_tiles=Tp // tq)
        att_s = cross_attention_sample(q, Mp, mem_k, mem_v, l, heads=xa_heads, tq=SAMPLE_STEPS, seqs=SAMPLE_SEQS)
        x, xb = attn_out_ffn((att_p, att_s), w_xo, w_g, w_u, w_d, l, x, ln2_g[l], ln2_b[l], ln3_g[l], ln3_b[l],
                             alpha=alpha, tm=tm, tf=tf)

        outs['p_sh'].append(rwkv_cols(sh_p[:, 0])); outs['s_sh'].append(rwkv_cols(sh_s[:, 0]))
        outs['p_cv'].append(cv_p[:, SUBLANES - (GD_CONV - 1):])
        outs['s_cv'].append(cv_s[:, conv_at:conv_at + GD_CONV - 1])
        outs['p_mk'].append(kv[:, :xa_width].reshape(Bp, n_mem, xa_heads, -1))
        outs['p_mv'].append(kv[:, xa_width:].reshape(Bp, n_mem, xa_heads, -1))

    st = {n: jnp.stack(v) for n, v in outs.items()}
    return (x[:Mp].reshape(Bp, Tp, D), x[Mp:].reshape(Bs, SAMPLE_STEPS, D)[:, :Ts], rw_p, st['p_sh'],
            gd_p, st['p_cv'], st['p_mk'], st['p_mv'], rw_s, st['s_sh'], gd_s, st['s_cv'])
```

```python
import functools

import jax
import jax.numpy as jnp
from jax import lax
from jax.experimental import pallas as pl
from jax.experimental.pallas import tpu as pltpu

F32 = jnp.float32
BF16 = jnp.bfloat16

LANES = 128
SUBLANES = 8
RW_HEAD = 64
GD_HEAD = 128
GD_CONV = 4
LN_EPS = 1e-5
RW_GN_EPS = 64e-5
GD_NORM_EPS = 1e-6
L2_EPS = 1e-12

PROMPT_CHUNK = 64
SAMPLE_STEPS = SUBLANES
SAMPLE_SEQS = 8
RW_PROMPT_CHUNKS = 2
GD_PROMPT_CHUNKS = 4

VMEM_LIMIT_BYTES = 56 * 1024 * 1024

_NN = (((1,), (0,)), ((), ()))
_NT = (((1,), (1,)), ((), ()))
_TN = (((0,), (0,)), ((), ()))
_TT = (((0,), (1,)), ((), ()))


def _cparams(*sem):
    return pltpu.CompilerParams(dimension_semantics=sem, vmem_limit_bytes=VMEM_LIMIT_BYTES)


def _round_up(n, m):
    return -(-n // m) * m


def _pick_tile(n, candidates):
    for c in candidates:
        if n % c == 0:
            return c
    raise ValueError(f"no tile in {candidates} divides {n}")


def _mm_kernel(x_ref, w_ref, o_ref):
    o_ref[...] = jnp.dot(x_ref[...], w_ref[...], preferred_element_type=F32).astype(o_ref.dtype)


def matmul(x, w, layer, *, tm, tn, out_dtype=F32):
    M, K = x.shape
    N = w.shape[2]
    assert M % tm == 0 and N % tn == 0
    return pl.pallas_call(
        _mm_kernel,
        grid=(M // tm, N // tn),
        in_specs=[pl.BlockSpec((tm, K), lambda i, j: (i, 0)),
                  pl.BlockSpec((None, K, tn), lambda i, j: (layer, 0, j))],
        out_specs=pl.BlockSpec((tm, tn), lambda i, j: (i, j)),
        out_shape=jax.ShapeDtypeStruct((M, N), out_dtype),
        compiler_params=_cparams("parallel", "arbitrary"),
        name="matmul",
    )(x, w)


def _group_rows(p_ref, s_ref, n_p):
    return jnp.where(pl.program_id(0) < n_p, p_ref[...], s_ref[...])


def _group_specs(tm, width, n_p):
    return (pl.BlockSpec((tm, width), lambda i, *_: (jnp.minimum(i, n_p - 1), 0)),
            pl.BlockSpec((tm, width), lambda i, *_: (jnp.maximum(i - n_p, 0), 0)))


def _layer_norm_rows(y, g, b):
    mu = jnp.mean(y, -1, keepdims=True)
    yc = y - mu
    var = jnp.mean(yc * yc, -1, keepdims=True)
    return yc * lax.rsqrt(var + LN_EPS) * g + b


def _mix_out_kernel(ap_ref, as_ref, bp_ref, bs_ref, wa_ref, wb_ref, res_ref, g_ref, beta_ref, wq_ref, x_ref, q_ref,
                    *, alpha, n_p):
    h = jnp.dot(_group_rows(ap_ref, as_ref, n_p), wa_ref[...], preferred_element_type=F32)
    h += jnp.dot(_group_rows(bp_ref, bs_ref, n_p), wb_ref[...], preferred_element_type=F32)
    x = _layer_norm_rows(alpha * res_ref[...] + h, g_ref[...], beta_ref[...])
    x_ref[...] = x
    q_ref[...] = jnp.dot(x.astype(BF16), wq_ref[...], preferred_element_type=F32).astype(BF16)


def mix_out_ln_q(o_a, o_b, w, wq, layer, res, g, b, *, alpha, tm):
    M, D = res.shape
    Ka, Kb = o_a[0].shape[1], o_b[0].shape[1]
    Q = wq.shape[2]
    n_p = o_a[0].shape[0] // tm
    assert M % tm == 0 and o_a[0].shape[0] % tm == 0 and o_a[1].shape[0] % tm == 0
    assert Ka == Kb and w.shape[1] == Ka + Kb and o_a[0].shape[0] + o_a[1].shape[0] == M
    rows = lambda n: pl.BlockSpec((tm, n), lambda i: (i, 0))
    vec = pl.BlockSpec((1, D), lambda i: (0, 0))
    return pl.pallas_call(
        functools.partial(_mix_out_kernel, alpha=alpha, n_p=n_p),
        grid=(M // tm,),
        in_specs=[*_group_specs(tm, Ka, n_p), *_group_specs(tm, Kb, n_p),
                  pl.BlockSpec((None, Ka, D), lambda i: (layer, 0, 0)),
                  pl.BlockSpec((None, Kb, D), lambda i: (layer, 1, 0)),
                  rows(D), vec, vec,
                  pl.BlockSpec((None, D, Q), lambda i: (layer, 0, 0))],
        out_specs=[rows(D), rows(Q)],
        out_shape=[jax.ShapeDtypeStruct((M, D), F32), jax.ShapeDtypeStruct((M, Q), BF16)],
        compiler_params=_cparams("parallel"),
        name="mix_out_ln_q",
    )(*o_a, *o_b, w, w, res, g.reshape(1, D), b.reshape(1, D), wq)


def _attn_ffn_kernel(ap_ref, as_ref, wo_ref, res_ref, g2_ref, b2_ref, wg_ref, wu_ref, wd_ref, g3_ref, b3_ref,
                     o_ref, ob_ref, x_scr, xb_scr, acc_ref, *, alpha, n_p):
    f = pl.program_id(1)

    @pl.when(f == 0)
    def _():
        h = jnp.dot(_group_rows(ap_ref, as_ref, n_p), wo_ref[...], preferred_element_type=F32)
        x = _layer_norm_rows(alpha * res_ref[...] + h, g2_ref[...], b2_ref[...])
        x_scr[...] = x
        xb_scr[...] = x.astype(BF16)
        acc_ref[...] = jnp.zeros_like(acc_ref)

    x = xb_scr[...]
    gate = jnp.dot(x, wg_ref[...], preferred_element_type=F32)
    up = jnp.dot(x, wu_ref[...], preferred_element_type=F32)
    hidden = (gate * jax.nn.sigmoid(gate) * up).astype(BF16)
    acc_ref[...] += jnp.dot(hidden, wd_ref[...], preferred_element_type=F32)

    @pl.when(f == pl.num_programs(1) - 1)
    def _():
        y = _layer_norm_rows(alpha * x_scr[...] + acc_ref[...], g3_ref[...], b3_ref[...])
        o_ref[...] = y
        ob_ref[...] = y.astype(BF16)


def attn_out_ffn(att, wo, wg, wu, wd, layer, res, g2, b2, g3, b3, *, alpha, tm, tf):
    M, D = res.shape
    Q = att[0].shape[1]
    Fd = wg.shape[2]
    n_p = att[0].shape[0] // tm
    assert M % tm == 0 and Fd % tf == 0 and att[0].shape[0] % tm == 0 and att[0].shape[0] + att[1].shape[0] == M
    vec = pl.BlockSpec((1, D), lambda i, f: (0, 0))
    rows = pl.BlockSpec((tm, D), lambda i, f: (i, 0))
    return pl.pallas_call(
        functools.partial(_attn_ffn_kernel, alpha=alpha, n_p=n_p),
        grid=(M // tm, Fd // tf),
        in_specs=[*_group_specs(tm, Q, n_p),
                  pl.BlockSpec((None, Q, D), lambda i, f: (layer, 0, 0)), rows, vec, vec,
                  pl.BlockSpec((None, D, tf), lambda i, f: (layer, 0, f)),
                  pl.BlockSpec((None, D, tf), lambda i, f: (layer, 0, f)),
                  pl.BlockSpec((None, tf, D), lambda i, f: (layer, f, 0)), vec, vec],
        out_specs=[rows, rows],
        out_shape=[jax.ShapeDtypeStruct((M, D), F32), jax.ShapeDtypeStruct((M, D), BF16)],
        scratch_shapes=[pltpu.VMEM((tm, D), F32), pltpu.VMEM((tm, D), BF16), pltpu.VMEM((tm, D), F32)],
        compiler_params=_cparams("parallel", "arbitrary"),
        name="attn_out_ffn",
    )(*att, wo, res, g2.reshape(1, D), b2.reshape(1, D), wg, wu, wd, g3.reshape(1, D), b3.reshape(1, D))


def _mm(a, b, dims):
    return lax.dot_general(a.astype(BF16), b.astype(BF16), dims, preferred_element_type=F32)


def _three_bf16_terms(a):
    a1 = a.astype(BF16)
    r1 = a - a1.astype(F32)
    a2 = r1.astype(BF16)
    return a1, a2, (r1 - a2.astype(F32)).astype(BF16)


def _unit_lower_inverse(xs, c, n_live):
    row = lax.broadcasted_iota(jnp.int32, (c, c), 0)
    col = lax.broadcasted_iota(jnp.int32, (c, c), 1)
    apart = row ^ col
    eye = jnp.where(row == col, 1.0, 0.0).astype(F32)
    ts = [eye + jnp.where((apart >> 1) == 0, x, 0.0) for x in xs]
    log_n = 1
    while (1 << log_n) < n_live:
        couples = (apart >> log_n) == 1
        xo = [jnp.where(couples, x, 0.0) for x in xs]
        ts = [t + _mm(_mm(t, x_off, _NN), t, _NN) for t, x_off in zip(ts, xo)]
        log_n += 1
    return ts


def _step_in_seq(R, C):
    return lax.broadcasted_iota(jnp.int32, (R, 1), 0) & (C - 1)


def _seq_tri(R, C):
    row = lax.broadcasted_iota(jnp.int32, (R, R), 0)
    col = lax.broadcasted_iota(jnp.int32, (R, R), 1)
    return jnp.where(row >= col, jnp.where((row & -C) == (col & -C), 1.0, 0.0), 0.0).astype(BF16)


def _per_seq_rows(rows, C):
    rows = [jnp.broadcast_to(t, (C, t.shape[-1])) for t in rows]
    return rows[0] if len(rows) == 1 else jnp.concatenate(rows, axis=0)


def _delayed_rows(x, tails, j, BB, C):
    xj = pltpu.roll(x, j, 0)
    row = lax.broadcasted_iota(jnp.int32, (SUBLANES, 1), 0)
    pieces = []
    for i in range(BB):
        pieces.append(jnp.where(row < j, pltpu.roll(tails[i], j, 0), xj[i * C:i * C + SUBLANES]))
        if C > SUBLANES:
            pieces.append(xj[i * C + SUBLANES:(i + 1) * C])
    return pieces[0] if len(pieces) == 1 else jnp.concatenate(pieces, axis=0)


def _half_sums(x, N):
    low = lax.broadcasted_iota(jnp.int32, (1, LANES), 1) < N
    out = []
    for j in range(x.shape[1] // LANES):
        xv = x[:, j * LANES:(j + 1) * LANES]
        lo = jnp.sum(jnp.where(low, xv, 0.0), -1, keepdims=True)
        out += [lo, jnp.sum(jnp.where(low, 0.0, xv), -1, keepdims=True)]
    return out


def _half_bcast(cols, N):
    low = lax.broadcasted_iota(jnp.int32, (1, LANES), 1) < N
    return jnp.concatenate([jnp.where(low, cols[2 * j], cols[2 * j + 1]) for j in range(len(cols) // 2)], axis=1)


def _write_states(st_ref, s_scr):
    if len(st_ref.shape) == len(s_scr.shape):
        st_ref[...] = s_scr[...]
    else:
        for j in range(st_ref.shape[0]):
            st_ref[j] = s_scr[...]


def _rwkv_kernel(*refs, C, H, N, NS, L, t_valid):
    r_ref, k_ref, v_ref, sm_ref, prev_ref, mu_ref, vec_ref, w2_ref, a2_ref, g2_ref, s0_ref = refs[:11]
    o_ref, st_ref, last_ref, s_scr, tail_scr, o_scr = refs[-6:]
    c = pl.program_id(1)
    W = H * N
    BB = NS * L
    R = BB * C
    Rs = L * C
    sw, sa, sg = w2_ref.shape[0], a2_ref.shape[0], g2_ref.shape[0]

    @pl.when(c == 0)
    def _():
        s_scr[...] = s0_ref[...]
        tail_scr[...] = prev_ref[...]

    first = _step_in_seq(R, Rs) == 0
    vec = vec_ref[...]
    w0, a0, k_k, k_a, r_k, ln_w, ln_b = (vec[j:j + 1] for j in range(7))
    mu = mu_ref[...]
    tail = _per_seq_rows([tail_scr[s] for s in range(NS)], Rs)

    def shifted(x, lo):
        hi = lo + x.shape[1]
        prev = jnp.where(first, tail[:, lo:hi], pltpu.roll(x, 1, 0))
        return x + (prev - x) * mu[:, lo:hi]

    xr, xk, xv, xs = r_ref[...], k_ref[...], v_ref[...], sm_ref[...]
    r, k, v, ms = shifted(xr, 0), shifted(xk, W), shifted(xv, 2 * W), shifted(xs, 3 * W)
    for x, lo in ((xr, 0), (xk, W), (xv, 2 * W), (xs, 3 * W)):
        for s in range(NS):
            end = s * Rs + (L - 1) * C + t_valid
            tail_scr[s, :, lo:lo + x.shape[1]] = x[end - 1:end]

    lora_w = jnp.dot(jnp.tanh(ms[:, :sw]).astype(BF16), w2_ref[...], preferred_element_type=F32)
    lora_a = jnp.dot(ms[:, sw:sw + sa].astype(BF16), a2_ref[...], preferred_element_type=F32)
    gate = jnp.dot(jax.nn.sigmoid(ms[:, sw + sa:sw + sa + sg]).astype(BF16), g2_ref[...],
                   preferred_element_type=F32)
    lw = -jnp.exp(-jax.nn.softplus(-(w0 + lora_w)) - 0.5)
    a = jax.nn.sigmoid(a0 + lora_a)
    kk = k * k_k
    kk = kk * _half_bcast([lax.rsqrt(s + L2_EPS) for s in _half_sums(kk * kk, N)], N)
    k = k * (1.0 + (a - 1.0) * k_a)
    if t_valid < C:
        valid = _step_in_seq(R, C) < t_valid
        r, k, v, kk, lw = (jnp.where(valid, t, 0.0) for t in (r, k, v, kk, lw))

    tri = _seq_tri(R, C)
    G = sum(jnp.dot(tri, t, preferred_element_type=F32) for t in _three_bf16_terms(lw))
    mid = C // 2 - 1
    g_mid = [G[i * C + mid:i * C + mid + 1] for i in range(BB)]
    g_last = [G[(i + 1) * C - 1:(i + 1) * C] for i in range(BB)]
    Gm = G - _per_seq_rows(g_mid, C)
    e_out = jnp.exp(-Gm)
    e_tail = jnp.exp(_per_seq_rows(g_last, C) - G)
    b_in = kk * a
    A_, R_ = -kk * jnp.exp(Gm - lw), r * jnp.exp(Gm)
    B_, K_ = b_in * e_out, k * e_out
    Bt, Kt = b_in * e_tail, k * e_tail
    e_mid, e_last = [jnp.exp(t) for t in g_mid], [jnp.exp(t) for t in g_last]

    row2 = lax.broadcasted_iota(jnp.int32, (2 * C, 2 * C), 0)
    col2 = lax.broadcasted_iota(jnp.int32, (2 * C, 2 * C), 1)
    rowc = jnp.where(row2 >= C, row2 - C, row2)
    colc = jnp.where(col2 >= C, col2 - C, col2)
    keep = jnp.where(row2 >= C, rowc, rowc - 1) >= colc

    rs = lambda i: slice(i * C, (i + 1) * C)
    sl = lambda h: slice(h * N, (h + 1) * N)
    pair = lambda x, y, i, h: jnp.concatenate([x[rs(i), sl(h)], y[rs(i), sl(h)]], axis=0)
    every = [(i, h) for i in range(BB) for h in range(H)]
    AR = {ih: pair(A_, R_, *ih) for ih in every}
    M = {ih: jnp.where(keep, _mm(AR[ih], pair(B_, K_, *ih), _NT), 0.0) for ih in every}
    AV = {(i, h): _mm(M[i, h][:C, C:], v[rs(i), sl(h)], _NN) for i, h in every}
    T = dict(zip(every, _unit_lower_inverse([M[ih][:C, :C] for ih in every], C, t_valid)))
    S = {(s, h): s_scr[s, h] for s in range(NS) for h in range(H)}
    for lvl in range(L):
        ch = [(s, s * L + lvl, h) for s in range(NS) for h in range(H)]
        SR = {(i, h): _mm(AR[i, h], S[s, h] * e_mid[i][:, sl(h)], _NT) for s, i, h in ch}
        U = {(i, h): _mm(T[i, h], SR[i, h][:C] + AV[i, h], _NN) for _, i, h in ch}
        UV = {(i, h): jnp.concatenate([U[i, h], v[rs(i), sl(h)]], axis=0) for _, i, h in ch}
        O = {(i, h): SR[i, h][C:] + _mm(M[i, h][C:], UV[i, h], _NN) for _, i, h in ch}
        S = {(s, h): S[s, h] * e_last[i][:, sl(h)] + _mm(UV[i, h], pair(Bt, Kt, i, h), _TN) for s, i, h in ch}
        for _, i, h in ch:
            o_scr[rs(i), sl(h)] = O[i, h]
    for (s, h), val in S.items():
        s_scr[s, h] = val

    o = o_scr[...]
    o = o - _half_bcast([s * (1.0 / N) for s in _half_sums(o, N)], N)
    o = o * _half_bcast([lax.rsqrt(s * (1.0 / N) + RW_GN_EPS) for s in _half_sums(o * o, N)], N) * ln_w + ln_b
    bonus = _half_bcast(_half_sums(r * k * r_k, N), N) * v
    o_ref[...] = ((o + bonus) * gate).astype(o_ref.dtype)

    @pl.when(c == pl.num_programs(1) - 1)
    def _():
        _write_states(st_ref, s_scr)
        last_ref[...] = tail_scr[...]


def _state_io(s0, s0_layer, st_buf, layer, n_layers, seqs):
    blk = (None, seqs) + s0.shape[2:]
    st_in = pl.BlockSpec(blk, lambda i, c: (s0_layer, i, 0, 0, 0))
    if st_buf is None:
        every = pl.BlockSpec((n_layers, seqs) + s0.shape[2:], lambda i, c: (0, i, 0, 0, 0))
        return [st_in], [s0], every, jax.ShapeDtypeStruct((n_layers,) + s0.shape[1:], F32), None
    assert st_buf.shape == (n_layers,) + s0.shape[1:]
    st_out = pl.BlockSpec(blk, lambda i, c: (layer, i, 0, 0, 0))
    return ([st_in, pl.BlockSpec(memory_space=pl.ANY)], [s0, st_buf], st_out,
            jax.ShapeDtypeStruct(st_buf.shape, F32), 1)


def rwkv7_mixer(P, row0, prev, s0, s0_layer, st_buf, layer, n_layers, prm, *, chunk, seqs, chunks, t_valid):
    _, B, H, N, _ = s0.shape
    W = H * N
    SW = prm['mu'].shape[-1] - 3 * W
    T = prm['T']
    blk = seqs * chunks * chunk
    nc = T // (chunks * chunk)
    assert T % (chunks * chunk) == 0 and B % seqs == 0 and row0 % blk == 0 and (7 * W) % SW == 0
    assert (seqs == 1 or nc == 1) and (chunks == 1 or t_valid == chunk)
    assert 2 * N == LANES and chunk & (chunk - 1) == 0
    rb0 = row0 // blk
    col = lambda j: pl.BlockSpec((blk, W), lambda i, c: (rb0 + i * nc + c, j))
    full = lambda a: pl.BlockSpec((None,) + a.shape[1:], lambda i, c: (layer,) + (0,) * (a.ndim - 1))
    st_specs, st_args, st_out, st_shape, alias_at = _state_io(s0, s0_layer, st_buf, layer, n_layers, seqs)
    row = pl.BlockSpec((seqs, 1, 3 * W + SW), lambda i, c: (i, 0, 0))
    consts = [prm['mu'], prm['vec'], prm['w2'], prm['a2'], prm['g2']]
    n_in = 5 + len(consts)
    return pl.pallas_call(
        functools.partial(_rwkv_kernel, C=chunk, H=H, N=N, NS=seqs, L=chunks, t_valid=t_valid),
        grid=(B // seqs, nc),
        in_specs=[col(0), col(1), col(2),
                  pl.BlockSpec((blk, SW), lambda i, c: (rb0 + i * nc + c, 7 * W // SW)),
                  pl.BlockSpec((None, seqs, 1, 3 * W + SW), lambda i, c: (s0_layer, i, 0, 0))]
                 + [full(a) for a in consts] + st_specs,
        out_specs=[pl.BlockSpec((blk, W), lambda i, c: (i * nc + c, 0)), st_out, row],
        out_shape=[jax.ShapeDtypeStruct((B * T, W), BF16), st_shape,
                   jax.ShapeDtypeStruct((B, 1, 3 * W + SW), F32)],
        scratch_shapes=[pltpu.VMEM((seqs, H, N, N), F32), pltpu.VMEM((seqs, 1, 3 * W + SW), F32),
                        pltpu.VMEM((blk, W), F32)],
        input_output_aliases={} if alias_at is None else {n_in + alias_at: 1},
        compiler_params=_cparams("parallel", "arbitrary"),
        name="rwkv7_mixer",
    )(P, P, P, P, prev, *consts, *st_args)


def _gdn_kernel(*refs, C, H, N, NS, L, t_valid, gate_lane):
    q_ref, k_ref, v_ref, z_ref, sm_ref, cprev_ref, cw_ref, gate_ref, nw_ref, s0_ref = refs[:10]
    o_ref, st_ref, last_ref, s_scr, tail_scr = refs[-5:]
    c = pl.program_id(1)
    W = H * N
    BB = NS * L
    R = BB * C
    Rs = L * C

    @pl.when(c == 0)
    def _():
        s_scr[...] = s0_ref[...]
        tail_scr[...] = cprev_ref[...]

    cw = cw_ref[...]
    nw = nw_ref[...]
    tails = [tail_scr[s] for s in range(NS)]

    def conv_silu(x, lo):
        hi = lo + W
        acc = x * cw[GD_CONV - 1:GD_CONV, lo:hi]
        for j in range(1, GD_CONV):
            acc = acc + (_delayed_rows(x, [t[:, lo:hi] for t in tails], j, NS, Rs)
                         * cw[GD_CONV - 1 - j:GD_CONV - j, lo:hi])
        return acc * jax.nn.sigmoid(acc)

    xq, xk, xv = q_ref[...], k_ref[...], v_ref[...]
    q_all, k_all, v_all = conv_silu(xq, 0), conv_silu(xk, W), conv_silu(xv, 2 * W)
    for x, lo in ((xq, 0), (xk, W), (xv, 2 * W)):
        for s in range(NS):
            tail_scr[s, :, lo:lo + W] = x[(s + 1) * Rs - SUBLANES:(s + 1) * Rs]
    z_all = z_ref[...]
    gates = sm_ref[:, gate_lane:gate_lane + LANES]
    beta = jax.nn.sigmoid(gates[:, :H])
    g = -jnp.exp(gate_ref[0:1, :]) * jax.nn.softplus(gates[:, H:2 * H] + gate_ref[1:2, :])
    if t_valid < C:
        valid = _step_in_seq(R, C) < t_valid
        beta, g, q_all, k_all, v_all = (jnp.where(valid, t, 0.0) for t in (beta, g, q_all, k_all, v_all))
    tri = _seq_tri(R, C)
    g_terms = _three_bf16_terms(g)
    Gc = sum(jnp.dot(tri, t, preferred_element_type=F32) for t in g_terms)
    Gr = sum(lax.dot_general(t, tri, _TT, preferred_element_type=F32) for t in g_terms)

    row2 = lax.broadcasted_iota(jnp.int32, (2 * C, C), 0)
    col2 = lax.broadcasted_iota(jnp.int32, (2 * C, C), 1)
    keep = jnp.where(row2 >= C, row2 - C, row2 - 1) >= col2

    l2 = lambda t: t * lax.rsqrt(jnp.sum(t * t, -1, keepdims=True) + L2_EPS)
    sl = lambda h: slice(h * N, (h + 1) * N)
    qn = [l2(q_all[:, sl(h)]) * (N ** -0.5) for h in range(H)]
    kn = [l2(k_all[:, sl(h)]) for h in range(H)]

    rs = lambda i: slice(i * C, (i + 1) * C)
    every = [(i, h) for i in range(BB) for h in range(H)]
    gc = {(i, h): Gc[rs(i), h:h + 1] for i, h in every}
    g_last = {(i, h): Gc[(i + 1) * C - 1:(i + 1) * C, h:h + 1] for i, h in every}
    gc2 = {ih: jnp.concatenate([gc[ih], gc[ih]], axis=0) for ih in every}
    decay = {(i, h): jnp.where(keep, jnp.exp(jnp.where(keep, gc2[i, h] - Gr[h:h + 1, rs(i)], 0.0)), 0.0)
             for i, h in every}
    bh = {(i, h): beta[rs(i), h:h + 1] for i, h in every}
    kj = {(i, h): kn[h][rs(i)] for i, h in every}
    KQ = {(i, h): jnp.concatenate([kj[i, h] * bh[i, h], qn[h][rs(i)]], axis=0) for i, h in every}
    M = {ih: _mm(KQ[ih], kj[ih], _NT) * decay[ih] for ih in every}
    T = dict(zip(every, _unit_lower_inverse([-M[ih][:C] for ih in every], C, t_valid)))
    S = {(s, h): s_scr[s, h] for s in range(NS) for h in range(H)}
    for lvl in range(L):
        ch = [(s, s * L + lvl, h) for s in range(NS) for h in range(H)]
        SR = {(i, h): _mm(KQ[i, h] * jnp.exp(gc2[i, h]), S[s, h], _NN) for s, i, h in ch}
        v_new = {(i, h): _mm(T[i, h], v_all[rs(i), sl(h)] * bh[i, h] - SR[i, h][:C], _NN) for _, i, h in ch}
        O = {(i, h): SR[i, h][C:] + _mm(M[i, h][C:], v_new[i, h], _NN) for _, i, h in ch}
        S = {(s, h): S[s, h] * jnp.exp(g_last[i, h])
             + _mm(kj[i, h] * jnp.exp(g_last[i, h] - gc[i, h]), v_new[i, h], _TN) for s, i, h in ch}
        for _, i, h in ch:
            zh = z_all[rs(i), sl(h)]
            o = O[i, h] * lax.rsqrt(jnp.mean(O[i, h] * O[i, h], -1, keepdims=True) + GD_NORM_EPS) * nw
            o_ref[rs(i), sl(h)] = (o * (zh * jax.nn.sigmoid(zh))).astype(o_ref.dtype)
    for (s, h), val in S.items():
        s_scr[s, h] = val

    @pl.when(c == pl.num_programs(1) - 1)
    def _():
        _write_states(st_ref, s_scr)
        last_ref[...] = tail_scr[...]


def gdn_mixer(P, row0, conv_prev, s0, s0_layer, st_buf, layer, n_layers, prm, *, chunk, seqs, chunks, t_valid):
    _, B, H, N, _ = s0.shape
    W = H * N
    SW = prm['SW']
    T = prm['T']
    blk = seqs * chunks * chunk
    nc = T // (chunks * chunk)
    assert T % (chunks * chunk) == 0 and B % seqs == 0 and row0 % blk == 0 and chunk >= SUBLANES
    assert (seqs == 1 or nc == 1) and (chunks == 1 or t_valid == chunk) and 2 * H <= LANES
    assert N == LANES and chunk & (chunk - 1) == 0
    rb0 = row0 // blk
    col = lambda j: pl.BlockSpec((blk, W), lambda i, c: (rb0 + i * nc + c, j))
    full = lambda a: pl.BlockSpec((None,) + a.shape[1:], lambda i, c: (layer,) + (0,) * (a.ndim - 1))
    st_specs, st_args, st_out, st_shape, alias_at = _state_io(s0, s0_layer, st_buf, layer, n_layers, seqs)
    rows = pl.BlockSpec((seqs, SUBLANES, 3 * W), lambda i, c: (i, 0, 0))
    consts = [prm['conv_w'], prm['gate'], prm['norm_w']]
    n_in = 6 + len(consts)
    return pl.pallas_call(
        functools.partial(_gdn_kernel, C=chunk, H=H, N=N, NS=seqs, L=chunks, t_valid=t_valid,
                          gate_lane=SW - LANES),
        grid=(B // seqs, nc),
        in_specs=[col(3), col(4), col(5), col(6),
                  pl.BlockSpec((blk, SW), lambda i, c: (rb0 + i * nc + c, 7 * W // SW)),
                  pl.BlockSpec((None, seqs, SUBLANES, 3 * W), lambda i, c: (s0_layer, i, 0, 0))]
                 + [full(a) for a in consts] + st_specs,
        out_specs=[pl.BlockSpec((blk, W), lambda i, c: (i * nc + c, 0)), st_out, rows],
        out_shape=[jax.ShapeDtypeStruct((B * T, W), BF16), st_shape,
                   jax.ShapeDtypeStruct((B, SUBLANES, 3 * W), F32)],
        scratch_shapes=[pltpu.VMEM((seqs, H, N, N), F32), pltpu.VMEM((seqs, SUBLANES, 3 * W), F32)],
        input_output_aliases={} if alias_at is None else {n_in + alias_at: 1},
        compiler_params=_cparams("parallel", "arbitrary"),
        name="gdn_mixer",
    )(P, P, P, P, P, conv_prev, *consts, *st_args)


def _xattn_kernel(q_ref, k_ref, v_ref, o_ref, *, H, Dh, BB, tq, head_rows):
    scale = Dh ** -0.5
    pairs = [(b, h) for b in range(BB) for h in range(H)]
    sl = [slice(h * Dh, (h + 1) * Dh) for h in range(H)]

    def head(ref, b, h):
        if head_rows:
            return ref[b, pl.ds(h, ref.shape[1] // H, stride=H), :].astype(BF16)
        return ref[:, sl[h]].astype(BF16)

    q = [q_ref[b * tq:(b + 1) * tq, :] for b in range(BB)]
    s = [lax.dot_general(q[b][:, sl[h]], head(k_ref, b, h), _NT, preferred_element_type=F32) * scale
         for b, h in pairs]
    e = [jnp.exp(t - jnp.max(t, -1, keepdims=True)) for t in s]
    pr = [(t / jnp.sum(t, -1, keepdims=True)).astype(BF16) for t in e]
    o = [jnp.dot(pr[j], head(v_ref, b, h), preferred_element_type=F32) for j, (b, h) in enumerate(pairs)]
    for j, (b, h) in enumerate(pairs):
        o_ref[b * tq:(b + 1) * tq, sl[h]] = o[j].astype(o_ref.dtype)


def cross_attention_prompt(q, kv, *, n_mem, heads, tq, q_tiles):
    Wd = kv.shape[1] // 2
    B = kv.shape[0] // n_mem
    return pl.pallas_call(
        functools.partial(_xattn_kernel, H=heads, Dh=Wd // heads, BB=1, tq=tq, head_rows=False),
        grid=(B, q_tiles),
        in_specs=[pl.BlockSpec((tq, Wd), lambda i, t: (i * q_tiles + t, 0)),
                  pl.BlockSpec((n_mem, Wd), lambda i, t: (i, 0)),
                  pl.BlockSpec((n_mem, Wd), lambda i, t: (i, 1))],
        out_specs=pl.BlockSpec((tq, Wd), lambda i, t: (i * q_tiles + t, 0)),
        out_shape=jax.ShapeDtypeStruct((B * q_tiles * tq, Wd), BF16),
        compiler_params=_cparams("parallel", "arbitrary"),
        name="cross_attention_prompt",
    )(q, kv, kv)


def cross_attention_sample(q, row0, mem_k, mem_v, layer, *, heads, tq, seqs):
    _, B, rows, Dh = mem_k.shape
    Wd = heads * Dh
    blk = seqs * tq
    assert B % seqs == 0 and row0 % blk == 0
    rb0 = row0 // blk
    kv = pl.BlockSpec((None, seqs, rows, Dh), lambda i: (layer, i, 0, 0))
    return pl.pallas_call(
        functools.partial(_xattn_kernel, H=heads, Dh=Dh, BB=seqs, tq=tq, head_rows=True),
        grid=(B // seqs,),
        in_specs=[pl.BlockSpec((blk, Wd), lambda i: (rb0 + i, 0)), kv, kv],
        out_specs=pl.BlockSpec((blk, Wd), lambda i: (i, 0)),
        out_shape=jax.ShapeDtypeStruct((B * tq, Wd), BF16),
        compiler_params=_cparams("parallel"),
        name="cross_attention_sample",
    )(q, mem_k, mem_v)


def _lane_pad(a, width):
    return jnp.pad(a, [(0, 0)] * (a.ndim - 1) + [(0, width - a.shape[-1])])


def kernel(x_prompt, mem_prompt, x_sample, state_rwkv, state_shift, state_gdn, state_conv, cache_mem_k, cache_mem_v, w_in, mu_shift, rw_w0, rw_w2, rw_a0, rw_a2, rw_g2, rw_kk, rw_ka, rw_rk, rw_lnx_w, rw_lnx_b, gd_conv_w, gd_a_log, gd_dt_bias, gd_norm_w, w_out, ln1_g, ln1_b, xa_wq, xa_wk, xa_wv, xa_wo, ln2_g, ln2_b, ff_wg, ff_wu, ff_wd, ln3_g, ln3_b):
    depth = w_in.shape[0]
    Bp, Tp, D = x_prompt.shape
    Bs, Ts, _ = x_sample.shape
    Hr, Hg = state_rwkv.shape[2], state_gdn.shape[2]
    W = Hr * RW_HEAD
    assert W == Hg * GD_HEAD and Ts <= SAMPLE_STEPS and Ts >= GD_CONV - 1 and Tp % PROMPT_CHUNK == 0
    n_w, n_a, n_g = rw_w2.shape[1], rw_a2.shape[1], rw_g2.shape[1]
    sw, sa, sg = (_round_up(n, LANES) for n in (n_w, n_a, n_g))
    SW = sw + sa + sg + LANES
    n_mem, xa_heads = cache_mem_k.shape[2], cache_mem_k.shape[3]
    xa_width = xa_heads * cache_mem_k.shape[4]
    alpha = (2 * depth) ** 0.25
    Mp, Ms = Bp * Tp, Bs * SAMPLE_STEPS
    M = Mp + Ms
    tm = _pick_tile(M, (512, 384, 256, 128))
    tf = _pick_tile(ff_wg.shape[-1], (512, 256, 128))
    c_w, c_a, c_g = 3 * W, 3 * W + n_w, 3 * W + n_w + n_a
    c_gd = c_g + n_g

    def to_layout(t):
        parts = [t[..., :c_w], t[..., c_gd:c_gd + 4 * W],
                 _lane_pad(t[..., c_w:c_a], sw), _lane_pad(t[..., c_a:c_g], sa), _lane_pad(t[..., c_g:c_gd], sg),
                 _lane_pad(t[..., c_gd + 4 * W:], LANES)]
        return jnp.concatenate(parts, axis=-1)

    def rwkv_layout(t):
        parts = [t[..., :c_w], _lane_pad(t[..., c_w:c_a], sw), _lane_pad(t[..., c_a:c_g], sa),
                 _lane_pad(t[..., c_g:c_gd], sg + LANES)]
        return jnp.concatenate(parts, axis=-1)

    def rwkv_cols(t):
        s0 = 3 * W
        return jnp.concatenate([t[..., :3 * W], t[..., s0:s0 + n_w], t[..., s0 + sw:s0 + sw + n_a],
                                t[..., s0 + sw + sa:s0 + sw + sa + n_g]], axis=-1)

    w_proj = to_layout(w_in.astype(BF16))
    w_kv = jnp.concatenate([xa_wk, xa_wv], axis=2).astype(BF16)
    w_o, w_q, w_xo = w_out.astype(BF16), xa_wq.astype(BF16), xa_wo.astype(BF16)
    w_g, w_u, w_d = ff_wg.astype(BF16), ff_wu.astype(BF16), ff_wd.astype(BF16)

    x_s = jnp.pad(x_sample, ((0, 0), (0, SAMPLE_STEPS - Ts), (0, 0)))
    x = jnp.concatenate([x_prompt.reshape(Mp, D), x_s.reshape(Ms, D)], axis=0)
    xb = x.astype(BF16)
    mem = mem_prompt.reshape(Bp * n_mem, D).astype(BF16)
    mem_k = cache_mem_k.reshape(depth, Bs, n_mem * xa_heads, -1)
    mem_v = cache_mem_v.reshape(depth, Bs, n_mem * xa_heads, -1)
    zeros = lambda *s: jnp.zeros(s, F32)
    tq = _pick_tile(Tp, (512, 256, 128))
    rw_p = gd_p = rw_s = gd_s = None
    rw_p0, gd_p0 = zeros(1, Bp, Hr, RW_HEAD, RW_HEAD), zeros(1, Bp, Hg, GD_HEAD, GD_HEAD)
    conv_at = SUBLANES - (SAMPLE_STEPS - Ts) - (GD_CONV - 1)

    outs = {n: [] for n in ('p_sh', 'p_cv', 'p_mk', 'p_mv', 's_sh', 's_cv')}
    lora = lambda w, rows: jnp.pad(w, ((0, 0), (0, rows - w.shape[1]), (0, 0))).astype(BF16)
    rw = {'T': None, 'mu': rwkv_layout(mu_shift)[:, None],
          'vec': jnp.stack([rw_w0, rw_a0, rw_kk, rw_ka, rw_rk.reshape(depth, W), rw_lnx_w, rw_lnx_b,
                            jnp.zeros((depth, W), F32)], axis=1),
          'w2': lora(rw_w2, sw), 'a2': lora(rw_a2, sa), 'g2': lora(rw_g2, sg)}
    gd = {'T': None, 'SW': SW, 'conv_w': gd_conv_w, 'gate': jnp.stack([gd_a_log, gd_dt_bias], axis=1),
          'norm_w': gd_norm_w[:, None]}
    prev_s = rwkv_layout(state_shift)[:, :, None]
    conv_s = jnp.pad(state_conv, ((0, 0), (0, 0), (SUBLANES - (GD_CONV - 1), 0), (0, 0)))
    prev_p, conv_p = zeros(1, Bp, 1, 3 * W + SW), zeros(1, Bp, SUBLANES, 3 * W)

    for l in range(depth):
        kv = matmul(mem, w_kv, l, tm=_pick_tile(Bp * n_mem, (512, 256, 128)),
                    tn=_pick_tile(2 * xa_width, (512, 256, 128)))
        P = matmul(xb, w_proj, l, tm=_pick_tile(M, (1024, 512, 384, 256, 128)),
                   tn=_pick_tile(7 * W + SW, (768, 512, 256, 128)))

        o_rw_p, rw_p, sh_p = rwkv7_mixer(P, 0, prev_p, rw_p0, 0, rw_p, l, depth, dict(rw, T=Tp),
                                         chunk=PROMPT_CHUNK, seqs=1, chunks=RW_PROMPT_CHUNKS,
                                         t_valid=PROMPT_CHUNK)
        o_gd_p, gd_p, cv_p = gdn_mixer(P, 0, conv_p, gd_p0, 0, gd_p, l, depth, dict(gd, T=Tp),
                                       chunk=PROMPT_CHUNK, seqs=1, chunks=GD_PROMPT_CHUNKS, t_valid=PROMPT_CHUNK)
        o_rw_s, rw_s, sh_s = rwkv7_mixer(P, Mp, prev_s, state_rwkv, l, rw_s, l, depth, dict(rw, T=SAMPLE_STEPS),
                                         chunk=SAMPLE_STEPS, seqs=SAMPLE_SEQS, chunks=1, t_valid=Ts)
        o_gd_s, gd_s, cv_s = gdn_mixer(P, Mp, conv_s, state_gdn, l, gd_s, l, depth, dict(gd, T=SAMPLE_STEPS),
                                       chunk=SAMPLE_STEPS, seqs=SAMPLE_SEQS, chunks=1, t_valid=Ts)

        x, q = mix_out_ln_q((o_rw_p, o_rw_s), (o_gd_p, o_gd_s), w_o, w_q, l, x, ln1_g[l], ln1_b[l],
                            alpha=alpha, tm=tm)

        att_p = cross_attention_prompt(q, kv, n_mem=n_mem, heads=xa_heads, tq=tq, q_tiles=Tp // tq)
        att_s = cross_attention_sample(q, Mp, mem_k, mem_v, l, heads=xa_heads, tq=SAMPLE_STEPS, seqs=SAMPLE_SEQS)
        x, xb = attn_out_ffn((att_p, att_s), w_xo, w_g, w_u, w_d, l, x, ln2_g[l], ln2_b[l], ln3_g[l], ln3_b[l],
                             alpha=alpha, tm=tm, tf=tf)

        outs['p_sh'].append(rwkv_cols(sh_p[:, 0])); outs['s_sh'].append(rwkv_cols(sh_s[:, 0]))
        outs['p_cv'].append(cv_p[:, SUBLANES - (GD_CONV - 1):])
        outs['s_cv'].append(cv_s[:, conv_at:conv_at + GD_CONV - 1])
        outs['p_mk'].append(kv[:, :xa_width].reshape(Bp, n_mem, xa_heads, -1))
        outs['p_mv'].append(kv[:, xa_width:].reshape(Bp, n_mem, xa_heads, -1))

    st = {n: jnp.stack(v) for n, v in outs.items()}
    return (x[:Mp].reshape(Bp, Tp, D), x[Mp:].reshape(Bs, SAMPLE_STEPS, D)[:, :Ts], rw_p, st['p_sh'],
            gd_p, st['p_cv'], st['p_mk'], st['p_mv'], rw_s, st['s_sh'], gd_s, st['s_cv'])
```

```python
import functools

import jax
import jax.numpy as jnp
from jax import lax
from jax.experimental import pallas as pl
from jax.experimental.pallas import tpu as pltpu

F32 = jnp.float32
BF16 = jnp.bfloat16

LANES = 128
SUBLANES = 8
RW_HEAD = 64
GD_HEAD = 128
GD_CONV = 4
LN_EPS = 1e-5
RW_GN_EPS = 64e-5
GD_NORM_EPS = 1e-6
L2_EPS = 1e-12

PROMPT_CHUNK = 64
SAMPLE_STEPS = SUBLANES
SAMPLE_SEQS = 8
RW_PROMPT_CHUNKS = 4
GD_PROMPT_CHUNKS = 8

VMEM_LIMIT_BYTES = 56 * 1024 * 1024

_NN = (((1,), (0,)), ((), ()))
_NT = (((1,), (1,)), ((), ()))
_TN = (((0,), (0,)), ((), ()))
_TT = (((0,), (1,)), ((), ()))


def _cparams(*sem):
    return pltpu.CompilerParams(dimension_semantics=sem, vmem_limit_bytes=VMEM_LIMIT_BYTES)


def _round_up(n, m):
    return -(-n // m) * m


def _pick_tile(n, candidates):
    for c in candidates:
        if n % c == 0:
            return c
    raise ValueError(f"no tile in {candidates} divides {n}")


def _mm_kernel(x_ref, w_ref, o_ref):
    o_ref[...] = jnp.dot(x_ref[...], w_ref[...], preferred_element_type=F32).astype(o_ref.dtype)


def matmul(x, w, layer, *, tm, tn, out_dtype=F32):
    M, K = x.shape
    N = w.shape[2]
    assert M % tm == 0 and N % tn == 0
    return pl.pallas_call(
        _mm_kernel,
        grid=(M // tm, N // tn),
        in_specs=[pl.BlockSpec((tm, K), lambda i, j: (i, 0)),
                  pl.BlockSpec((None, K, tn), lambda i, j: (layer, 0, j))],
        out_specs=pl.BlockSpec((tm, tn), lambda i, j: (i, j)),
        out_shape=jax.ShapeDtypeStruct((M, N), out_dtype),
        compiler_params=_cparams("parallel", "arbitrary"),
        name="matmul",
    )(x, w)


def _group_rows(p_ref, s_ref, n_p):
    return jnp.where(pl.program_id(0) < n_p, p_ref[...], s_ref[...])


def _group_specs(tm, width, n_p):
    return (pl.BlockSpec((tm, width), lambda i, *_: (jnp.minimum(i, n_p - 1), 0)),
            pl.BlockSpec((tm, width), lambda i, *_: (jnp.maximum(i - n_p, 0), 0)))


def _layer_norm_rows(y, g, b):
    mu = jnp.mean(y, -1, keepdims=True)
    yc = y - mu
    var = jnp.mean(yc * yc, -1, keepdims=True)
    return yc * lax.rsqrt(var + LN_EPS) * g + b


def _mix_out_kernel(ap_ref, as_ref, bp_ref, bs_ref, wa_ref, wb_ref, res_ref, g_ref, beta_ref, wq_ref, x_ref, q_ref,
                    *, alpha, n_p):
    h = jnp.dot(_group_rows(ap_ref, as_ref, n_p), wa_ref[...], preferred_element_type=F32)
    h += jnp.dot(_group_rows(bp_ref, bs_ref, n_p), wb_ref[...], preferred_element_type=F32)
    x = _layer_norm_rows(alpha * res_ref[...] + h, g_ref[...], beta_ref[...])
    x_ref[...] = x
    q_ref[...] = jnp.dot(x.astype(BF16), wq_ref[...], preferred_element_type=F32).astype(BF16)


def mix_out_ln_q(o_a, o_b, w, wq, layer, res, g, b, *, alpha, tm):
    M, D = res.shape
    Ka, Kb = o_a[0].shape[1], o_b[0].shape[1]
    Q = wq.shape[2]
    n_p = o_a[0].shape[0] // tm
    assert M % tm == 0 and o_a[0].shape[0] % tm == 0 and o_a[1].shape[0] % tm == 0
    assert Ka == Kb and w.shape[1] == Ka + Kb and o_a[0].shape[0] + o_a[1].shape[0] == M
    rows = lambda n: pl.BlockSpec((tm, n), lambda i: (i, 0))
    vec = pl.BlockSpec((1, D), lambda i: (0, 0))
    return pl.pallas_call(
        functools.partial(_mix_out_kernel, alpha=alpha, n_p=n_p),
        grid=(M // tm,),
        in_specs=[*_group_specs(tm, Ka, n_p), *_group_specs(tm, Kb, n_p),
                  pl.BlockSpec((None, Ka, D), lambda i: (layer, 0, 0)),
                  pl.BlockSpec((None, Kb, D), lambda i: (layer, 1, 0)),
                  rows(D), vec, vec,
                  pl.BlockSpec((None, D, Q), lambda i: (layer, 0, 0))],
        out_specs=[rows(D), rows(Q)],
        out_shape=[jax.ShapeDtypeStruct((M, D), F32), jax.ShapeDtypeStruct((M, Q), BF16)],
        compiler_params=_cparams("parallel"),
        name="mix_out_ln_q",
    )(*o_a, *o_b, w, w, res, g.reshape(1, D), b.reshape(1, D), wq)


def _attn_ffn_kernel(ap_ref, as_ref, wo_ref, res_ref, g2_ref, b2_ref, wg_ref, wu_ref, wd_ref, g3_ref, b3_ref,
                     o_ref, ob_ref, x_scr, xb_scr, acc_ref, *, alpha, n_p):
    f = pl.program_id(1)

    @pl.when(f == 0)
    def _():
        h = jnp.dot(_group_rows(ap_ref, as_ref, n_p), wo_ref[...], preferred_element_type=F32)
        x = _layer_norm_rows(alpha * res_ref[...] + h, g2_ref[...], b2_ref[...])
        x_scr[...] = x
        xb_scr[...] = x.astype(BF16)
        acc_ref[...] = jnp.zeros_like(acc_ref)

    x = xb_scr[...]
    gate = jnp.dot(x, wg_ref[...], preferred_element_type=F32)
    up = jnp.dot(x, wu_ref[...], preferred_element_type=F32)
    hidden = (gate * jax.nn.sigmoid(gate) * up).astype(BF16)
    acc_ref[...] += jnp.dot(hidden, wd_ref[...], preferred_element_type=F32)

    @pl.when(f == pl.num_programs(1) - 1)
    def _():
        y = _layer_norm_rows(alpha * x_scr[...] + acc_ref[...], g3_ref[...], b3_ref[...])
        o_ref[...] = y
        ob_ref[...] = y.astype(BF16)


def attn_out_ffn(att, wo, wg, wu, wd, layer, res, g2, b2, g3, b3, *, alpha, tm, tf):
    M, D = res.shape
    Q = att[0].shape[1]
    Fd = wg.shape[2]
    n_p = att[0].shape[0] // tm
    assert M % tm == 0 and Fd % tf == 0 and att[0].shape[0] % tm == 0 and att[0].shape[0] + att[1].shape[0] == M
    vec = pl.BlockSpec((1, D), lambda i, f: (0, 0))
    rows = pl.BlockSpec((tm, D), lambda i, f: (i, 0))
    return pl.pallas_call(
        functools.partial(_attn_ffn_kernel, alpha=alpha, n_p=n_p),
        grid=(M // tm, Fd // tf),
        in_specs=[*_group_specs(tm, Q, n_p),
                  pl.BlockSpec((None, Q, D), lambda i, f: (layer, 0, 0)), rows, vec, vec,
                  pl.BlockSpec((None, D, tf), lambda i, f: (layer, 0, f)),
                  pl.BlockSpec((None, D, tf), lambda i, f: (layer, 0, f)),
                  pl.BlockSpec((None, tf, D), lambda i, f: (layer, f, 0)), vec, vec],
        out_specs=[rows, rows],
        out_shape=[jax.ShapeDtypeStruct((M, D), F32), jax.ShapeDtypeStruct((M, D), BF16)],
        scratch_shapes=[pltpu.VMEM((tm, D), F32), pltpu.VMEM((tm, D), BF16), pltpu.VMEM((tm, D), F32)],
        compiler_params=_cparams("parallel", "arbitrary"),
        name="attn_out_ffn",
    )(*att, wo, res, g2.reshape(1, D), b2.reshape(1, D), wg, wu, wd, g3.reshape(1, D), b3.reshape(1, D))


def _mm(a, b, dims):
    return lax.dot_general(a.astype(BF16), b.astype(BF16), dims, preferred_element_type=F32)


def _three_bf16_terms(a):
    a1 = a.astype(BF16)
    r1 = a - a1.astype(F32)
    a2 = r1.astype(BF16)
    return a1, a2, (r1 - a2.astype(F32)).astype(BF16)


def _unit_lower_inverse(xs, c, n_live):
    row = lax.broadcasted_iota(jnp.int32, (c, c), 0)
    col = lax.broadcasted_iota(jnp.int32, (c, c), 1)
    apart = row ^ col
    eye = jnp.where(row == col, 1.0, 0.0).astype(F32)
    ts = [eye + jnp.where((apart >> 1) == 0, x, 0.0) for x in xs]
    log_n = 1
    while (1 << log_n) < n_live:
        couples = (apart >> log_n) == 1
        xo = [jnp.where(couples, x, 0.0) for x in xs]
        ts = [t + _mm(_mm(t, x_off, _NN), t, _NN) for t, x_off in zip(ts, xo)]
        log_n += 1
    return ts


def _step_in_seq(R, C):
    return lax.broadcasted_iota(jnp.int32, (R, 1), 0) & (C - 1)


def _seq_tri(R, C):
    row = lax.broadcasted_iota(jnp.int32, (R, R), 0)
    col = lax.broadcasted_iota(jnp.int32, (R, R), 1)
    return jnp.where(row >= col, jnp.where((row & -C) == (col & -C), 1.0, 0.0), 0.0).astype(BF16)


def _per_seq_rows(rows, C):
    rows = [jnp.broadcast_to(t, (C, t.shape[-1])) for t in rows]
    return rows[0] if len(rows) == 1 else jnp.concatenate(rows, axis=0)


def _delayed_rows(x, tails, j, BB, C):
    xj = pltpu.roll(x, j, 0)
    row = lax.broadcasted_iota(jnp.int32, (SUBLANES, 1), 0)
    pieces = []
    for i in range(BB):
        pieces.append(jnp.where(row < j, pltpu.roll(tails[i], j, 0), xj[i * C:i * C + SUBLANES]))
        if C > SUBLANES:
            pieces.append(xj[i * C + SUBLANES:(i + 1) * C])
    return pieces[0] if len(pieces) == 1 else jnp.concatenate(pieces, axis=0)


def _half_sums(x, N):
    low = lax.broadcasted_iota(jnp.int32, (1, LANES), 1) < N
    out = []
    for j in range(x.shape[1] // LANES):
        xv = x[:, j * LANES:(j + 1) * LANES]
        lo = jnp.sum(jnp.where(low, xv, 0.0), -1, keepdims=True)
        out += [lo, jnp.sum(jnp.where(low, 0.0, xv), -1, keepdims=True)]
    return out


def _half_bcast(cols, N):
    low = lax.broadcasted_iota(jnp.int32, (1, LANES), 1) < N
    return jnp.concatenate([jnp.where(low, cols[2 * j], cols[2 * j + 1]) for j in range(len(cols) // 2)], axis=1)


def _write_states(st_ref, s_scr):
    if len(st_ref.shape) == len(s_scr.shape):
        st_ref[...] = s_scr[...]
    else:
        for j in range(st_ref.shape[0]):
            st_ref[j] = s_scr[...]


def _rwkv_kernel(*refs, C, H, N, NS, L, t_valid):
    r_ref, k_ref, v_ref, sm_ref, prev_ref, mu_ref, vec_ref, w2_ref, a2_ref, g2_ref, s0_ref = refs[:11]
    o_ref, st_ref, last_ref, s_scr, tail_scr, o_scr = refs[-6:]
    c = pl.program_id(1)
    W = H * N
    BB = NS * L
    R = BB * C
    Rs = L * C
    sw, sa, sg = w2_ref.shape[0], a2_ref.shape[0], g2_ref.shape[0]

    @pl.when(c == 0)
    def _():
        s_scr[...] = s0_ref[...]
        tail_scr[...] = prev_ref[...]

    first = _step_in_seq(R, Rs) == 0
    vec = vec_ref[...]
    w0, a0, k_k, k_a, r_k, ln_w, ln_b = (vec[j:j + 1] for j in range(7))
    mu = mu_ref[...]
    tail = _per_seq_rows([tail_scr[s] for s in range(NS)], Rs)

    def shifted(x, lo):
        hi = lo + x.shape[1]
        prev = jnp.where(first, tail[:, lo:hi], pltpu.roll(x, 1, 0))
        return x + (prev - x) * mu[:, lo:hi]

    xr, xk, xv, xs = r_ref[...], k_ref[...], v_ref[...], sm_ref[...]
    r, k, v, ms = shifted(xr, 0), shifted(xk, W), shifted(xv, 2 * W), shifted(xs, 3 * W)
    for x, lo in ((xr, 0), (xk, W), (xv, 2 * W), (xs, 3 * W)):
        for s in range(NS):
            end = s * Rs + (L - 1) * C + t_valid
            tail_scr[s, :, lo:lo + x.shape[1]] = x[end - 1:end]

    lora_w = jnp.dot(jnp.tanh(ms[:, :sw]).astype(BF16), w2_ref[...], preferred_element_type=F32)
    lora_a = jnp.dot(ms[:, sw:sw + sa].astype(BF16), a2_ref[...], preferred_element_type=F32)
    gate = jnp.dot(jax.nn.sigmoid(ms[:, sw + sa:sw + sa + sg]).astype(BF16), g2_ref[...],
                   preferred_element_type=F32)
    lw = -jnp.exp(-jax.nn.softplus(-(w0 + lora_w)) - 0.5)
    a = jax.nn.sigmoid(a0 + lora_a)
    kk = k * k_k
    kk = kk * _half_bcast([lax.rsqrt(s + L2_EPS) for s in _half_sums(kk * kk, N)], N)
    k = k * (1.0 + (a - 1.0) * k_a)
    if t_valid < C:
        valid = _step_in_seq(R, C) < t_valid
        r, k, v, kk, lw = (jnp.where(valid, t, 0.0) for t in (r, k, v, kk, lw))

    tri = _seq_tri(R, C)
    G = sum(jnp.dot(tri, t, preferred_element_type=F32) for t in _three_bf16_terms(lw))
    mid = C // 2 - 1
    g_mid = [G[i * C + mid:i * C + mid + 1] for i in range(BB)]
    g_last = [G[(i + 1) * C - 1:(i + 1) * C] for i in range(BB)]
    Gm = G - _per_seq_rows(g_mid, C)
    e_out = jnp.exp(-Gm)
    e_tail = jnp.exp(_per_seq_rows(g_last, C) - G)
    b_in = kk * a
    A_, R_ = -kk * jnp.exp(Gm - lw), r * jnp.exp(Gm)
    B_, K_ = b_in * e_out, k * e_out
    Bt, Kt = b_in * e_tail, k * e_tail
    e_mid, e_last = [jnp.exp(t) for t in g_mid], [jnp.exp(t) for t in g_last]

    row2 = lax.broadcasted_iota(jnp.int32, (2 * C, 2 * C), 0)
    col2 = lax.broadcasted_iota(jnp.int32, (2 * C, 2 * C), 1)
    rowc = jnp.where(row2 >= C, row2 - C, row2)
    colc = jnp.where(col2 >= C, col2 - C, col2)
    keep = jnp.where(row2 >= C, rowc, rowc - 1) >= colc

    rs = lambda i: slice(i * C, (i + 1) * C)
    sl = lambda h: slice(h * N, (h + 1) * N)
    pair = lambda x, y, i, h: jnp.concatenate([x[rs(i), sl(h)], y[rs(i), sl(h)]], axis=0)
    every = [(i, h) for i in range(BB) for h in range(H)]
    AR = {ih: pair(A_, R_, *ih) for ih in every}
    M = {ih: jnp.where(keep, _mm(AR[ih], pair(B_, K_, *ih), _NT), 0.0) for ih in every}
    AV = {(i, h): _mm(M[i, h][:C, C:], v[rs(i), sl(h)], _NN) for i, h in every}
    T = dict(zip(every, _unit_lower_inverse([M[ih][:C, :C] for ih in every], C, t_valid)))
    S = {(s, h): s_scr[s, h] for s in range(NS) for h in range(H)}
    for lvl in range(L):
        ch = [(s, s * L + lvl, h) for s in range(NS) for h in range(H)]
        SR = {(i, h): _mm(AR[i, h], S[s, h] * e_mid[i][:, sl(h)], _NT) for s, i, h in ch}
        U = {(i, h): _mm(T[i, h], SR[i, h][:C] + AV[i, h], _NN) for _, i, h in ch}
        UV = {(i, h): jnp.concatenate([U[i, h], v[rs(i), sl(h)]], axis=0) for _, i, h in ch}
        O = {(i, h): SR[i, h][C:] + _mm(M[i, h][C:], UV[i, h], _NN) for _, i, h in ch}
        S = {(s, h): S[s, h] * e_last[i][:, sl(h)] + _mm(UV[i, h], pair(Bt, Kt, i, h), _TN) for s, i, h in ch}
        for _, i, h in ch:
            o_scr[rs(i), sl(h)] = O[i, h]
    for (s, h), val in S.items():
        s_scr[s, h] = val

    o = o_scr[...]
    o = o - _half_bcast([s * (1.0 / N) for s in _half_sums(o, N)], N)
    o = o * _half_bcast([lax.rsqrt(s * (1.0 / N) + RW_GN_EPS) for s in _half_sums(o * o, N)], N) * ln_w + ln_b
    bonus = _half_bcast(_half_sums(r * k * r_k, N), N) * v
    o_ref[...] = ((o + bonus) * gate).astype(o_ref.dtype)

    @pl.when(c == pl.num_programs(1) - 1)
    def _():
        _write_states(st_ref, s_scr)
        last_ref[...] = tail_scr[...]


def _state_io(s0, s0_layer, st_buf, layer, n_layers, seqs):
    blk = (None, seqs) + s0.shape[2:]
    st_in = pl.BlockSpec(blk, lambda i, c: (s0_layer, i, 0, 0, 0))
    if st_buf is None:
        every = pl.BlockSpec((n_layers, seqs) + s0.shape[2:], lambda i, c: (0, i, 0, 0, 0))
        return [st_in], [s0], every, jax.ShapeDtypeStruct((n_layers,) + s0.shape[1:], F32), None
    assert st_buf.shape == (n_layers,) + s0.shape[1:]
    st_out = pl.BlockSpec(blk, lambda i, c: (layer, i, 0, 0, 0))
    return ([st_in, pl.BlockSpec(memory_space=pl.ANY)], [s0, st_buf], st_out,
            jax.ShapeDtypeStruct(st_buf.shape, F32), 1)


def rwkv7_mixer(P, row0, prev, s0, s0_layer, st_buf, layer, n_layers, prm, *, chunk, seqs, chunks, t_valid):
    _, B, H, N, _ = s0.shape
    W = H * N
    SW = prm['mu'].shape[-1] - 3 * W
    T = prm['T']
    blk = seqs * chunks * chunk
    nc = T // (chunks * chunk)
    assert T % (chunks * chunk) == 0 and B % seqs == 0 and row0 % blk == 0 and (7 * W) % SW == 0
    assert (seqs == 1 or nc == 1) and (chunks == 1 or t_valid == chunk)
    assert 2 * N == LANES and chunk & (chunk - 1) == 0
    rb0 = row0 // blk
    col = lambda j: pl.BlockSpec((blk, W), lambda i, c: (rb0 + i * nc + c, j))
    full = lambda a: pl.BlockSpec((None,) + a.shape[1:], lambda i, c: (layer,) + (0,) * (a.ndim - 1))
    st_specs, st_args, st_out, st_shape, alias_at = _state_io(s0, s0_layer, st_buf, layer, n_layers, seqs)
    row = pl.BlockSpec((seqs, 1, 3 * W + SW), lambda i, c: (i, 0, 0))
    consts = [prm['mu'], prm['vec'], prm['w2'], prm['a2'], prm['g2']]
    n_in = 5 + len(consts)
    return pl.pallas_call(
        functools.partial(_rwkv_kernel, C=chunk, H=H, N=N, NS=seqs, L=chunks, t_valid=t_valid),
        grid=(B // seqs, nc),
        in_specs=[col(0), col(1), col(2),
                  pl.BlockSpec((blk, SW), lambda i, c: (rb0 + i * nc + c, 7 * W // SW)),
                  pl.BlockSpec((None, seqs, 1, 3 * W + SW), lambda i, c: (s0_layer, i, 0, 0))]
                 + [full(a) for a in consts] + st_specs,
        out_specs=[pl.BlockSpec((blk, W), lambda i, c: (i * nc + c, 0)), st_out, row],
        out_shape=[jax.ShapeDtypeStruct((B * T, W), BF16), st_shape,
                   jax.ShapeDtypeStruct((B, 1, 3 * W + SW), F32)],
        scratch_shapes=[pltpu.VMEM((seqs, H, N, N), F32), pltpu.VMEM((seqs, 1, 3 * W + SW), F32),
                        pltpu.VMEM((blk, W), F32)],
        input_output_aliases={} if alias_at is None else {n_in + alias_at: 1},
        compiler_params=_cparams("parallel", "arbitrary"),
        name="rwkv7_mixer",
    )(P, P, P, P, prev, *consts, *st_args)


def _gdn_kernel(*refs, C, H, N, NS, L, t_valid, gate_lane):
    q_ref, k_ref, v_ref, z_ref, sm_ref, cprev_ref, cw_ref, gate_ref, nw_ref, s0_ref = refs[:10]
    o_ref, st_ref, last_ref, s_scr, tail_scr = refs[-5:]
    c = pl.program_id(1)
    W = H * N
    BB = NS * L
    R = BB * C
    Rs = L * C

    @pl.when(c == 0)
    def _():
        s_scr[...] = s0_ref[...]
        tail_scr[...] = cprev_ref[...]

    cw = cw_ref[...]
    nw = nw_ref[...]
    tails = [tail_scr[s] for s in range(NS)]

    def conv_silu(x, lo):
        hi = lo + W
        acc = x * cw[GD_CONV - 1:GD_CONV, lo:hi]
        for j in range(1, GD_CONV):
            acc = acc + (_delayed_rows(x, [t[:, lo:hi] for t in tails], j, NS, Rs)
                         * cw[GD_CONV - 1 - j:GD_CONV - j, lo:hi])
        return acc * jax.nn.sigmoid(acc)

    xq, xk, xv = q_ref[...], k_ref[...], v_ref[...]
    q_all, k_all, v_all = conv_silu(xq, 0), conv_silu(xk, W), conv_silu(xv, 2 * W)
    for x, lo in ((xq, 0), (xk, W), (xv, 2 * W)):
        for s in range(NS):
            tail_scr[s, :, lo:lo + W] = x[(s + 1) * Rs - SUBLANES:(s + 1) * Rs]
    z_all = z_ref[...]
    gates = sm_ref[:, gate_lane:gate_lane + LANES]
    beta = jax.nn.sigmoid(gates[:, :H])
    g = -jnp.exp(gate_ref[0:1, :]) * jax.nn.softplus(gates[:, H:2 * H] + gate_ref[1:2, :])
    if t_valid < C:
        valid = _step_in_seq(R, C) < t_valid
        beta, g, q_all, k_all, v_all = (jnp.where(valid, t, 0.0) for t in (beta, g, q_all, k_all, v_all))
    tri = _seq_tri(R, C)
    g_terms = _three_bf16_terms(g)
    Gc = sum(jnp.dot(tri, t, preferred_element_type=F32) for t in g_terms)
    Gr = sum(lax.dot_general(t, tri, _TT, preferred_element_type=F32) for t in g_terms)

    row2 = lax.broadcasted_iota(jnp.int32, (2 * C, C), 0)
    col2 = lax.broadcasted_iota(jnp.int32, (2 * C, C), 1)
    keep = jnp.where(row2 >= C, row2 - C, row2 - 1) >= col2

    l2 = lambda t: t * lax.rsqrt(jnp.sum(t * t, -1, keepdims=True) + L2_EPS)
    sl = lambda h: slice(h * N, (h + 1) * N)
    qn = [l2(q_all[:, sl(h)]) * (N ** -0.5) for h in range(H)]
    kn = [l2(k_all[:, sl(h)]) for h in range(H)]

    rs = lambda i: slice(i * C, (i + 1) * C)
    every = [(i, h) for i in range(BB) for h in range(H)]
    gc = {(i, h): Gc[rs(i), h:h + 1] for i, h in every}
    g_last = {(i, h): Gc[(i + 1) * C - 1:(i + 1) * C, h:h + 1] for i, h in every}
    gc2 = {ih: jnp.concatenate([gc[ih], gc[ih]], axis=0) for ih in every}
    decay = {(i, h): jnp.where(keep, jnp.exp(jnp.where(keep, gc2[i, h] - Gr[h:h + 1, rs(i)], 0.0)), 0.0)
             for i, h in every}
    bh = {(i, h): beta[rs(i), h:h + 1] for i, h in every}
    kj = {(i, h): kn[h][rs(i)] for i, h in every}
    KQ = {(i, h): jnp.concatenate([kj[i, h] * bh[i, h], qn[h][rs(i)]], axis=0) for i, h in every}
    M = {ih: _mm(KQ[ih], kj[ih], _NT) * decay[ih] for ih in every}
    T = dict(zip(every, _unit_lower_inverse([-M[ih][:C] for ih in every], C, t_valid)))
    S = {(s, h): s_scr[s, h] for s in range(NS) for h in range(H)}
    for lvl in range(L):
        ch = [(s, s * L + lvl, h) for s in range(NS) for h in range(H)]
        SR = {(i, h): _mm(KQ[i, h] * jnp.exp(gc2[i, h]), S[s, h], _NN) for s, i, h in ch}
        v_new = {(i, h): _mm(T[i, h], v_all[rs(i), sl(h)] * bh[i, h] - SR[i, h][:C], _NN) for _, i, h in ch}
        O = {(i, h): SR[i, h][C:] + _mm(M[i, h][C:], v_new[i, h], _NN) for _, i, h in ch}
        S = {(s, h): S[s, h] * jnp.exp(g_last[i, h])
             + _mm(kj[i, h] * jnp.exp(g_last[i, h] - gc[i, h]), v_new[i, h], _TN) for s, i, h in ch}
        for _, i, h in ch:
            zh = z_all[rs(i), sl(h)]
            o = O[i, h] * lax.rsqrt(jnp.mean(O[i, h] * O[i, h], -1, keepdims=True) + GD_NORM_EPS) * nw
            o_ref[rs(i), sl(h)] = (o * (zh * jax.nn.sigmoid(zh))).astype(o_ref.dtype)
    for (s, h), val in S.items():
        s_scr[s, h] = val

    @pl.when(c == pl.num_programs(1) - 1)
    def _():
        _write_states(st_ref, s_scr)
        last_ref[...] = tail_scr[...]


def gdn_mixer(P, row0, conv_prev, s0, s0_layer, st_buf, layer, n_layers, prm, *, chunk, seqs, chunks, t_valid):
    _, B, H, N, _ = s0.shape
    W = H * N
    SW = prm['SW']
    T = prm['T']
    blk = seqs * chunks * chunk
    nc = T // (chunks * chunk)
    assert T % (chunks * chunk) == 0 and B % seqs == 0 and row0 % blk == 0 and chunk >= SUBLANES
    assert (seqs == 1 or nc == 1) and (chunks == 1 or t_valid == chunk) and 2 * H <= LANES
    assert N == LANES and chunk & (chunk - 1) == 0
    rb0 = row0 // blk
    col = lambda j: pl.BlockSpec((blk, W), lambda i, c: (rb0 + i * nc + c, j))
    full = lambda a: pl.BlockSpec((None,) + a.shape[1:], lambda i, c: (layer,) + (0,) * (a.ndim - 1))
    st_specs, st_args, st_out, st_shape, alias_at = _state_io(s0, s0_layer, st_buf, layer, n_layers, seqs)
    rows = pl.BlockSpec((seqs, SUBLANES, 3 * W), lambda i, c: (i, 0, 0))
    consts = [prm['conv_w'], prm['gate'], prm['norm_w']]
    n_in = 6 + len(consts)
    return pl.pallas_call(
        functools.partial(_gdn_kernel, C=chunk, H=H, N=N, NS=seqs, L=chunks, t_valid=t_valid,
                          gate_lane=SW - LANES),
        grid=(B // seqs, nc),
        in_specs=[col(3), col(4), col(5), col(6),
                  pl.BlockSpec((blk, SW), lambda i, c: (rb0 + i * nc + c, 7 * W // SW)),
                  pl.BlockSpec((None, seqs, SUBLANES, 3 * W), lambda i, c: (s0_layer, i, 0, 0))]
                 + [full(a) for a in consts] + st_specs,
        out_specs=[pl.BlockSpec((blk, W), lambda i, c: (i * nc + c, 0)), st_out, rows],
        out_shape=[jax.ShapeDtypeStruct((B * T, W), BF16), st_shape,
                   jax.ShapeDtypeStruct((B, SUBLANES, 3 * W), F32)],
        scratch_shapes=[pltpu.VMEM((seqs, H, N, N), F32), pltpu.VMEM((seqs, SUBLANES, 3 * W), F32)],
        input_output_aliases={} if alias_at is None else {n_in + alias_at: 1},
        compiler_params=_cparams("parallel", "arbitrary"),
        name="gdn_mixer",
    )(P, P, P, P, P, conv_prev, *consts, *st_args)


def _xattn_kernel(q_ref, k_ref, v_ref, o_ref, *, H, Dh, BB, tq, head_rows):
    scale = Dh ** -0.5
    pairs = [(b, h) for b in range(BB) for h in range(H)]
    sl = [slice(h * Dh, (h + 1) * Dh) for h in range(H)]

    def head(ref, b, h):
        if head_rows:
            return ref[b, pl.ds(h, ref.shape[1] // H, stride=H), :].astype(BF16)
        return ref[:, sl[h]].astype(BF16)

    q = [q_ref[b * tq:(b + 1) * tq, :] for b in range(BB)]
    s = [lax.dot_general(q[b][:, sl[h]], head(k_ref, b, h), _NT, preferred_element_type=F32) * scale
         for b, h in pairs]
    e = [jnp.exp(t - jnp.max(t, -1, keepdims=True)) for t in s]
    pr = [(t / jnp.sum(t, -1, keepdims=True)).astype(BF16) for t in e]
    o = [jnp.dot(pr[j], head(v_ref, b, h), preferred_element_type=F32) for j, (b, h) in enumerate(pairs)]
    for j, (b, h) in enumerate(pairs):
        o_ref[b * tq:(b + 1) * tq, sl[h]] = o[j].astype(o_ref.dtype)


def cross_attention_prompt(q, kv, *, n_mem, heads, tq, q_tiles):
    Wd = kv.shape[1] // 2
    B = kv.shape[0] // n_mem
    return pl.pallas_call(
        functools.partial(_xattn_kernel, H=heads, Dh=Wd // heads, BB=1, tq=tq, head_rows=False),
        grid=(B, q_tiles),
        in_specs=[pl.BlockSpec((tq, Wd), lambda i, t: (i * q_tiles + t, 0)),
                  pl.BlockSpec((n_mem, Wd), lambda i, t: (i, 0)),
                  pl.BlockSpec((n_mem, Wd), lambda i, t: (i, 1))],
        out_specs=pl.BlockSpec((tq, Wd), lambda i, t: (i * q_tiles + t, 0)),
        out_shape=jax.ShapeDtypeStruct((B * q_tiles * tq, Wd), BF16),
        compiler_params=_cparams("parallel", "arbitrary"),
        name="cross_attention_prompt",
    )(q, kv, kv)


def cross_attention_sample(q, row0, mem_k, mem_v, layer, *, heads, tq, seqs):
    _, B, rows, Dh = mem_k.shape
    Wd = heads * Dh
    blk = seqs * tq
    assert B % seqs == 0 and row0 % blk == 0
    rb0 = row0 // blk
    kv = pl.BlockSpec((None, seqs, rows, Dh), lambda i: (layer, i, 0, 0))
    return pl.pallas_call(
        functools.partial(_xattn_kernel, H=heads, Dh=Dh, BB=seqs, tq=tq, head_rows=True),
        grid=(B // seqs,),
        in_specs=[pl.BlockSpec((blk, Wd), lambda i: (rb0 + i, 0)), kv, kv],
        out_specs=pl.BlockSpec((blk, Wd), lambda i: (i, 0)),
        out_shape=jax.ShapeDtypeStruct((B * tq, Wd), BF16),
        compiler_params=_cparams("parallel"),
        name="cross_attention_sample",
    )(q, mem_k, mem_v)


def _lane_pad(a, width):
    return jnp.pad(a, [(0, 0)] * (a.ndim - 1) + [(0, width - a.shape[-1])])


def kernel(x_prompt, mem_prompt, x_sample, state_rwkv, state_shift, state_gdn, state_conv, cache_mem_k, cache_mem_v, w_in, mu_shift, rw_w0, rw_w2, rw_a0, rw_a2, rw_g2, rw_kk, rw_ka, rw_rk, rw_lnx_w, rw_lnx_b, gd_conv_w, gd_a_log, gd_dt_bias, gd_norm_w, w_out, ln1_g, ln1_b, xa_wq, xa_wk, xa_wv, xa_wo, ln2_g, ln2_b, ff_wg, ff_wu, ff_wd, ln3_g, ln3_b):
    depth = w_in.shape[0]
    Bp, Tp, D = x_prompt.shape
    Bs, Ts, _ = x_sample.shape
    Hr, Hg = state_rwkv.shape[2], state_gdn.shape[2]
    W = Hr * RW_HEAD
    assert W == Hg * GD_HEAD and Ts <= SAMPLE_STEPS and Ts >= GD_CONV - 1 and Tp % PROMPT_CHUNK == 0
    n_w, n_a, n_g = rw_w2.shape[1], rw_a2.shape[1], rw_g2.shape[1]
    sw, sa, sg = (_round_up(n, LANES) for n in (n_w, n_a, n_g))
    SW = sw + sa + sg + LANES
    n_mem, xa_heads = cache_mem_k.shape[2], cache_mem_k.shape[3]
    xa_width = xa_heads * cache_mem_k.shape[4]
    alpha = (2 * depth) ** 0.25
    Mp, Ms = Bp * Tp, Bs * SAMPLE_STEPS
    M = Mp + Ms
    tm = _pick_tile(M, (512, 384, 256, 128))
    tf = _pick_tile(ff_wg.shape[-1], (512, 256, 128))
    c_w, c_a, c_g = 3 * W, 3 * W + n_w, 3 * W + n_w + n_a
    c_gd = c_g + n_g

    def to_layout(t):
        parts = [t[..., :c_w], t[..., c_gd:c_gd + 4 * W],
                 _lane_pad(t[..., c_w:c_a], sw), _lane_pad(t[..., c_a:c_g], sa), _lane_pad(t[..., c_g:c_gd], sg),
                 _lane_pad(t[..., c_gd + 4 * W:], LANES)]
        return jnp.concatenate(parts, axis=-1)

    def rwkv_layout(t):
        parts = [t[..., :c_w], _lane_pad(t[..., c_w:c_a], sw), _lane_pad(t[..., c_a:c_g], sa),
                 _lane_pad(t[..., c_g:c_gd], sg + LANES)]
        return jnp.concatenate(parts, axis=-1)

    def rwkv_cols(t):
        s0 = 3 * W
        return jnp.concatenate([t[..., :3 * W], t[..., s0:s0 + n_w], t[..., s0 + sw:s0 + sw + n_a],
                                t[..., s0 + sw + sa:s0 + sw + sa + n_g]], axis=-1)

    w_proj = to_layout(w_in.astype(BF16))
    w_kv = jnp.concatenate([xa_wk, xa_wv], axis=2).astype(BF16)
    w_o, w_q, w_xo = w_out.astype(BF16), xa_wq.astype(BF16), xa_wo.astype(BF16)
    w_g, w_u, w_d = ff_wg.astype(BF16), ff_wu.astype(BF16), ff_wd.astype(BF16)

    x_s = jnp.pad(x_sample, ((0, 0), (0, SAMPLE_STEPS - Ts), (0, 0)))
    x = jnp.concatenate([x_prompt.reshape(Mp, D), x_s.reshape(Ms, D)], axis=0)
    xb = x.astype(BF16)
    mem = mem_prompt.reshape(Bp * n_mem, D).astype(BF16)
    mem_k = cache_mem_k.reshape(depth, Bs, n_mem * xa_heads, -1)
    mem_v = cache_mem_v.reshape(depth, Bs, n_mem * xa_heads, -1)
    zeros = lambda *s: jnp.zeros(s, F32)
    tq = _pick_tile(Tp, (512, 256, 128))
    rw_p = gd_p = rw_s = gd_s = None
    rw_p0, gd_p0 = zeros(1, Bp, Hr, RW_HEAD, RW_HEAD), zeros(1, Bp, Hg, GD_HEAD, GD_HEAD)
    conv_at = SUBLANES - (SAMPLE_STEPS - Ts) - (GD_CONV - 1)

    outs = {n: [] for n in ('p_sh', 'p_cv', 'p_mk', 'p_mv', 's_sh', 's_cv')}
    lora = lambda w, rows: jnp.pad(w, ((0, 0), (0, rows - w.shape[1]), (0, 0))).astype(BF16)
    rw = {'T': None, 'mu': rwkv_layout(mu_shift)[:, None],
          'vec': jnp.stack([rw_w0, rw_a0, rw_kk, rw_ka, rw_rk.reshape(depth, W), rw_lnx_w, rw_lnx_b,
                            jnp.zeros((depth, W), F32)], axis=1),
          'w2': lora(rw_w2, sw), 'a2': lora(rw_a2, sa), 'g2': lora(rw_g2, sg)}
    gd = {'T': None, 'SW': SW, 'conv_w': gd_conv_w, 'gate': jnp.stack([gd_a_log, gd_dt_bias], axis=1),
          'norm_w': gd_norm_w[:, None]}
    prev_s = rwkv_layout(state_shift)[:, :, None]
    conv_s = jnp.pad(state_conv, ((0, 0), (0, 0), (SUBLANES - (GD_CONV - 1), 0), (0, 0)))
    prev_p, conv_p = zeros(1, Bp, 1, 3 * W + SW), zeros(1, Bp, SUBLANES, 3 * W)

    for l in range(depth):
        kv = matmul(mem, w_kv, l, tm=_pick_tile(Bp * n_mem, (512, 256, 128)),
                    tn=_pick_tile(2 * xa_width, (512, 256, 128)))
        P = matmul(xb, w_proj, l, tm=_pick_tile(M, (1024, 512, 384, 256, 128)),
                   tn=_pick_tile(7 * W + SW, (768, 512, 256, 128)))

        o_rw_p, rw_p, sh_p = rwkv7_mixer(P, 0, prev_p, rw_p0, 0, rw_p, l, depth, dict(rw, T=Tp),
                                         chunk=PROMPT_CHUNK, seqs=1, chunks=RW_PROMPT_CHUNKS,
                                         t_valid=PROMPT_CHUNK)
        o_gd_p, gd_p, cv_p = gdn_mixer(P, 0, conv_p, gd_p0, 0, gd_p, l, depth, dict(gd, T=Tp),
                                       chunk=PROMPT_CHUNK, seqs=1, chunks=GD_PROMPT_CHUNKS, t_valid=PROMPT_CHUNK)
        o_rw_s, rw_s, sh_s = rwkv7_mixer(P, Mp, prev_s, state_rwkv, l, rw_s, l, depth, dict(rw, T=SAMPLE_STEPS),
                                         chunk=SAMPLE_STEPS, seqs=SAMPLE_SEQS, chunks=1, t_valid=Ts)
        o_gd_s, gd_s, cv_s = gdn_mixer(P, Mp, conv_s, state_gdn, l, gd_s, l, depth, dict(gd, T=SAMPLE_STEPS),
                                       chunk=SAMPLE_STEPS, seqs=SAMPLE_SEQS, chunks=1, t_valid=Ts)

        x, q = mix_out_ln_q((o_rw_p, o_rw_s), (o_gd_p, o_gd_s), w_o, w_q, l, x, ln1_g[l], ln1_b[l],
                            alpha=alpha, tm=tm)

        att_p = cross_attention_prompt(q, kv, n_mem=n_mem, heads=xa_heads, tq=tq, q_tiles=Tp // tq)
        att_s = cross_attention_sample(q, Mp, mem_k, mem_v, l, heads=xa_heads, tq=SAMPLE_STEPS, seqs=SAMPLE_SEQS)
        x, xb = attn_out_ffn((att_p, att_s), w_xo, w_g, w_u, w_d, l, x, ln2_g[l], ln2_b[l], ln3_g[l], ln3_b[l],
                             alpha=alpha, tm=tm, tf=tf)

        outs['p_sh'].append(rwkv_cols(sh_p[:, 0])); outs['s_sh'].append(rwkv_cols(sh_s[:, 0]))
        outs['p_cv'].append(cv_p[:, SUBLANES - (GD_CONV - 1):])
        outs['s_cv'].append(cv_s[:, conv_at:conv_at + GD_CONV - 1])
        outs['p_mk'].append(kv[:, :xa_width].reshape(Bp, n_mem, xa_heads, -1))
        outs['p_mv'].append(kv[:, xa_width:].reshape(Bp, n_mem, xa_heads, -1))

    st = {n: jnp.stack(v) for n, v in outs.items()}
    return (x[:Mp].reshape(Bp, Tp, D), x[Mp:].reshape(Bs, SAMPLE_STEPS, D)[:, :Ts], rw_p, st['p_sh'],
            gd_p, st['p_cv'], st['p_mk'], st['p_mv'], rw_s, st['s_sh'], gd_s, st['s_cv'])
```

```python
import functools

import jax
import jax.numpy as jnp
from jax import lax
from jax.experimental import pallas as pl
from jax.experimental.pallas import tpu as pltpu

F32 = jnp.float32
BF16 = jnp.bfloat16

LANES = 128
SUBLANES = 8
RW_HEAD = 64
GD_HEAD = 128
GD_CONV = 4
LN_EPS = 1e-5
RW_GN_EPS = 64e-5
GD_NORM_EPS = 1e-6
L2_EPS = 1e-12

PROMPT_CHUNK = 64
SAMPLE_STEPS = SUBLANES
SAMPLE_SEQS = 8
RW_PROMPT_CHUNKS = 4
GD_PROMPT_CHUNKS = 8

VMEM_LIMIT_BYTES = 56 * 1024 * 1024

_NN = (((1,), (0,)), ((), ()))
_NT = (((1,), (1,)), ((), ()))
_TN = (((0,), (0,)), ((), ()))
_TT = (((0,), (1,)), ((), ()))


def _cparams(*sem):
    return pltpu.CompilerParams(dimension_semantics=sem, vmem_limit_bytes=VMEM_LIMIT_BYTES)


def _round_up(n, m):
    return -(-n // m) * m


def _pick_tile(n, candidates):
    for c in candidates:
        if n % c == 0:
            return c
    raise ValueError(f"no tile in {candidates} divides {n}")


def _mm_kernel(x_ref, w_ref, o_ref):
    o_ref[...] = jnp.dot(x_ref[...], w_ref[...], preferred_element_type=F32).astype(o_ref.dtype)


def matmul(x, w, layer, *, tm, tn, out_dtype=F32):
    M, K = x.shape
    N = w.shape[2]
    assert M % tm == 0 and N % tn == 0
    return pl.pallas_call(
        _mm_kernel,
        grid=(M // tm, N // tn),
        in_specs=[pl.BlockSpec((tm, K), lambda i, j: (i, 0)),
                  pl.BlockSpec((None, K, tn), lambda i, j: (layer, 0, j))],
        out_specs=pl.BlockSpec((tm, tn), lambda i, j: (i, j)),
        out_shape=jax.ShapeDtypeStruct((M, N), out_dtype),
        compiler_params=_cparams("parallel", "arbitrary"),
        name="matmul",
    )(x, w)


def _group_rows(p_ref, s_ref, n_p):
    return jnp.where(pl.program_id(0) < n_p, p_ref[...], s_ref[...])


def _group_specs(tm, width, n_p):
    return (pl.BlockSpec((tm, width), lambda i, *_: (jnp.minimum(i, n_p - 1), 0)),
            pl.BlockSpec((tm, width), lambda i, *_: (jnp.maximum(i - n_p, 0), 0)))


def _layer_norm_rows(y, g, b):
    mu = jnp.mean(y, -1, keepdims=True)
    yc = y - mu
    var = jnp.mean(yc * yc, -1, keepdims=True)
    return yc * lax.rsqrt(var + LN_EPS) * g + b


def _mix_out_kernel(ap_ref, as_ref, bp_ref, bs_ref, wa_ref, wb_ref, *refs, alpha, n_p):
    *res_refs, g_ref, beta_ref, wq_ref, x_ref, q_ref = refs
    res = res_refs[0][...] if len(res_refs) == 1 else _group_rows(*res_refs, n_p)
    h = jnp.dot(_group_rows(ap_ref, as_ref, n_p), wa_ref[...], preferred_element_type=F32)
    h += jnp.dot(_group_rows(bp_ref, bs_ref, n_p), wb_ref[...], preferred_element_type=F32)
    x = _layer_norm_rows(alpha * res + h, g_ref[...], beta_ref[...])
    x_ref[...] = x
    q_ref[...] = jnp.dot(x.astype(BF16), wq_ref[...], preferred_element_type=F32).astype(BF16)


def mix_out_ln_q(o_a, o_b, w, wq, layer, res, g, b, *, alpha, tm):
    res = list(res) if isinstance(res, tuple) else [res]
    M, D = sum(r.shape[0] for r in res), res[0].shape[1]
    Ka, Kb = o_a[0].shape[1], o_b[0].shape[1]
    Q = wq.shape[2]
    n_p = o_a[0].shape[0] // tm
    assert M % tm == 0 and o_a[0].shape[0] % tm == 0 and o_a[1].shape[0] % tm == 0
    assert Ka == Kb and w.shape[1] == Ka + Kb and o_a[0].shape[0] + o_a[1].shape[0] == M
    rows = lambda n: pl.BlockSpec((tm, n), lambda i: (i, 0))
    vec = pl.BlockSpec((1, D), lambda i: (0, 0))
    return pl.pallas_call(
        functools.partial(_mix_out_kernel, alpha=alpha, n_p=n_p),
        grid=(M // tm,),
        in_specs=[*_group_specs(tm, Ka, n_p), *_group_specs(tm, Kb, n_p),
                  pl.BlockSpec((None, Ka, D), lambda i: (layer, 0, 0)),
                  pl.BlockSpec((None, Kb, D), lambda i: (layer, 1, 0)),
                  *([rows(D)] if len(res) == 1 else _group_specs(tm, D, n_p)), vec, vec,
                  pl.BlockSpec((None, D, Q), lambda i: (layer, 0, 0))],
        out_specs=[rows(D), rows(Q)],
        out_shape=[jax.ShapeDtypeStruct((M, D), F32), jax.ShapeDtypeStruct((M, Q), BF16)],
        compiler_params=_cparams("parallel"),
        name="mix_out_ln_q",
    )(*o_a, *o_b, w, w, *res, g.reshape(1, D), b.reshape(1, D), wq)


def _attn_ffn_kernel(ap_ref, as_ref, wo_ref, res_ref, g2_ref, b2_ref, wg_ref, wu_ref, wd_ref, g3_ref, b3_ref,
                     o_ref, ob_ref, x_scr, xb_scr, acc_ref, *, alpha, n_p):
    f = pl.program_id(1)

    @pl.when(f == 0)
    def _():
        h = jnp.dot(_group_rows(ap_ref, as_ref, n_p), wo_ref[...], preferred_element_type=F32)
        x = _layer_norm_rows(alpha * res_ref[...] + h, g2_ref[...], b2_ref[...])
        x_scr[...] = x
        xb_scr[...] = x.astype(BF16)
        acc_ref[...] = jnp.zeros_like(acc_ref)

    x = xb_scr[...]
    gate = jnp.dot(x, wg_ref[...], preferred_element_type=F32)
    up = jnp.dot(x, wu_ref[...], preferred_element_type=F32)
    hidden = (gate * jax.nn.sigmoid(gate) * up).astype(BF16)
    acc_ref[...] += jnp.dot(hidden, wd_ref[...], preferred_element_type=F32)

    @pl.when(f == pl.num_programs(1) - 1)
    def _():
        y = _layer_norm_rows(alpha * x_scr[...] + acc_ref[...], g3_ref[...], b3_ref[...])
        o_ref[...] = y
        ob_ref[...] = y.astype(BF16)


def attn_out_ffn(att, wo, wg, wu, wd, layer, res, g2, b2, g3, b3, *, alpha, tm, tf):
    M, D = res.shape
    Q = att[0].shape[1]
    Fd = wg.shape[2]
    n_p = att[0].shape[0] // tm
    assert M % tm == 0 and Fd % tf == 0 and att[0].shape[0] % tm == 0 and att[0].shape[0] + att[1].shape[0] == M
    vec = pl.BlockSpec((1, D), lambda i, f: (0, 0))
    rows = pl.BlockSpec((tm, D), lambda i, f: (i, 0))
    return pl.pallas_call(
        functools.partial(_attn_ffn_kernel, alpha=alpha, n_p=n_p),
        grid=(M // tm, Fd // tf),
        in_specs=[*_group_specs(tm, Q, n_p),
                  pl.BlockSpec((None, Q, D), lambda i, f: (layer, 0, 0)), rows, vec, vec,
                  pl.BlockSpec((None, D, tf), lambda i, f: (layer, 0, f)),
                  pl.BlockSpec((None, D, tf), lambda i, f: (layer, 0, f)),
                  pl.BlockSpec((None, tf, D), lambda i, f: (layer, f, 0)), vec, vec],
        out_specs=[rows, rows],
        out_shape=[jax.ShapeDtypeStruct((M, D), F32), jax.ShapeDtypeStruct((M, D), BF16)],
        scratch_shapes=[pltpu.VMEM((tm, D), F32), pltpu.VMEM((tm, D), BF16), pltpu.VMEM((tm, D), F32)],
        compiler_params=_cparams("parallel", "arbitrary"),
        name="attn_out_ffn",
    )(*att, wo, res, g2.reshape(1, D), b2.reshape(1, D), wg, wu, wd, g3.reshape(1, D), b3.reshape(1, D))


def _mm(a, b, dims):
    return lax.dot_general(a.astype(BF16), b.astype(BF16), dims, preferred_element_type=F32)


def _three_bf16_terms(a):
    a1 = a.astype(BF16)
    r1 = a - a1.astype(F32)
    a2 = r1.astype(BF16)
    return a1, a2, (r1 - a2.astype(F32)).astype(BF16)


def _unit_lower_inverse(xs, c, n_live):
    row = lax.broadcasted_iota(jnp.int32, (c, c), 0)
    col = lax.broadcasted_iota(jnp.int32, (c, c), 1)
    apart = row ^ col
    eye = jnp.where(row == col, 1.0, 0.0).astype(F32)
    ts = [eye + jnp.where((apart >> 1) == 0, x, 0.0) for x in xs]
    log_n = 1
    while (1 << log_n) < n_live:
        couples = (apart >> log_n) == 1
        xo = [jnp.where(couples, x, 0.0) for x in xs]
        ts = [t + _mm(_mm(t, x_off, _NN), t, _NN) for t, x_off in zip(ts, xo)]
        log_n += 1
    return ts


def _step_in_seq(R, C):
    return lax.broadcasted_iota(jnp.int32, (R, 1), 0) & (C - 1)


def _seq_tri(R, C):
    row = lax.broadcasted_iota(jnp.int32, (R, R), 0)
    col = lax.broadcasted_iota(jnp.int32, (R, R), 1)
    return jnp.where(row >= col, jnp.where((row & -C) == (col & -C), 1.0, 0.0), 0.0).astype(BF16)


def _per_seq_rows(rows, C):
    rows = [jnp.broadcast_to(t, (C, t.shape[-1])) for t in rows]
    return rows[0] if len(rows) == 1 else jnp.concatenate(rows, axis=0)


def _delayed_rows(x, tails, j, BB, C):
    xj = pltpu.roll(x, j, 0)
    row = lax.broadcasted_iota(jnp.int32, (SUBLANES, 1), 0)
    pieces = []
    for i in range(BB):
        pieces.append(jnp.where(row < j, pltpu.roll(tails[i], j, 0), xj[i * C:i * C + SUBLANES]))
        if C > SUBLANES:
            pieces.append(xj[i * C + SUBLANES:(i + 1) * C])
    return pieces[0] if len(pieces) == 1 else jnp.concatenate(pieces, axis=0)


def _half_sums(x, N):
    low = lax.broadcasted_iota(jnp.int32, (1, LANES), 1) < N
    out = []
    for j in range(x.shape[1] // LANES):
        xv = x[:, j * LANES:(j + 1) * LANES]
        lo = jnp.sum(jnp.where(low, xv, 0.0), -1, keepdims=True)
        out += [lo, jnp.sum(jnp.where(low, 0.0, xv), -1, keepdims=True)]
    return out


def _half_bcast(cols, N):
    low = lax.broadcasted_iota(jnp.int32, (1, LANES), 1) < N
    return jnp.concatenate([jnp.where(low, cols[2 * j], cols[2 * j + 1]) for j in range(len(cols) // 2)], axis=1)


def _write_states(st_ref, s_scr):
    if len(st_ref.shape) == len(s_scr.shape):
        st_ref[...] = s_scr[...]
    else:
        for j in range(st_ref.shape[0]):
            st_ref[j] = s_scr[...]


def _rwkv_kernel(*refs, C, H, N, NS, L, t_valid):
    r_ref, k_ref, v_ref, sm_ref, prev_ref, mu_ref, vec_ref, w2_ref, a2_ref, g2_ref, s0_ref = refs[:11]
    o_ref, st_ref, last_ref, s_scr, tail_scr, o_scr = refs[-6:]
    c = pl.program_id(1)
    W = H * N
    BB = NS * L
    R = BB * C
    Rs = L * C
    sw, sa, sg = w2_ref.shape[0], a2_ref.shape[0], g2_ref.shape[0]

    @pl.when(c == 0)
    def _():
        s_scr[...] = s0_ref[...]
        tail_scr[...] = prev_ref[...]

    first = _step_in_seq(R, Rs) == 0
    vec = vec_ref[...]
    w0, a0, k_k, k_a, r_k, ln_w, ln_b = (vec[j:j + 1] for j in range(7))
    mu = mu_ref[...]
    tail = _per_seq_rows([tail_scr[s] for s in range(NS)], Rs)

    def shifted(x, lo):
        hi = lo + x.shape[1]
        prev = jnp.where(first, tail[:, lo:hi], pltpu.roll(x, 1, 0))
        return x + (prev - x) * mu[:, lo:hi]

    xr, xk, xv, xs = r_ref[...], k_ref[...], v_ref[...], sm_ref[...]
    r, k, v, ms = shifted(xr, 0), shifted(xk, W), shifted(xv, 2 * W), shifted(xs, 3 * W)
    for x, lo in ((xr, 0), (xk, W), (xv, 2 * W), (xs, 3 * W)):
        for s in range(NS):
            end = s * Rs + (L - 1) * C + t_valid
            tail_scr[s, :, lo:lo + x.shape[1]] = x[end - 1:end]

    lora_w = jnp.dot(jnp.tanh(ms[:, :sw]).astype(BF16), w2_ref[...], preferred_element_type=F32)
    lora_a = jnp.dot(ms[:, sw:sw + sa].astype(BF16), a2_ref[...], preferred_element_type=F32)
    gate = jnp.dot(jax.nn.sigmoid(ms[:, sw + sa:sw + sa + sg]).astype(BF16), g2_ref[...],
                   preferred_element_type=F32)
    lw = -jnp.exp(-jax.nn.softplus(-(w0 + lora_w)) - 0.5)
    a = jax.nn.sigmoid(a0 + lora_a)
    kk = k * k_k
    kk = kk * _half_bcast([lax.rsqrt(s + L2_EPS) for s in _half_sums(kk * kk, N)], N)
    k = k * (1.0 + (a - 1.0) * k_a)
    if t_valid < C:
        valid = _step_in_seq(R, C) < t_valid
        r, k, v, kk, lw = (jnp.where(valid, t, 0.0) for t in (r, k, v, kk, lw))

    tri = _seq_tri(R, C)
    G = sum(jnp.dot(tri, t, preferred_element_type=F32) for t in _three_bf16_terms(lw))
    mid = C // 2 - 1
    g_mid = [G[i * C + mid:i * C + mid + 1] for i in range(BB)]
    g_last = [G[(i + 1) * C - 1:(i + 1) * C] for i in range(BB)]
    Gm = G - _per_seq_rows(g_mid, C)
    e_out = jnp.exp(-Gm)
    e_tail = jnp.exp(_per_seq_rows(g_last, C) - G)
    b_in = kk * a
    A_, R_ = -kk * jnp.exp(Gm - lw), r * jnp.exp(Gm)
    B_, K_ = b_in * e_out, k * e_out
    Bt, Kt = b_in * e_tail, k * e_tail
    e_mid, e_last = [jnp.exp(t) for t in g_mid], [jnp.exp(t) for t in g_last]

    row2 = lax.broadcasted_iota(jnp.int32, (2 * C, 2 * C), 0)
    col2 = lax.broadcasted_iota(jnp.int32, (2 * C, 2 * C), 1)
    rowc = jnp.where(row2 >= C, row2 - C, row2)
    colc = jnp.where(col2 >= C, col2 - C, col2)
    keep = jnp.where(row2 >= C, rowc, rowc - 1) >= colc

    rs = lambda i: slice(i * C, (i + 1) * C)
    sl = lambda h: slice(h * N, (h + 1) * N)
    pair = lambda x, y, i, h: jnp.concatenate([x[rs(i), sl(h)], y[rs(i), sl(h)]], axis=0)
    every = [(i, h) for i in range(BB) for h in range(H)]
    AR = {ih: pair(A_, R_, *ih) for ih in every}
    M = {ih: jnp.where(keep, _mm(AR[ih], pair(B_, K_, *ih), _NT), 0.0) for ih in every}
    AV = {(i, h): _mm(M[i, h][:C, C:], v[rs(i), sl(h)], _NN) for i, h in every}
    T = dict(zip(every, _unit_lower_inverse([M[ih][:C, :C] for ih in every], C, t_valid)))
    S = {(s, h): s_scr[s, h] for s in range(NS) for h in range(H)}
    for lvl in range(L):
        ch = [(s, s * L + lvl, h) for s in range(NS) for h in range(H)]
        SR = {(i, h): _mm(AR[i, h], S[s, h] * e_mid[i][:, sl(h)], _NT) for s, i, h in ch}
        U = {(i, h): _mm(T[i, h], SR[i, h][:C] + AV[i, h], _NN) for _, i, h in ch}
        UV = {(i, h): jnp.concatenate([U[i, h], v[rs(i), sl(h)]], axis=0) for _, i, h in ch}
        O = {(i, h): SR[i, h][C:] + _mm(M[i, h][C:], UV[i, h], _NN) for _, i, h in ch}
        S = {(s, h): S[s, h] * e_last[i][:, sl(h)] + _mm(UV[i, h], pair(Bt, Kt, i, h), _TN) for s, i, h in ch}
        for _, i, h in ch:
            o_scr[rs(i), sl(h)] = O[i, h]
    for (s, h), val in S.items():
        s_scr[s, h] = val

    o = o_scr[...]
    o = o - _half_bcast([s * (1.0 / N) for s in _half_sums(o, N)], N)
    o = o * _half_bcast([lax.rsqrt(s * (1.0 / N) + RW_GN_EPS) for s in _half_sums(o * o, N)], N) * ln_w + ln_b
    bonus = _half_bcast(_half_sums(r * k * r_k, N), N) * v
    o_ref[...] = ((o + bonus) * gate).astype(o_ref.dtype)

    @pl.when(c == pl.num_programs(1) - 1)
    def _():
        _write_states(st_ref, s_scr)
        last_ref[...] = tail_scr[...]


def _state_io(s0, s0_layer, st_buf, layer, n_layers, seqs):
    blk = (None, seqs) + s0.shape[2:]
    st_in = pl.BlockSpec(blk, lambda i, c: (s0_layer, i, 0, 0, 0))
    if st_buf is None:
        every = pl.BlockSpec((n_layers, seqs) + s0.shape[2:], lambda i, c: (0, i, 0, 0, 0))
        return [st_in], [s0], every, jax.ShapeDtypeStruct((n_layers,) + s0.shape[1:], F32), None
    assert st_buf.shape == (n_layers,) + s0.shape[1:]
    st_out = pl.BlockSpec(blk, lambda i, c: (layer, i, 0, 0, 0))
    return ([st_in, pl.BlockSpec(memory_space=pl.ANY)], [s0, st_buf], st_out,
            jax.ShapeDtypeStruct(st_buf.shape, F32), 1)


def rwkv7_mixer(P, row0, prev, s0, s0_layer, st_buf, layer, n_layers, prm, *, chunk, seqs, chunks, t_valid):
    _, B, H, N, _ = s0.shape
    W = H * N
    SW = prm['mu'].shape[-1] - 3 * W
    T = prm['T']
    blk = seqs * chunks * chunk
    nc = T // (chunks * chunk)
    assert T % (chunks * chunk) == 0 and B % seqs == 0 and row0 % blk == 0 and (7 * W) % SW == 0
    assert (seqs == 1 or nc == 1) and (chunks == 1 or t_valid == chunk)
    assert 2 * N == LANES and chunk & (chunk - 1) == 0
    rb0 = row0 // blk
    col = lambda j: pl.BlockSpec((blk, W), lambda i, c: (rb0 + i * nc + c, j))
    full = lambda a: pl.BlockSpec((None,) + a.shape[1:], lambda i, c: (layer,) + (0,) * (a.ndim - 1))
    st_specs, st_args, st_out, st_shape, alias_at = _state_io(s0, s0_layer, st_buf, layer, n_layers, seqs)
    row = pl.BlockSpec((seqs, 1, 3 * W + SW), lambda i, c: (i, 0, 0))
    consts = [prm['mu'], prm['vec'], prm['w2'], prm['a2'], prm['g2']]
    n_in = 5 + len(consts)
    return pl.pallas_call(
        functools.partial(_rwkv_kernel, C=chunk, H=H, N=N, NS=seqs, L=chunks, t_valid=t_valid),
        grid=(B // seqs, nc),
        in_specs=[col(0), col(1), col(2),
                  pl.BlockSpec((blk, SW), lambda i, c: (rb0 + i * nc + c, 7 * W // SW)),
                  pl.BlockSpec((None, seqs, 1, 3 * W + SW), lambda i, c: (s0_layer, i, 0, 0))]
                 + [full(a) for a in consts] + st_specs,
        out_specs=[pl.BlockSpec((blk, W), lambda i, c: (i * nc + c, 0)), st_out, row],
        out_shape=[jax.ShapeDtypeStruct((B * T, W), BF16), st_shape,
                   jax.ShapeDtypeStruct((B, 1, 3 * W + SW), F32)],
        scratch_shapes=[pltpu.VMEM((seqs, H, N, N), F32), pltpu.VMEM((seqs, 1, 3 * W + SW), F32),
                        pltpu.VMEM((blk, W), F32)],
        input_output_aliases={} if alias_at is None else {n_in + alias_at: 1},
        compiler_params=_cparams("parallel", "arbitrary"),
        name="rwkv7_mixer",
    )(P, P, P, P, prev, *consts, *st_args)


def _gdn_kernel(*refs, C, H, N, NS, L, t_valid, gate_lane):
    q_ref, k_ref, v_ref, z_ref, sm_ref, cprev_ref, cw_ref, gate_ref, nw_ref, s0_ref = refs[:10]
    o_ref, st_ref, last_ref, s_scr, tail_scr = refs[-5:]
    c = pl.program_id(1)
    W = H * N
    BB = NS * L
    R = BB * C
    Rs = L * C

    @pl.when(c == 0)
    def _():
        s_scr[...] = s0_ref[...]
        tail_scr[...] = cprev_ref[...]

    cw = cw_ref[...]
    nw = nw_ref[...]
    tails = [tail_scr[s] for s in range(NS)]

    def conv_silu(x, lo):
        hi = lo + W
        acc = x * cw[GD_CONV - 1:GD_CONV, lo:hi]
        for j in range(1, GD_CONV):
            acc = acc + (_delayed_rows(x, [t[:, lo:hi] for t in tails], j, NS, Rs)
                         * cw[GD_CONV - 1 - j:GD_CONV - j, lo:hi])
        return acc * jax.nn.sigmoid(acc)

    xq, xk, xv = q_ref[...], k_ref[...], v_ref[...]
    q_all, k_all, v_all = conv_silu(xq, 0), conv_silu(xk, W), conv_silu(xv, 2 * W)
    for x, lo in ((xq, 0), (xk, W), (xv, 2 * W)):
        for s in range(NS):
            tail_scr[s, :, lo:lo + W] = x[(s + 1) * Rs - SUBLANES:(s + 1) * Rs]
    z_all = z_ref[...]
    gates = sm_ref[:, gate_lane:gate_lane + LANES]
    beta = jax.nn.sigmoid(gates[:, :H])
    g = -jnp.exp(gate_ref[0:1, :]) * jax.nn.softplus(gates[:, H:2 * H] + gate_ref[1:2, :])
    if t_valid < C:
        valid = _step_in_seq(R, C) < t_valid
        beta, g, q_all, k_all, v_all = (jnp.where(valid, t, 0.0) for t in (beta, g, q_all, k_all, v_all))
    tri = _seq_tri(R, C)
    g_terms = _three_bf16_terms(g)
    Gc = sum(jnp.dot(tri, t, preferred_element_type=F32) for t in g_terms)
    Gr = sum(lax.dot_general(t, tri, _TT, preferred_element_type=F32) for t in g_terms)

    row2 = lax.broadcasted_iota(jnp.int32, (2 * C, C), 0)
    col2 = lax.broadcasted_iota(jnp.int32, (2 * C, C), 1)
    keep = jnp.where(row2 >= C, row2 - C, row2 - 1) >= col2

    l2 = lambda t: t * lax.rsqrt(jnp.sum(t * t, -1, keepdims=True) + L2_EPS)
    sl = lambda h: slice(h * N, (h + 1) * N)
    qn = [l2(q_all[:, sl(h)]) * (N ** -0.5) for h in range(H)]
    kn = [l2(k_all[:, sl(h)]) for h in range(H)]

    rs = lambda i: slice(i * C, (i + 1) * C)
    every = [(i, h) for i in range(BB) for h in range(H)]
    gc = {(i, h): Gc[rs(i), h:h + 1] for i, h in every}
    g_last = {(i, h): Gc[(i + 1) * C - 1:(i + 1) * C, h:h + 1] for i, h in every}
    gc2 = {ih: jnp.concatenate([gc[ih], gc[ih]], axis=0) for ih in every}
    decay = {(i, h): jnp.where(keep, jnp.exp(jnp.where(keep, gc2[i, h] - Gr[h:h + 1, rs(i)], 0.0)), 0.0)
             for i, h in every}
    bh = {(i, h): beta[rs(i), h:h + 1] for i, h in every}
    kj = {(i, h): kn[h][rs(i)] for i, h in every}
    KQ = {(i, h): jnp.concatenate([kj[i, h] * bh[i, h], qn[h][rs(i)]], axis=0) for i, h in every}
    M = {ih: _mm(KQ[ih], kj[ih], _NT) * decay[ih] for ih in every}
    T = dict(zip(every, _unit_lower_inverse([-M[ih][:C] for ih in every], C, t_valid)))
    S = {(s, h): s_scr[s, h] for s in range(NS) for h in range(H)}
    for lvl in range(L):
        ch = [(s, s * L + lvl, h) for s in range(NS) for h in range(H)]
        SR = {(i, h): _mm(KQ[i, h] * jnp.exp(gc2[i, h]), S[s, h], _NN) for s, i, h in ch}
        v_new = {(i, h): _mm(T[i, h], v_all[rs(i), sl(h)] * bh[i, h] - SR[i, h][:C], _NN) for _, i, h in ch}
        O = {(i, h): SR[i, h][C:] + _mm(M[i, h][C:], v_new[i, h], _NN) for _, i, h in ch}
        S = {(s, h): S[s, h] * jnp.exp(g_last[i, h])
             + _mm(kj[i, h] * jnp.exp(g_last[i, h] - gc[i, h]), v_new[i, h], _TN) for s, i, h in ch}
        for _, i, h in ch:
            zh = z_all[rs(i), sl(h)]
            o = O[i, h] * lax.rsqrt(jnp.mean(O[i, h] * O[i, h], -1, keepdims=True) + GD_NORM_EPS) * nw
            o_ref[rs(i), sl(h)] = (o * (zh * jax.nn.sigmoid(zh))).astype(o_ref.dtype)
    for (s, h), val in S.items():
        s_scr[s, h] = val

    @pl.when(c == pl.num_programs(1) - 1)
    def _():
        _write_states(st_ref, s_scr)
        last_ref[...] = tail_scr[...]


def gdn_mixer(P, row0, conv_prev, s0, s0_layer, st_buf, layer, n_layers, prm, *, chunk, seqs, chunks, t_valid):
    _, B, H, N, _ = s0.shape
    W = H * N
    SW = prm['SW']
    T = prm['T']
    blk = seqs * chunks * chunk
    nc = T // (chunks * chunk)
    assert T % (chunks * chunk) == 0 and B % seqs == 0 and row0 % blk == 0 and chunk >= SUBLANES
    assert (seqs == 1 or nc == 1) and (chunks == 1 or t_valid == chunk) and 2 * H <= LANES
    assert N == LANES and chunk & (chunk - 1) == 0
    rb0 = row0 // blk
    col = lambda j: pl.BlockSpec((blk, W), lambda i, c: (rb0 + i * nc + c, j))
    full = lambda a: pl.BlockSpec((None,) + a.shape[1:], lambda i, c: (layer,) + (0,) * (a.ndim - 1))
    st_specs, st_args, st_out, st_shape, alias_at = _state_io(s0, s0_layer, st_buf, layer, n_layers, seqs)
    rows = pl.BlockSpec((seqs, SUBLANES, 3 * W), lambda i, c: (i, 0, 0))
    consts = [prm['conv_w'], prm['gate'], prm['norm_w']]
    n_in = 6 + len(consts)
    return pl.pallas_call(
        functools.partial(_gdn_kernel, C=chunk, H=H, N=N, NS=seqs, L=chunks, t_valid=t_valid,
                          gate_lane=SW - LANES),
        grid=(B // seqs, nc),
        in_specs=[col(3), col(4), col(5), col(6),
                  pl.BlockSpec((blk, SW), lambda i, c: (rb0 + i * nc + c, 7 * W // SW)),
                  pl.BlockSpec((None, seqs, SUBLANES, 3 * W), lambda i, c: (s0_layer, i, 0, 0))]
                 + [full(a) for a in consts] + st_specs,
        out_specs=[pl.BlockSpec((blk, W), lambda i, c: (i * nc + c, 0)), st_out, rows],
        out_shape=[jax.ShapeDtypeStruct((B * T, W), BF16), st_shape,
                   jax.ShapeDtypeStruct((B, SUBLANES, 3 * W), F32)],
        scratch_shapes=[pltpu.VMEM((seqs, H, N, N), F32), pltpu.VMEM((seqs, SUBLANES, 3 * W), F32)],
        input_output_aliases={} if alias_at is None else {n_in + alias_at: 1},
        compiler_params=_cparams("parallel", "arbitrary"),
        name="gdn_mixer",
    )(P, P, P, P, P, conv_prev, *consts, *st_args)


def _xattn_kernel(q_ref, k_ref, v_ref, o_ref, *, H, Dh, BB, tq, head_rows):
    scale = Dh ** -0.5
    pairs = [(b, h) for b in range(BB) for h in range(H)]
    sl = [slice(h * Dh, (h + 1) * Dh) for h in range(H)]

    def head(ref, b, h):
        if head_rows:
            return ref[b, pl.ds(h, ref.shape[1] // H, stride=H), :].astype(BF16)
        return ref[:, sl[h]].astype(BF16)

    q = [q_ref[b * tq:(b + 1) * tq, :] for b in range(BB)]
    s = [lax.dot_general(q[b][:, sl[h]], head(k_ref, b, h), _NT, preferred_element_type=F32) * scale
         for b, h in pairs]
    e = [jnp.exp(t - jnp.max(t, -1, keepdims=True)) for t in s]
    pr = [(t / jnp.sum(t, -1, keepdims=True)).astype(BF16) for t in e]
    o = [jnp.dot(pr[j], head(v_ref, b, h), preferred_element_type=F32) for j, (b, h) in enumerate(pairs)]
    for j, (b, h) in enumerate(pairs):
        o_ref[b * tq:(b + 1) * tq, sl[h]] = o[j].astype(o_ref.dtype)


def cross_attention_prompt(q, kv, *, n_mem, heads, tq, q_tiles):
    Wd = kv.shape[1] // 2
    B = kv.shape[0] // n_mem
    return pl.pallas_call(
        functools.partial(_xattn_kernel, H=heads, Dh=Wd // heads, BB=1, tq=tq, head_rows=False),
        grid=(B, q_tiles),
        in_specs=[pl.BlockSpec((tq, Wd), lambda i, t: (i * q_tiles + t, 0)),
                  pl.BlockSpec((n_mem, Wd), lambda i, t: (i, 0)),
                  pl.BlockSpec((n_mem, Wd), lambda i, t: (i, 1))],
        out_specs=pl.BlockSpec((tq, Wd), lambda i, t: (i * q_tiles + t, 0)),
        out_shape=jax.ShapeDtypeStruct((B * q_tiles * tq, Wd), BF16),
        compiler_params=_cparams("parallel", "arbitrary"),
        name="cross_attention_prompt",
    )(q, kv, kv)


def cross_attention_sample(q, row0, mem_k, mem_v, layer, *, heads, tq, seqs):
    _, B, rows, Dh = mem_k.shape
    Wd = heads * Dh
    blk = seqs * tq
    assert B % seqs == 0 and row0 % blk == 0
    rb0 = row0 // blk
    kv = pl.BlockSpec((None, seqs, rows, Dh), lambda i: (layer, i, 0, 0))
    return pl.pallas_call(
        functools.partial(_xattn_kernel, H=heads, Dh=Dh, BB=seqs, tq=tq, head_rows=True),
        grid=(B // seqs,),
        in_specs=[pl.BlockSpec((blk, Wd), lambda i: (rb0 + i, 0)), kv, kv],
        out_specs=pl.BlockSpec((blk, Wd), lambda i: (i, 0)),
        out_shape=jax.ShapeDtypeStruct((B * tq, Wd), BF16),
        compiler_params=_cparams("parallel"),
        name="cross_attention_sample",
    )(q, mem_k, mem_v)


def _lane_pad(a, width):
    return jnp.pad(a, [(0, 0)] * (a.ndim - 1) + [(0, width - a.shape[-1])])


def kernel(x_prompt, mem_prompt, x_sample, state_rwkv, state_shift, state_gdn, state_conv, cache_mem_k, cache_mem_v, w_in, mu_shift, rw_w0, rw_w2, rw_a0, rw_a2, rw_g2, rw_kk, rw_ka, rw_rk, rw_lnx_w, rw_lnx_b, gd_conv_w, gd_a_log, gd_dt_bias, gd_norm_w, w_out, ln1_g, ln1_b, xa_wq, xa_wk, xa_wv, xa_wo, ln2_g, ln2_b, ff_wg, ff_wu, ff_wd, ln3_g, ln3_b):
    depth = w_in.shape[0]
    Bp, Tp, D = x_prompt.shape
    Bs, Ts, _ = x_sample.shape
    Hr, Hg = state_rwkv.shape[2], state_gdn.shape[2]
    W = Hr * RW_HEAD
    assert W == Hg * GD_HEAD and Ts <= SAMPLE_STEPS and Ts >= GD_CONV - 1 and Tp % PROMPT_CHUNK == 0
    n_w, n_a, n_g = rw_w2.shape[1], rw_a2.shape[1], rw_g2.shape[1]
    sw, sa, sg = (_round_up(n, LANES) for n in (n_w, n_a, n_g))
    SW = sw + sa + sg + LANES
    n_mem, xa_heads = cache_mem_k.shape[2], cache_mem_k.shape[3]
    xa_width = xa_heads * cache_mem_k.shape[4]
    alpha = (2 * depth) ** 0.25
    Mp, Ms = Bp * Tp, Bs * SAMPLE_STEPS
    M = Mp + Ms
    tm = _pick_tile(M, (512, 384, 256, 128))
    tf = _pick_tile(ff_wg.shape[-1], (512, 256, 128))
    c_w, c_a, c_g = 3 * W, 3 * W + n_w, 3 * W + n_w + n_a
    c_gd = c_g + n_g

    def to_layout(t):
        parts = [t[..., :c_w], t[..., c_gd:c_gd + 4 * W],
                 _lane_pad(t[..., c_w:c_a], sw), _lane_pad(t[..., c_a:c_g], sa), _lane_pad(t[..., c_g:c_gd], sg),
                 _lane_pad(t[..., c_gd + 4 * W:], LANES)]
        return jnp.concatenate(parts, axis=-1)

    def rwkv_layout(t):
        parts = [t[..., :c_w], _lane_pad(t[..., c_w:c_a], sw), _lane_pad(t[..., c_a:c_g], sa),
                 _lane_pad(t[..., c_g:c_gd], sg + LANES)]
        return jnp.concatenate(parts, axis=-1)

    def rwkv_cols(t):
        s0 = 3 * W
        return jnp.concatenate([t[..., :3 * W], t[..., s0:s0 + n_w], t[..., s0 + sw:s0 + sw + n_a],
                                t[..., s0 + sw + sa:s0 + sw + sa + n_g]], axis=-1)

    w_proj = to_layout(w_in.astype(BF16))
    w_kv = jnp.concatenate([xa_wk, xa_wv], axis=2).astype(BF16)
    w_o, w_q, w_xo = w_out.astype(BF16), xa_wq.astype(BF16), xa_wo.astype(BF16)
    w_g, w_u, w_d = ff_wg.astype(BF16), ff_wu.astype(BF16), ff_wd.astype(BF16)

    x_s = jnp.pad(x_sample, ((0, 0), (0, SAMPLE_STEPS - Ts), (0, 0)))
    x = (x_prompt.reshape(Mp, D), x_s.reshape(Ms, D))
    xb = jnp.concatenate([x[0].astype(BF16), x[1].astype(BF16)], axis=0)
    mem = mem_prompt.reshape(Bp * n_mem, D).astype(BF16)
    mem_k = cache_mem_k.reshape(depth, Bs, n_mem * xa_heads, -1)
    mem_v = cache_mem_v.reshape(depth, Bs, n_mem * xa_heads, -1)
    zeros = lambda *s: jnp.zeros(s, F32)
    tq = _pick_tile(Tp, (512, 256, 128))
    rw_p = gd_p = rw_s = gd_s = None
    rw_p0, gd_p0 = zeros(1, Bp, Hr, RW_HEAD, RW_HEAD), zeros(1, Bp, Hg, GD_HEAD, GD_HEAD)
    conv_at = SUBLANES - (SAMPLE_STEPS - Ts) - (GD_CONV - 1)

    outs = {n: [] for n in ('p_sh', 'p_cv', 'p_mk', 'p_mv', 's_sh', 's_cv')}
    lora = lambda w, rows: jnp.pad(w, ((0, 0), (0, rows - w.shape[1]), (0, 0))).astype(BF16)
    rw = {'T': None, 'mu': rwkv_layout(mu_shift)[:, None],
          'vec': jnp.stack([rw_w0, rw_a0, rw_kk, rw_ka, rw_rk.reshape(depth, W), rw_lnx_w, rw_lnx_b,
                            jnp.zeros((depth, W), F32)], axis=1),
          'w2': lora(rw_w2, sw), 'a2': lora(rw_a2, sa), 'g2': lora(rw_g2, sg)}
    gd = {'T': None, 'SW': SW, 'conv_w': gd_conv_w, 'gate': jnp.stack([gd_a_log, gd_dt_bias], axis=1),
          'norm_w': gd_norm_w[:, None]}
    prev_s = rwkv_layout(state_shift)[:, :, None]
    conv_s = jnp.pad(state_conv, ((0, 0), (0, 0), (SUBLANES - (GD_CONV - 1), 0), (0, 0)))
    prev_p, conv_p = zeros(1, Bp, 1, 3 * W + SW), zeros(1, Bp, SUBLANES, 3 * W)

    for l in range(depth):
        kv = matmul(mem, w_kv, l, tm=_pick_tile(Bp * n_mem, (512, 256, 128)),
                    tn=_pick_tile(2 * xa_width, (512, 256, 128)))
        P = matmul(xb, w_proj, l, tm=_pick_tile(M, (1024, 512, 384, 256, 128)),
                   tn=_pick_tile(7 * W + SW, (768, 512, 256, 128)))

        o_rw_p, rw_p, sh_p = rwkv7_mixer(P, 0, prev_p, rw_p0, 0, rw_p, l, depth, dict(rw, T=Tp),
                                         chunk=PROMPT_CHUNK, seqs=1, chunks=RW_PROMPT_CHUNKS,
                                         t_valid=PROMPT_CHUNK)
        o_gd_p, gd_p, cv_p = gdn_mixer(P, 0, conv_p, gd_p0, 0, gd_p, l, depth, dict(gd, T=Tp),
                                       chunk=PROMPT_CHUNK, seqs=1, chunks=GD_PROMPT_CHUNKS, t_valid=PROMPT_CHUNK)
        o_rw_s, rw_s, sh_s = rwkv7_mixer(P, Mp, prev_s, state_rwkv, l, rw_s, l, depth, dict(rw, T=SAMPLE_STEPS),
                                         chunk=SAMPLE_STEPS, seqs=SAMPLE_SEQS, chunks=1, t_valid=Ts)
        o_gd_s, gd_s, cv_s = gdn_mixer(P, Mp, conv_s, state_gdn, l, gd_s, l, depth, dict(gd, T=SAMPLE_STEPS),
                                       chunk=SAMPLE_STEPS, seqs=SAMPLE_SEQS, chunks=1, t_valid=Ts)

        x, q = mix_out_ln_q((o_rw_p, o_rw_s), (o_gd_p, o_gd_s), w_o, w_q, l, x, ln1_g[l], ln1_b[l],
                            alpha=alpha, tm=tm)

        att_p = cross_attention_prompt(q, kv, n_mem=n_mem, heads=xa_heads, tq=tq, q_tiles=Tp // tq)
        att_s = cross_attention_sample(q, Mp, mem_k, mem_v, l, heads=xa_heads, tq=SAMPLE_STEPS, seqs=SAMPLE_SEQS)
        x, xb = attn_out_ffn((att_p, att_s), w_xo, w_g, w_u, w_d, l, x, ln2_g[l], ln2_b[l], ln3_g[l], ln3_b[l],
                             alpha=alpha, tm=tm, tf=tf)

        outs['p_sh'].append(rwkv_cols(sh_p[:, 0])); outs['s_sh'].append(rwkv_cols(sh_s[:, 0]))
        outs['p_cv'].append(cv_p[:, SUBLANES - (GD_CONV - 1):])
        outs['s_cv'].append(cv_s[:, conv_at:conv_at + GD_CONV - 1])
        outs['p_mk'].append(kv[:, :xa_width].reshape(Bp, n_mem, xa_heads, -1))
        outs['p_mv'].append(kv[:, xa_width:].reshape(Bp, n_mem, xa_heads, -1))

    st = {n: jnp.stack(v) for n, v in outs.items()}
    return (x[:Mp].reshape(Bp, Tp, D), x[Mp:].reshape(Bs, SAMPLE_STEPS, D)[:, :Ts], rw_p, st['p_sh'],
            gd_p, st['p_cv'], st['p_mk'], st['p_mv'], rw_s, st['s_sh'], gd_s, st['s_cv'])
```
